```python
import jax
import jax.numpy as jnp
from jax import lax
import numpy as np

D_MODEL = 1024
BATCH = 4
SEQ = 4096
DEPTH = 1

GRID_W = 64
CTX_LEN = 256
HEAD_DIM = 128
N_Q_HEADS = 8
N_KV_HEADS = 2
Q_PER_KV = N_Q_HEADS // N_KV_HEADS
Q_BLOCK = 128
ROPE_THETA = 10000.0
GLA_HEADS = 4
GLA_DK = D_MODEL // (2 * GLA_HEADS)
GLA_DV = D_MODEL // GLA_HEADS
GLA_GATE_RANK = 16
GLA_GATE_NORM = 16.0
GLA_CHUNK = 64
D_FF = 4 * D_MODEL
EPS = 1e-6

IN_SIZES = (
    N_KV_HEADS * HEAD_DIM,
    N_KV_HEADS * HEAD_DIM,
    GLA_HEADS * GLA_DK,
    GLA_HEADS * GLA_DV,
    GLA_GATE_RANK,
    GLA_GATE_RANK,
    N_Q_HEADS * HEAD_DIM,
    GLA_HEADS * GLA_DK,
    GLA_HEADS * GLA_DV,
    D_MODEL,
    D_MODEL,
)
N_CTX_GROUPS = 6
D_IN = int(sum(IN_SIZES))
IN_OFFSETS = tuple(int(o) for o in np.cumsum(IN_SIZES)[:-1])
CTX_COLS = int(sum(IN_SIZES[:N_CTX_GROUPS]))

kernel_name = 'hybrid_gqa_gla_dit_layer'


def rms_norm(x, g):
    xf = x.astype(jnp.float32)
    y = xf * lax.rsqrt(jnp.mean(xf * xf, axis=-1, keepdims=True) + EPS)
    return (y * g.astype(jnp.float32)).astype(x.dtype)


def modulate(h, shift, scale):
    return h * (1 + scale[..., None, :]) + shift[..., None, :]


def split_heads(a, n_heads, d):
    return a.reshape(*a.shape[:-1], n_heads, d)


def axial_rope_tables(n_rows):
    rows = jnp.repeat(jnp.arange(n_rows), GRID_W).astype(jnp.float32)
    cols = jnp.tile(jnp.arange(GRID_W), n_rows).astype(jnp.float32)
    half = HEAD_DIM // 2
    freqs = ROPE_THETA ** (-jnp.arange(0, half, 2, dtype=jnp.float32) / half)
    ang = jnp.stack([rows[:, None] * freqs, cols[:, None] * freqs], axis=1)
    return jnp.cos(ang)[None, :, None], jnp.sin(ang)[None, :, None]


def apply_rope(x, cos, sin):
    xf = x.astype(jnp.float32).reshape(*x.shape[:-1], 2, 2, HEAD_DIM // 4)
    a, b = xf[..., 0, :], xf[..., 1, :]
    out = jnp.stack([a * cos - b * sin, b * cos + a * sin], axis=-2)
    return out.reshape(x.shape).astype(x.dtype)


def gqa_attention(q, k, v):
    b, t = q.shape[:2]
    qb = jnp.moveaxis(q.reshape(b, t // Q_BLOCK, Q_BLOCK, N_KV_HEADS, Q_PER_KV, HEAD_DIM), 1, 0)

    def one_block(q_blk):
        s = jnp.einsum('bqhgd,bkhd->bhgqk', q_blk, k, preferred_element_type=jnp.float32) * (HEAD_DIM ** -0.5)
        p = jax.nn.softmax(s, axis=-1).astype(v.dtype)
        return jnp.einsum('bhgqk,bkhd->bqhgd', p, v)

    o = lax.map(one_block, qb)
    return jnp.moveaxis(o, 0, 1).reshape(b, t, N_Q_HEADS * HEAD_DIM)


def gla_heads(a, d):
    return jnp.moveaxis(split_heads(a.astype(jnp.float32), GLA_HEADS, d), 2, 1)


def gla_log_decay(lowrank, w_gk, b_gk):
    g = jax.nn.log_sigmoid((lowrank @ w_gk + b_gk).astype(jnp.float32)) / GLA_GATE_NORM
    return gla_heads(g, GLA_DK)


def flip_t(a):
    return a[:, :, ::-1]


def gla_chunked(q, k, v, g, s0):
    b_, h_, t, _ = q.shape
    n = t // GLA_CHUNK

    def to_chunks(a):
        return jnp.moveaxis(a.reshape(b_, h_, n, GLA_CHUNK, a.shape[-1]), 2, 0)

    mask = jnp.tril(jnp.ones((GLA_CHUNK, GLA_CHUNK), dtype=bool))[:, :, None]

    def step(state, inp):
        qc, kc, vc, gc = inp
        b = jnp.cumsum(gc, axis=2)
        o_inter = jnp.einsum('bhid,bhde->bhie', qc * jnp.exp(b), state)
        diff = jnp.where(mask, b[:, :, :, None, :] - b[:, :, None, :, :], -jnp.inf)
        a = jnp.einsum('bhid,bhjd,bhijd->bhij', qc, kc, jnp.exp(diff))
        o = o_inter + jnp.einsum('bhij,bhje->bhie', a, vc)
        b_last = b[:, :, -1:, :]
        state = jnp.exp(b_last[:, :, 0, :])[..., None] * state + jnp.einsum('bhjd,bhje->bhde', kc * jnp.exp(b_last - b), vc)
        return state, o

    s_fin, o = lax.scan(step, s0, (to_chunks(q), to_chunks(k), to_chunks(v), to_chunks(g)))
    o = jnp.moveaxis(o, 0, 2).reshape(b_, h_, t, v.shape[-1])
    return o, s_fin


def gla_final_state(k, v, g):
    b = jnp.cumsum(g, axis=2)
    return jnp.einsum('bhtd,bhte->bhde', k * jnp.exp(b[:, :, -1:] - b), v)


def gla_output(o, gate, g_norm):
    o = jnp.moveaxis(o, 1, 2)
    y = rms_norm(o, g_norm) * jax.nn.silu(split_heads(gate, GLA_HEADS, GLA_DV).astype(jnp.float32))
    return y.reshape(*y.shape[:2], GLA_HEADS * GLA_DV).astype(gate.dtype)


def branch_merge(attn, gla, gate_a, gate_g, w_br_attn, w_br_gla, w_out):
    y = jax.nn.sigmoid(gate_a) * (attn @ w_br_attn) + jax.nn.sigmoid(gate_g) * (gla @ w_br_gla)
    return y @ w_out


def sq_relu_mlp(h, w1, w2):
    return jnp.square(jax.nn.relu(h @ w1)) @ w2


def hybrid_layer(x, ctx, mod_x, mod_c, rope, norm1, w_in, q_norm, k_norm, w_gk_fwd, b_gk_fwd,
                 w_gk_bwd, b_gk_bwd, gla_norm, w_br_attn, w_br_gla, w_out, norm2, w_mlp1, w_mlp2,
                 update_ctx):
    cos, sin = rope
    sh1, sc1, gt1, sh2, sc2, gt2 = mod_x
    bsz = x.shape[0]
    hx = modulate(rms_norm(x, norm1), sh1, sc1)
    hc = modulate(rms_norm(ctx, norm1), mod_c[0], mod_c[1])
    ak, av, gk, gv, lrf, lrb, aq, gq, go, ga, gg = jnp.split(hx @ w_in, IN_OFFSETS, axis=-1)
    if update_ctx:
        ak_c, av_c, gk_c, gv_c, lrf_c, lrb_c, aq_c, gq_c, go_c, ga_c, gg_c = jnp.split(hc @ w_in, IN_OFFSETS, axis=-1)
    else:
        ak_c, av_c, gk_c, gv_c, lrf_c, lrb_c = jnp.split(hc @ w_in[:, :CTX_COLS], IN_OFFSETS[:N_CTX_GROUPS - 1], axis=-1)

    qx = apply_rope(rms_norm(split_heads(aq, N_Q_HEADS, HEAD_DIM), q_norm), cos, sin)
    kx = apply_rope(rms_norm(split_heads(ak, N_KV_HEADS, HEAD_DIM), k_norm), cos, sin)
    vx = split_heads(av, N_KV_HEADS, HEAD_DIM)
    kc = rms_norm(split_heads(ak_c, N_KV_HEADS, HEAD_DIM), k_norm)
    vc = split_heads(av_c, N_KV_HEADS, HEAD_DIM)
    attn_x = gqa_attention(qx, jnp.concatenate([kx, kc], axis=1), jnp.concatenate([vx, vc], axis=1))

    kgc, vgc = gla_heads(gk_c, GLA_DK), gla_heads(gv_c, GLA_DV)
    gfc = gla_log_decay(lrf_c, w_gk_fwd, b_gk_fwd)
    gbc = gla_log_decay(lrb_c, w_gk_bwd, b_gk_bwd)
    if update_ctx:
        zeros = jnp.zeros((bsz, GLA_HEADS, GLA_DK, GLA_DV), jnp.float32)
        qgc = gla_heads(gq_c, GLA_DK) * (GLA_DK ** -0.5)
        oc_f, s_f = gla_chunked(qgc, kgc, vgc, gfc, zeros)
        oc_b, s_b = gla_chunked(flip_t(qgc), flip_t(kgc), flip_t(vgc), flip_t(gbc), zeros)
    else:
        s_f = gla_final_state(kgc, vgc, gfc)
        s_b = gla_final_state(flip_t(kgc), flip_t(vgc), flip_t(gbc))
    qgx = gla_heads(gq, GLA_DK) * (GLA_DK ** -0.5)
    kgx, vgx = gla_heads(gk, GLA_DK), gla_heads(gv, GLA_DV)
    gfx = gla_log_decay(lrf, w_gk_fwd, b_gk_fwd)
    gbx = gla_log_decay(lrb, w_gk_bwd, b_gk_bwd)
    ox_f, _ = gla_chunked(qgx, kgx, vgx, gfx, s_f)
    ox_b, _ = gla_chunked(flip_t(qgx), flip_t(kgx), flip_t(vgx), flip_t(gbx), s_b)
    gla_x = gla_output(ox_f + flip_t(ox_b), go, gla_norm)

    x = x + gt1[..., None, :] * branch_merge(attn_x, gla_x, ga, gg, w_br_attn, w_br_gla, w_out)
    x = x + gt2[..., None, :] * sq_relu_mlp(modulate(rms_norm(x, norm2), sh2, sc2), w_mlp1, w_mlp2)

    if update_ctx:
        _, _, gt1c, sh2c, sc2c, gt2c = mod_c
        qc = rms_norm(split_heads(aq_c, N_Q_HEADS, HEAD_DIM), q_norm)
        attn_c = gqa_attention(qc, kc, vc)
        gla_c = gla_output(oc_f + flip_t(oc_b), go_c, gla_norm)
        ctx = ctx + gt1c[..., None, :] * branch_merge(attn_c, gla_c, ga_c, gg_c, w_br_attn, w_br_gla, w_out)
        ctx = ctx + gt2c[..., None, :] * sq_relu_mlp(modulate(rms_norm(ctx, norm2), sh2c, sc2c), w_mlp1, w_mlp2)
    return x, ctx


def setup_inputs(seed: int = 0) -> dict:
    key = jax.random.key(seed)
    ks = jax.random.split(key, 24)
    f32 = jnp.float32

    def nrm(k, shape, scale=1.0):
        return jax.random.normal(k, shape, f32) * scale

    L, D = DEPTH, D_MODEL
    return {
        'x': nrm(ks[0], (BATCH, SEQ, D)),
        'c': nrm(ks[1], (BATCH, D)),
        'ctx': nrm(ks[2], (BATCH, CTX_LEN, D)),
        'c_ctx': nrm(ks[3], (D,)),
        'w_ada': nrm(ks[4], (L, D, 6 * D), 0.5 * D ** -0.5),
        'b_ada': nrm(ks[5], (L, 6 * D), 0.02),
        'norm1': 1.0 + nrm(ks[6], (L, D), 0.02),
        'w_in': nrm(ks[7], (L, D, D_IN), D ** -0.5),
        'q_norm': 1.0 + nrm(ks[8], (L, HEAD_DIM), 0.02),
        'k_norm': 1.0 + nrm(ks[9], (L, HEAD_DIM), 0.02),
        'w_gk_fwd': nrm(ks[10], (L, GLA_GATE_RANK, GLA_HEADS * GLA_DK), GLA_GATE_RANK ** -0.5),
        'b_gk_fwd': nrm(ks[11], (L, GLA_HEADS * GLA_DK), 0.1),
        'w_gk_bwd': nrm(ks[12], (L, GLA_GATE_RANK, GLA_HEADS * GLA_DK), GLA_GATE_RANK ** -0.5),
        'b_gk_bwd': nrm(ks[13], (L, GLA_HEADS * GLA_DK), 0.1),
        'gla_norm': 1.0 + nrm(ks[14], (L, GLA_DV), 0.02),
        'w_br_attn': nrm(ks[15], (L, N_Q_HEADS * HEAD_DIM, D), (N_Q_HEADS * HEAD_DIM) ** -0.5),
        'w_br_gla': nrm(ks[16], (L, GLA_HEADS * GLA_DV, D), (GLA_HEADS * GLA_DV) ** -0.5),
        'w_out': nrm(ks[17], (L, D, D), D ** -0.5),
        'norm2': 1.0 + nrm(ks[18], (L, D), 0.02),
        'w_mlp1': nrm(ks[19], (L, D, D_FF), D ** -0.5),
        'w_mlp2': nrm(ks[20], (L, D_FF, D), D_FF ** -0.5),
    }


def reference(x, c, ctx, c_ctx, w_ada, b_ada, norm1, w_in, q_norm, k_norm, w_gk_fwd, b_gk_fwd,
              w_gk_bwd, b_gk_bwd, gla_norm, w_br_attn, w_br_gla, w_out, norm2, w_mlp1, w_mlp2):
    ROWS = x.shape[1] // GRID_W
    rope = axial_rope_tables(ROWS)
    for l in range(DEPTH):
        update_ctx = l < DEPTH - 1
        n_mod = 6 if update_ctx else 2
        mod_x = jnp.split(jax.nn.silu(c) @ w_ada[l] + b_ada[l], 6, axis=-1)
        mod_c = jnp.split(jax.nn.silu(c_ctx) @ w_ada[l][:, :n_mod * D_MODEL] + b_ada[l][:n_mod * D_MODEL], n_mod, axis=-1)
        x, ctx = hybrid_layer(x, ctx, mod_x, mod_c, rope, norm1[l], w_in[l], q_norm[l], k_norm[l],
                              w_gk_fwd[l], b_gk_fwd[l], w_gk_bwd[l], b_gk_bwd[l], gla_norm[l],
                              w_br_attn[l], w_br_gla[l], w_out[l], norm2[l], w_mlp1[l], w_mlp2[l],
                              update_ctx)
    return x
```

```python
import functools

import numpy as np
import jax
import jax.numpy as jnp
from jax import lax
from jax.experimental import pallas as pl
from jax.experimental.pallas import tpu as pltpu

F32 = jnp.float32
BF16 = jnp.bfloat16

GRID_W = 64
HEAD_DIM = 128
N_Q_HEADS = 8
N_KV_HEADS = 2
Q_PER_KV = N_Q_HEADS // N_KV_HEADS
ROPE_THETA = 10000.0
GLA_HEADS = 4
GLA_DK = 128
GLA_DV = 256
GLA_GATE_RANK = 16
GLA_GATE_NORM = 16.0
EPS = 1e-6

COL_AK, COL_AV, COL_GK, COL_GV = 0, 256, 512, 1024
COL_AQ, COL_GO, COL_GA, COL_GG, COL_GQ = 2048, 3072, 4096, 5120, 6144
N_MAIN = 6656
N_CTX_MAIN = 2048
LR_PAD = 128
PROJ_CHUNK = 512

VMEM_LIMIT = 56 * 1024 * 1024


def _dot(a, b):
    return jnp.dot(a, b, preferred_element_type=F32)


def _dot_nt(a, b):
    return lax.dot_general(a, b, (((1,), (1,)), ((), ())), preferred_element_type=F32)


def _dot_tn(a, b):
    return lax.dot_general(a, b, (((0,), (0,)), ((), ())), preferred_element_type=F32)


def _sigmoid(x):
    return 1.0 / (1.0 + jnp.exp(-x))


def _resident(shape, index_map):
    return pl.BlockSpec(shape, index_map, pipeline_mode=pl.Buffered(1))


def _ada_kernel(c_ref, w_ref, b_ref, o_ref):
    c = c_ref[...]
    s = (c * _sigmoid(c)).astype(BF16)
    o_ref[...] = _dot(s, w_ref[...].astype(BF16)) + b_ref[...]


def _ada(cc, w_ada, b_ada):
    d, n = w_ada.shape
    tn = 1536
    return pl.pallas_call(
        _ada_kernel,
        grid=(n // tn,),
        in_specs=[pl.BlockSpec((8, d), lambda j: (0, 0)),
                  pl.BlockSpec((d, tn), lambda j: (0, j)),
                  pl.BlockSpec((1, tn), lambda j: (0, j))],
        out_specs=pl.BlockSpec((8, tn), lambda j: (0, j)),
        out_shape=jax.ShapeDtypeStruct((8, n), F32),
        compiler_params=pltpu.CompilerParams(vmem_limit_bytes=VMEM_LIMIT),
        name="ada",
    )(cc, w_ada, b_ada)


def _head_norm(a, gain):
    ms = jnp.mean(a * a, axis=-1, keepdims=True)
    return a * lax.rsqrt(ms + EPS) * gain


def _rope(n, cos, sin_signed):
    lane = lax.broadcasted_iota(jnp.int32, n.shape, 1)
    partner = jnp.where((lane % 64) < 32, pltpu.roll(n, 96, 1), pltpu.roll(n, 32, 1))
    return n * cos + partner * sin_signed


def _inproj_kernel(x_ref, sh_ref, sc_ref, n1_ref, w_ref, wlr_ref, cos_ref, sin_ref, qg_ref, kg_ref,
                   main_ref, lr_ref, *, n_cols, rope):
    x = x_ref[...]
    ms = jnp.mean(x * x, axis=-1, keepdims=True)
    h = x * lax.rsqrt(ms + EPS) * n1_ref[...]
    hb = (h * (1.0 + sc_ref[...]) + sh_ref[...]).astype(BF16)
    lr_ref[...] = _dot(hb, wlr_ref[...])

    def qk_head(a, gain):
        n = _head_norm(a, gain)
        if rope:
            n = _rope(n, cos_ref[...], sin_ref[...])
        return n.astype(BF16)

    for c in range(n_cols // PROJ_CHUNK):
        lo = c * PROJ_CHUNK
        acc = _dot(hb, w_ref[:, lo:lo + PROJ_CHUNK])
        for s in range(PROJ_CHUNK // HEAD_DIM):
            col = lo + s * HEAD_DIM
            a = acc[:, s * HEAD_DIM:(s + 1) * HEAD_DIM]
            if COL_AK <= col < COL_AV:
                main_ref[:, col:col + HEAD_DIM] = qk_head(a, kg_ref[...])
            elif COL_AQ <= col < COL_GO:
                main_ref[:, col:col + HEAD_DIM] = qk_head(a, qg_ref[...])
            else:
                main_ref[:, col:col + HEAD_DIM] = a.astype(BF16)


def _inproj(x2d, mod4, mod_row_of_tile, norm1, w_main, w_lr, cos_t, sin_t, q_gain, k_gain,
            *, n_cols, rope, tm, seq_tiles):
    rows, d = x2d.shape
    kern = functools.partial(_inproj_kernel, n_cols=n_cols, rope=rope)
    return pl.pallas_call(
        kern,
        grid=(rows // tm,),
        in_specs=[
            pl.BlockSpec((tm, d), lambda i: (i, 0)),
            pl.BlockSpec((None, None, 1, d), lambda i: (mod_row_of_tile(i), 0, 0, 0)),
            pl.BlockSpec((None, None, 1, d), lambda i: (mod_row_of_tile(i), 1, 0, 0)),
            _resident((1, d), lambda i: (0, 0)),
            _resident((d, n_cols), lambda i: (0, 0)),
            _resident((d, LR_PAD), lambda i: (0, 0)),
            pl.BlockSpec((tm, HEAD_DIM), lambda i: (i % seq_tiles, 0)),
            pl.BlockSpec((tm, HEAD_DIM), lambda i: (i % seq_tiles, 0)),
            _resident((1, HEAD_DIM), lambda i: (0, 0)),
            _resident((1, HEAD_DIM), lambda i: (0, 0)),
        ],
        out_specs=[pl.BlockSpec((tm, n_cols), lambda i: (i, 0)),
                   pl.BlockSpec((tm, LR_PAD), lambda i: (i, 0))],
        out_shape=[jax.ShapeDtypeStruct((rows, n_cols), BF16),
                   jax.ShapeDtypeStruct((rows, LR_PAD), F32)],
        compiler_params=pltpu.CompilerParams(vmem_limit_bytes=VMEM_LIMIT),
        name="inproj_rope" if rope else "inproj_ctx",
    )(x2d, mod4, mod4, norm1, w_main, w_lr, cos_t, sin_t, q_gain, k_gain)


def _log_decay(lr, w, b):
    z = _dot(lr.astype(BF16), w) + b
    return (jnp.minimum(z, 0.0) - jnp.log(1.0 + jnp.exp(-jnp.abs(z)))) * (1.0 / GLA_GATE_NORM)


def _tri(n, upper):
    r = lax.broadcasted_iota(jnp.int32, (n, n), 0)
    c = lax.broadcasted_iota(jnp.int32, (n, n), 1)
    return (c >= r) if upper else (r >= c)


def _cumsum_rows(tri_bf16, g):
    g_hi = g.astype(BF16)
    g_lo = (g - g_hi.astype(F32)).astype(BF16)
    return _dot(tri_bf16, g_hi) + _dot(tri_bf16, g_lo)


def _glastate_kernel(k_ref, v_ref, lr_ref, wf_ref, bf_ref, wb_ref, bb_ref, sf_ref, sb_ref, *, n):
    k = k_ref[...].astype(F32)
    v = v_ref[...]
    lr = lr_ref[...]
    gf = _log_decay(lr, wf_ref[...], bf_ref[...])
    b = _cumsum_rows(_tri(n, False).astype(BF16), gf)
    sf_ref[...] = _dot_tn(v, (k * jnp.exp(b[n - 1:n] - b)).astype(BF16))
    gb = _log_decay(lr, wb_ref[...], bb_ref[...])
    e = _cumsum_rows(_tri(n, True).astype(BF16), gb)
    sb_ref[...] = _dot_tn(v, (k * jnp.exp(e[0:1] - e)).astype(BF16))


def _glastate(ctx_main, ctx_lr, wf, bf, wb, bb, *, batch, n_ctx):
    kb = COL_GK // GLA_DK
    vb = COL_GV // GLA_DV
    st = jax.ShapeDtypeStruct((batch, GLA_HEADS, GLA_DV, GLA_DK), F32)
    st_spec = pl.BlockSpec((None, None, GLA_DV, GLA_DK), lambda b, h: (b, h, 0, 0))
    w_spec = pl.BlockSpec((LR_PAD, GLA_DK), lambda b, h: (0, h))
    b_spec = pl.BlockSpec((1, GLA_DK), lambda b, h: (0, h))
    return pl.pallas_call(
        functools.partial(_glastate_kernel, n=n_ctx),
        grid=(batch, GLA_HEADS),
        in_specs=[pl.BlockSpec((n_ctx, GLA_DK), lambda b, h: (b, kb + h)),
                  pl.BlockSpec((n_ctx, GLA_DV), lambda b, h: (b, vb + h)),
                  pl.BlockSpec((n_ctx, LR_PAD), lambda b, h: (b, 0)),
                  w_spec, b_spec, w_spec, b_spec],
        out_specs=[st_spec, st_spec],
        out_shape=[st, st],
        compiler_params=pltpu.CompilerParams(vmem_limit_bytes=VMEM_LIMIT),
        name="gla_ctx_state",
    )(ctx_main, ctx_main, ctx_lr, wf, bf, wb, bb)


GLA_CHUNK = 64
GLA_TBLOCK = 512


def _gla_chunk(g, q, k, v, tri, st_ref, edge):
    c = g.shape[0]
    b = _cumsum_rows(tri.astype(BF16), g)
    b_mid = b[c // 2:c // 2 + 1]
    b_edge = b[edge:edge + 1]
    qf = q.astype(F32) * (GLA_DK ** -0.5)
    kf = k.astype(F32)
    qt = (qf * jnp.exp(b - b_mid)).astype(BF16)
    kt = (kf * jnp.exp(b_mid - b)).astype(BF16)
    a = jnp.where(tri, _dot_nt(qt, kt), 0.0).astype(BF16)
    st = st_ref[...]
    o = _dot(a, v) + _dot_nt((qf * jnp.exp(b)).astype(BF16), st.astype(BF16))
    khat = (kf * jnp.exp(b_edge - b)).astype(BF16)
    st_ref[...] = st * jnp.exp(b_edge) + _dot_tn(v, khat)
    return o


def _gla_kernel(qf_ref, kf_ref, vf_ref, lrf_ref, qb_ref, kb_ref, vb_ref, lrb_ref,
                wf_ref, bf_ref, wb_ref, bb_ref, sf0_ref, sb0_ref, of_ref, ob_ref, stf, stb):
    @pl.when(pl.program_id(2) == 0)
    def _():
        stf[...] = sf0_ref[...]
        stb[...] = sb0_ref[...]

    c = GLA_CHUNK
    n = GLA_TBLOCK // c
    lower = _tri(c, False)
    upper = _tri(c, True)
    gf = _log_decay(lrf_ref[...], wf_ref[...], bf_ref[...])
    gb = _log_decay(lrb_ref[...], wb_ref[...], bb_ref[...])
    for i in range(n):
        f = slice(i * c, (i + 1) * c)
        of_ref[f, :] = _gla_chunk(gf[f], qf_ref[f, :], kf_ref[f, :], vf_ref[f, :], lower, stf, c - 1)
        r = slice((n - 1 - i) * c, (n - i) * c)
        ob_ref[r, :] = _gla_chunk(gb[r], qb_ref[r, :], kb_ref[r, :], vb_ref[r, :], upper, stb, 0)


def _gla(main, lr, wf, bf, wb, bb, sf0, sb0, *, batch, seq):
    nt = seq // GLA_TBLOCK
    tb = GLA_TBLOCK
    qb_, kb_, vb_ = COL_GQ // GLA_DK, COL_GK // GLA_DK, COL_GV // GLA_DV

    def fwd(col, per_head=True):
        return lambda b, h, t: (b * nt + t, col + (h if per_head else 0))

    def bwd(col, per_head=True):
        return lambda b, h, t: (b * nt + nt - 1 - t, col + (h if per_head else 0))

    def seq_specs(m):
        return [pl.BlockSpec((tb, GLA_DK), m(qb_)), pl.BlockSpec((tb, GLA_DK), m(kb_)),
                pl.BlockSpec((tb, GLA_DV), m(vb_)), pl.BlockSpec((tb, LR_PAD), m(0, False))]

    w_spec = pl.BlockSpec((LR_PAD, GLA_DK), lambda b, h, t: (0, h))
    b_spec = pl.BlockSpec((1, GLA_DK), lambda b, h, t: (0, h))
    st_spec = pl.BlockSpec((None, None, GLA_DV, GLA_DK), lambda b, h, t: (b, h, 0, 0))
    out = jax.ShapeDtypeStruct((batch * seq, GLA_HEADS * GLA_DV), F32)
    return pl.pallas_call(
        _gla_kernel,
        grid=(batch, GLA_HEADS, nt),
        in_specs=seq_specs(fwd) + seq_specs(bwd) + [w_spec, b_spec, w_spec, b_spec, st_spec, st_spec],
        out_specs=[pl.BlockSpec((tb, GLA_DV), fwd(0)), pl.BlockSpec((tb, GLA_DV), bwd(0))],
        out_shape=[out, out],
        scratch_shapes=[pltpu.VMEM((GLA_DV, GLA_DK), F32), pltpu.VMEM((GLA_DV, GLA_DK), F32)],
        compiler_params=pltpu.CompilerParams(
            dimension_semantics=("arbitrary", "arbitrary", "arbitrary"), vmem_limit_bytes=VMEM_LIMIT),
        name="gla",
    )(main, main, main, lr, main, main, main, lr, wf, bf, wb, bb, sf0, sb0)


ATT_TQ = 256
ATT_TK = 512


def _attn_kernel(q_ref, k_ref, v_ref, kc_ref, vc_ref, o_ref, m_sc, l_sc, acc_sc, *, seq):
    tq = ATT_TQ
    q4 = jnp.concatenate([q_ref[:, g * HEAD_DIM:(g + 1) * HEAD_DIM] for g in range(Q_PER_KV)], axis=0)
    m_sc[...] = jnp.full(m_sc.shape, -jnp.inf, F32)
    l_sc[...] = jnp.zeros(l_sc.shape, F32)
    acc_sc[...] = jnp.zeros(acc_sc.shape, F32)

    def step(kc, vc):
        s = _dot_nt(q4, kc)
        m_prev = m_sc[...]
        m_new = jnp.maximum(m_prev, jnp.max(s, axis=-1, keepdims=True))
        alpha = jnp.exp(m_prev - m_new)
        p = jnp.exp(s - m_new)
        l_sc[...] = alpha * l_sc[...] + jnp.sum(p, axis=-1, keepdims=True)
        acc_sc[...] = alpha * acc_sc[...] + _dot(p.astype(BF16), vc)
        m_sc[...] = m_new

    def body(j, carry):
        off = pl.multiple_of(j * ATT_TK, ATT_TK)
        step(k_ref[pl.ds(off, ATT_TK), :], v_ref[pl.ds(off, ATT_TK), :])
        return carry

    lax.fori_loop(0, seq // ATT_TK, body, 0)
    step(kc_ref[...], vc_ref[...])
    o = acc_sc[...] / l_sc[...]
    for g in range(Q_PER_KV):
        o_ref[:, g * HEAD_DIM:(g + 1) * HEAD_DIM] = o[g * tq:(g + 1) * tq].astype(BF16)


def _attn(main, ctx_main, *, batch, seq, n_ctx):
    nq = seq // ATT_TQ
    gw = Q_PER_KV * HEAD_DIM
    rows = Q_PER_KV * ATT_TQ
    return pl.pallas_call(
        functools.partial(_attn_kernel, seq=seq),
        grid=(batch, N_KV_HEADS, nq),
        in_specs=[pl.BlockSpec((ATT_TQ, gw), lambda b, h, i: (b * nq + i, COL_AQ // gw + h)),
                  pl.BlockSpec((seq, HEAD_DIM), lambda b, h, i: (b, COL_AK // HEAD_DIM + h)),
                  pl.BlockSpec((seq, HEAD_DIM), lambda b, h, i: (b, COL_AV // HEAD_DIM + h)),
                  pl.BlockSpec((n_ctx, HEAD_DIM), lambda b, h, i: (b, COL_AK // HEAD_DIM + h)),
                  pl.BlockSpec((n_ctx, HEAD_DIM), lambda b, h, i: (b, COL_AV // HEAD_DIM + h))],
        out_specs=pl.BlockSpec((ATT_TQ, gw), lambda b, h, i: (b * nq + i, h)),
        out_shape=jax.ShapeDtypeStruct((batch * seq, N_Q_HEADS * HEAD_DIM), BF16),
        scratch_shapes=[pltpu.VMEM((rows, 1), F32), pltpu.VMEM((rows, 1), F32),
                        pltpu.VMEM((rows, HEAD_DIM), F32)],
        compiler_params=pltpu.CompilerParams(vmem_limit_bytes=VMEM_LIMIT),
        name="attn",
    )(main, main, main, ctx_main, ctx_main)


def _merge_kernel(x_ref, attn_ref, of_ref, ob_ref, go_ref, ga_ref, gg_ref, gn_ref, gt_ref,
                  wa_ref, wg_ref, wo_ref, o_ref):
    gn = gn_ref[...]
    heads = []
    for h in range(GLA_HEADS):
        s = slice(h * GLA_DV, (h + 1) * GLA_DV)
        o = of_ref[:, s] + ob_ref[:, s]
        y = _head_norm(o, gn)
        gate = go_ref[:, s].astype(F32)
        heads.append((y * (gate * _sigmoid(gate))).astype(BF16))
    gla = jnp.concatenate(heads, axis=-1)
    ya = _sigmoid(ga_ref[...].astype(F32)) * _dot(attn_ref[...], wa_ref[...])
    yg = _sigmoid(gg_ref[...].astype(F32)) * _dot(gla, wg_ref[...])
    out = _dot((ya + yg).astype(BF16), wo_ref[...])
    o_ref[...] = x_ref[...] + gt_ref[...] * out


def _merge(x2d, attn, o_f, o_b, main, gla_norm, mod4, wa, wg, wo, *, tm, seq_tiles):
    rows, d = x2d.shape
    row = lambda i: (i, 0)
    return pl.pallas_call(
        _merge_kernel,
        grid=(rows // tm,),
        in_specs=[pl.BlockSpec((tm, d), row), pl.BlockSpec((tm, d), row),
                  pl.BlockSpec((tm, d), row), pl.BlockSpec((tm, d), row),
                  pl.BlockSpec((tm, d), lambda i: (i, COL_GO // d)),
                  pl.BlockSpec((tm, d), lambda i: (i, COL_GA // d)),
                  pl.BlockSpec((tm, d), lambda i: (i, COL_GG // d)),
                  _resident((1, GLA_DV), lambda i: (0, 0)),
                  pl.BlockSpec((None, None, 1, d), lambda i: (i // seq_tiles, 2, 0, 0)),
                  _resident((d, d), lambda i: (0, 0)), _resident((d, d), lambda i: (0, 0)),
                  _resident((d, d), lambda i: (0, 0))],
        out_specs=pl.BlockSpec((tm, d), row),
        out_shape=jax.ShapeDtypeStruct((rows, d), F32),
        compiler_params=pltpu.CompilerParams(vmem_limit_bytes=VMEM_LIMIT),
        name="merge_out",
    )(x2d, attn, o_f, o_b, main, main, main, gla_norm, mod4, wa, wg, wo)


FF_CHUNK = 1024


def _mlp_kernel(x_ref, sh_ref, sc_ref, gt_ref, n2_ref, w1_ref, w2_ref, o_ref, *, d_ff):
    x = x_ref[...]
    ms = jnp.mean(x * x, axis=-1, keepdims=True)
    h = x * lax.rsqrt(ms + EPS) * n2_ref[...]
    hb = (h * (1.0 + sc_ref[...]) + sh_ref[...]).astype(BF16)
    acc = jnp.zeros(x.shape, F32)
    for c in range(d_ff // FF_CHUNK):
        s = slice(c * FF_CHUNK, (c + 1) * FF_CHUNK)
        u = jnp.maximum(_dot(hb, w1_ref[:, s]), 0.0)
        acc = acc + _dot((u * u).astype(BF16), w2_ref[s, :])
    o_ref[...] = x + gt_ref[...] * acc


def _mlp(x2d, mod4, norm2, w1, w2, *, tm, seq_tiles):
    rows, d = x2d.shape
    d_ff = w1.shape[1]
    mod = lambda g: pl.BlockSpec((None, None, 1, d), lambda i: (i // seq_tiles, g, 0, 0))
    return pl.pallas_call(
        functools.partial(_mlp_kernel, d_ff=d_ff),
        grid=(rows // tm,),
        in_specs=[pl.BlockSpec((tm, d), lambda i: (i, 0)), mod(3), mod(4), mod(5),
                  _resident((1, d), lambda i: (0, 0)),
                  _resident((d, d_ff), lambda i: (0, 0)), _resident((d_ff, d), lambda i: (0, 0))],
        out_specs=pl.BlockSpec((tm, d), lambda i: (i, 0)),
        out_shape=jax.ShapeDtypeStruct((rows, d), F32),
        compiler_params=pltpu.CompilerParams(vmem_limit_bytes=VMEM_LIMIT),
        name="mlp",
    )(x2d, mod4, mod4, mod4, norm2, w1, w2)


def _rope_tables(seq):
    t = np.arange(seq)
    half = HEAD_DIM // 2
    freqs = ROPE_THETA ** (-np.arange(0, half, 2, dtype=np.float32) / half)
    ang_r = (t // GRID_W).astype(np.float32)[:, None] * freqs
    ang_c = (t % GRID_W).astype(np.float32)[:, None] * freqs
    ang_r, ang_c = jnp.asarray(ang_r, F32), jnp.asarray(ang_c, F32)
    cos = jnp.concatenate([jnp.cos(ang_r)] * 2 + [jnp.cos(ang_c)] * 2, axis=-1)
    sin = jnp.concatenate([-jnp.sin(ang_r), jnp.sin(ang_r), -jnp.sin(ang_c), jnp.sin(ang_c)], axis=-1)
    return cos, sin


def _pad_rows(w, row0):
    return jnp.pad(w, ((row0, LR_PAD - row0 - w.shape[0]), (0, 0)))


def kernel(x, c, ctx, c_ctx, w_ada, b_ada, norm1, w_in, q_norm, k_norm, w_gk_fwd, b_gk_fwd,
           w_gk_bwd, b_gk_bwd, gla_norm, w_br_attn, w_br_gla, w_out, norm2, w_mlp1, w_mlp2):
    batch, seq, d = x.shape
    n_ctx = ctx.shape[1]
    depth = w_ada.shape[0]
    assert depth == 1 and seq % GLA_TBLOCK == 0 and seq % ATT_TK == 0 and batch <= 7
    tm = 512
    seq_tiles = seq // tm
    ctx_tm = n_ctx
    l = 0

    cc = jnp.zeros((8, d), F32).at[:batch].set(c).at[batch].set(c_ctx)
    mod4 = _ada(cc, w_ada[l], b_ada[l][None, :]).reshape(8, 6, 1, d)

    w = w_in[l]
    w_main = jnp.concatenate(
        [w[:, :2048], w[:, 2080:3104], w[:, 3616:4640], w[:, 4640:5664], w[:, 5664:6688], w[:, 3104:3616]],
        axis=1).astype(BF16)
    w_lr = jnp.pad(w[:, 2048:2080], ((0, 0), (0, LR_PAD - 2 * GLA_GATE_RANK))).astype(BF16)
    cos_t, sin_t = _rope_tables(seq)
    q_gain = (q_norm[l] * (HEAD_DIM ** -0.5))[None, :]
    k_gain = k_norm[l][None, :]
    n1 = norm1[l][None, :]

    main, lr = _inproj(x.reshape(batch * seq, d), mod4, lambda i: i // seq_tiles, n1, w_main, w_lr,
                       cos_t, sin_t, q_gain, k_gain, n_cols=N_MAIN, rope=True, tm=tm, seq_tiles=seq_tiles)
    ctx_main, ctx_lr = _inproj(ctx.reshape(batch * n_ctx, d), mod4, lambda i: batch, n1, w_main, w_lr,
                               cos_t, sin_t, q_gain, k_gain, n_cols=N_CTX_MAIN, rope=False, tm=ctx_tm,
                               seq_tiles=1)

    wf = _pad_rows(w_gk_fwd[l], 0).astype(BF16)
    wb = _pad_rows(w_gk_bwd[l], GLA_GATE_RANK).astype(BF16)
    bf, bb = b_gk_fwd[l][None, :], b_gk_bwd[l][None, :]
    sf0, sb0 = _glastate(ctx_main, ctx_lr, wf, bf, wb, bb, batch=batch, n_ctx=n_ctx)
    o_f, o_b = _gla(main, lr, wf, bf, wb, bb, sf0, sb0, batch=batch, seq=seq)

    attn = _attn(main, ctx_main, batch=batch, seq=seq, n_ctx=n_ctx)

    x1 = _merge(x.reshape(batch * seq, d), attn, o_f, o_b, main, gla_norm[l][None, :], mod4,
                w_br_attn[l].astype(BF16), w_br_gla[l].astype(BF16), w_out[l].astype(BF16),
                tm=tm, seq_tiles=seq_tiles)
    x2 = _mlp(x1, mod4, norm2[l][None, :], w_mlp1[l].astype(BF16), w_mlp2[l].astype(BF16),
              tm=tm, seq_tiles=seq_tiles)
    return x2.reshape(batch, seq, d)
```

```python
import functools

import numpy as np
import jax
import jax.numpy as jnp
from jax import lax
from jax.experimental import pallas as pl
from jax.experimental.pallas import tpu as pltpu

F32 = jnp.float32
BF16 = jnp.bfloat16

GRID_W = 64
HEAD_DIM = 128
N_Q_HEADS = 8
N_KV_HEADS = 2
Q_PER_KV = N_Q_HEADS // N_KV_HEADS
ROPE_THETA = 10000.0
GLA_HEADS = 4
GLA_DK = 128
GLA_DV = 256
GLA_GATE_RANK = 16
GLA_GATE_NORM = 16.0
EPS = 1e-6

COL_AK, COL_AV, COL_GK, COL_GV = 0, 256, 512, 1024
COL_AQ, COL_GO, COL_GA, COL_GG, COL_GQ = 2048, 3072, 4096, 5120, 6144
N_MAIN = 6656
N_CTX_MAIN = 2048
LR_PAD = 128
PROJ_CHUNK = 512

VMEM_LIMIT = 56 * 1024 * 1024


def _dot(a, b):
    return jnp.dot(a, b, preferred_element_type=F32)


def _dot_nt(a, b):
    return lax.dot_general(a, b, (((1,), (1,)), ((), ())), preferred_element_type=F32)


def _dot_tn(a, b):
    return lax.dot_general(a, b, (((0,), (0,)), ((), ())), preferred_element_type=F32)


def _sigmoid(x):
    return 1.0 / (1.0 + jnp.exp(-x))


def _resident(shape, index_map):
    return pl.BlockSpec(shape, index_map, pipeline_mode=pl.Buffered(1))


def _ada_kernel(c_ref, w_ref, b_ref, o_ref):
    c = c_ref[...]
    s = (c * _sigmoid(c)).astype(BF16)
    o_ref[...] = _dot(s, w_ref[...].astype(BF16)) + b_ref[...]


def _ada(cc, w_ada, b_ada):
    d, n = w_ada.shape
    tn = 1536
    return pl.pallas_call(
        _ada_kernel,
        grid=(n // tn,),
        in_specs=[pl.BlockSpec((8, d), lambda j: (0, 0)),
                  pl.BlockSpec((d, tn), lambda j: (0, j)),
                  pl.BlockSpec((1, tn), lambda j: (0, j))],
        out_specs=pl.BlockSpec((8, tn), lambda j: (0, j)),
        out_shape=jax.ShapeDtypeStruct((8, n), F32),
        compiler_params=pltpu.CompilerParams(vmem_limit_bytes=VMEM_LIMIT),
        name="ada",
    )(cc, w_ada, b_ada)


def _head_norm(a, gain):
    ms = jnp.mean(a * a, axis=-1, keepdims=True)
    return a * lax.rsqrt(ms + EPS) * gain


def _rope(n, cos, sin_signed):
    lane = lax.broadcasted_iota(jnp.int32, n.shape, 1)
    partner = jnp.where((lane % 64) < 32, pltpu.roll(n, 96, 1), pltpu.roll(n, 32, 1))
    return n * cos + partner * sin_signed


def _inproj_kernel(x_ref, sh_ref, sc_ref, n1_ref, w_ref, wlr_ref, cos_ref, sin_ref, qg_ref, kg_ref,
                   main_ref, lr_ref, *, n_cols, rope):
    x = x_ref[...]
    ms = jnp.mean(x * x, axis=-1, keepdims=True)
    h = x * lax.rsqrt(ms + EPS) * n1_ref[...]
    hb = (h * (1.0 + sc_ref[...]) + sh_ref[...]).astype(BF16)
    lr_ref[...] = _dot(hb, wlr_ref[...])

    def qk_head(a, gain):
        n = _head_norm(a, gain)
        if rope:
            n = _rope(n, cos_ref[...], sin_ref[...])
        return n.astype(BF16)

    for c in range(n_cols // PROJ_CHUNK):
        lo = c * PROJ_CHUNK
        acc = _dot(hb, w_ref[:, lo:lo + PROJ_CHUNK])
        for s in range(PROJ_CHUNK // HEAD_DIM):
            col = lo + s * HEAD_DIM
            a = acc[:, s * HEAD_DIM:(s + 1) * HEAD_DIM]
            if COL_AK <= col < COL_AV:
                main_ref[:, col:col + HEAD_DIM] = qk_head(a, kg_ref[...])
            elif COL_AQ <= col < COL_GO:
                main_ref[:, col:col + HEAD_DIM] = qk_head(a, qg_ref[...])
            else:
                main_ref[:, col:col + HEAD_DIM] = a.astype(BF16)


def _inproj(x2d, mod4, mod_row_of_tile, norm1, w_main, w_lr, cos_t, sin_t, q_gain, k_gain,
            *, n_cols, rope, tm, seq_tiles):
    rows, d = x2d.shape
    kern = functools.partial(_inproj_kernel, n_cols=n_cols, rope=rope)
    return pl.pallas_call(
        kern,
        grid=(rows // tm,),
        in_specs=[
            pl.BlockSpec((tm, d), lambda i: (i, 0)),
            pl.BlockSpec((None, None, 1, d), lambda i: (mod_row_of_tile(i), 0, 0, 0)),
            pl.BlockSpec((None, None, 1, d), lambda i: (mod_row_of_tile(i), 1, 0, 0)),
            _resident((1, d), lambda i: (0, 0)),
            _resident((d, n_cols), lambda i: (0, 0)),
            _resident((d, LR_PAD), lambda i: (0, 0)),
            pl.BlockSpec((tm, HEAD_DIM), lambda i: (i % seq_tiles, 0)),
            pl.BlockSpec((tm, HEAD_DIM), lambda i: (i % seq_tiles, 0)),
            _resident((1, HEAD_DIM), lambda i: (0, 0)),
            _resident((1, HEAD_DIM), lambda i: (0, 0)),
        ],
        out_specs=[pl.BlockSpec((tm, n_cols), lambda i: (i, 0)),
                   pl.BlockSpec((tm, LR_PAD), lambda i: (i, 0))],
        out_shape=[jax.ShapeDtypeStruct((rows, n_cols), BF16),
                   jax.ShapeDtypeStruct((rows, LR_PAD), F32)],
        compiler_params=pltpu.CompilerParams(vmem_limit_bytes=VMEM_LIMIT),
        name="inproj_rope" if rope else "inproj_ctx",
    )(x2d, mod4, mod4, norm1, w_main, w_lr, cos_t, sin_t, q_gain, k_gain)


def _log_decay(lr, w, b):
    z = _dot(lr.astype(BF16), w) + b
    return (jnp.minimum(z, 0.0) - jnp.log(1.0 + jnp.exp(-jnp.abs(z)))) * (1.0 / GLA_GATE_NORM)


def _tri(n, upper):
    r = lax.broadcasted_iota(jnp.int32, (n, n), 0)
    c = lax.broadcasted_iota(jnp.int32, (n, n), 1)
    return (c >= r) if upper else (r >= c)


def _cumsum_rows(tri_bf16, g):
    g_hi = g.astype(BF16)
    g_lo = (g - g_hi.astype(F32)).astype(BF16)
    return _dot(tri_bf16, g_hi) + _dot(tri_bf16, g_lo)


def _glastate_kernel(k_ref, v_ref, lr_ref, wf_ref, bf_ref, wb_ref, bb_ref, sf_ref, sb_ref, *, n):
    k = k_ref[...].astype(F32)
    v = v_ref[...]
    lr = lr_ref[...]
    gf = _log_decay(lr, wf_ref[...], bf_ref[...])
    b = _cumsum_rows(_tri(n, False).astype(BF16), gf)
    sf_ref[...] = _dot_tn(v, (k * jnp.exp(b[n - 1:n] - b)).astype(BF16))
    gb = _log_decay(lr, wb_ref[...], bb_ref[...])
    e = _cumsum_rows(_tri(n, True).astype(BF16), gb)
    sb_ref[...] = _dot_tn(v, (k * jnp.exp(e[0:1] - e)).astype(BF16))


def _glastate(ctx_main, ctx_lr, wf, bf, wb, bb, *, batch, n_ctx):
    kb = COL_GK // GLA_DK
    vb = COL_GV // GLA_DV
    st = jax.ShapeDtypeStruct((batch, GLA_HEADS, GLA_DV, GLA_DK), F32)
    st_spec = pl.BlockSpec((None, None, GLA_DV, GLA_DK), lambda b, h: (b, h, 0, 0))
    w_spec = pl.BlockSpec((LR_PAD, GLA_DK), lambda b, h: (0, h))
    b_spec = pl.BlockSpec((1, GLA_DK), lambda b, h: (0, h))
    return pl.pallas_call(
        functools.partial(_glastate_kernel, n=n_ctx),
        grid=(batch, GLA_HEADS),
        in_specs=[pl.BlockSpec((n_ctx, GLA_DK), lambda b, h: (b, kb + h)),
                  pl.BlockSpec((n_ctx, GLA_DV), lambda b, h: (b, vb + h)),
                  pl.BlockSpec((n_ctx, LR_PAD), lambda b, h: (b, 0)),
                  w_spec, b_spec, w_spec, b_spec],
        out_specs=[st_spec, st_spec],
        out_shape=[st, st],
        compiler_params=pltpu.CompilerParams(vmem_limit_bytes=VMEM_LIMIT),
        name="gla_ctx_state",
    )(ctx_main, ctx_main, ctx_lr, wf, bf, wb, bb)


GLA_CHUNK = 64
GLA_TBLOCK = 512


def _gla_chunk(g, q, k, v, tri, st_ref, edge):
    c = g.shape[0]
    b = _cumsum_rows(tri.astype(BF16), g)
    b_mid = b[c // 2:c // 2 + 1]
    b_edge = b[edge:edge + 1]
    qf = q.astype(F32) * (GLA_DK ** -0.5)
    kf = k.astype(F32)
    qt = (qf * jnp.exp(b - b_mid)).astype(BF16)
    kt = (kf * jnp.exp(b_mid - b)).astype(BF16)
    a = jnp.where(tri, _dot_nt(qt, kt), 0.0).astype(BF16)
    st = st_ref[...]
    o = _dot(a, v) + _dot_nt((qf * jnp.exp(b)).astype(BF16), st.astype(BF16))
    khat = (kf * jnp.exp(b_edge - b)).astype(BF16)
    st_ref[...] = st * jnp.exp(b_edge) + _dot_tn(v, khat)
    return o


def _gla_kernel(qf_ref, kf_ref, vf_ref, lrf_ref, qb_ref, kb_ref, vb_ref, lrb_ref,
                wf_ref, bf_ref, wb_ref, bb_ref, sf0_ref, sb0_ref, of_ref, ob_ref, stf, stb):
    @pl.when(pl.program_id(2) == 0)
    def _():
        stf[...] = sf0_ref[...]
        stb[...] = sb0_ref[...]

    c = GLA_CHUNK
    n = GLA_TBLOCK // c
    lower = _tri(c, False)
    upper = _tri(c, True)
    gf = _log_decay(lrf_ref[...], wf_ref[...], bf_ref[...])
    gb = _log_decay(lrb_ref[...], wb_ref[...], bb_ref[...])
    for i in range(n):
        f = slice(i * c, (i + 1) * c)
        of_ref[f, :] = _gla_chunk(gf[f], qf_ref[f, :], kf_ref[f, :], vf_ref[f, :], lower, stf, c - 1)
        r = slice((n - 1 - i) * c, (n - i) * c)
        ob_ref[r, :] = _gla_chunk(gb[r], qb_ref[r, :], kb_ref[r, :], vb_ref[r, :], upper, stb, 0)


def _gla(main, lr, wf, bf, wb, bb, sf0, sb0, *, batch, seq):
    nt = seq // GLA_TBLOCK
    tb = GLA_TBLOCK
    qb_, kb_, vb_ = COL_GQ // GLA_DK, COL_GK // GLA_DK, COL_GV // GLA_DV

    def fwd(col, per_head=True):
        return lambda b, h, t: (b * nt + t, col + (h if per_head else 0))

    def bwd(col, per_head=True):
        return lambda b, h, t: (b * nt + nt - 1 - t, col + (h if per_head else 0))

    def seq_specs(m):
        return [pl.BlockSpec((tb, GLA_DK), m(qb_)), pl.BlockSpec((tb, GLA_DK), m(kb_)),
                pl.BlockSpec((tb, GLA_DV), m(vb_)), pl.BlockSpec((tb, LR_PAD), m(0, False))]

    w_spec = pl.BlockSpec((LR_PAD, GLA_DK), lambda b, h, t: (0, h))
    b_spec = pl.BlockSpec((1, GLA_DK), lambda b, h, t: (0, h))
    st_spec = pl.BlockSpec((None, None, GLA_DV, GLA_DK), lambda b, h, t: (b, h, 0, 0))
    out = jax.ShapeDtypeStruct((batch * seq, GLA_HEADS * GLA_DV), F32)
    return pl.pallas_call(
        _gla_kernel,
        grid=(batch, GLA_HEADS, nt),
        in_specs=seq_specs(fwd) + seq_specs(bwd) + [w_spec, b_spec, w_spec, b_spec, st_spec, st_spec],
        out_specs=[pl.BlockSpec((tb, GLA_DV), fwd(0)), pl.BlockSpec((tb, GLA_DV), bwd(0))],
        out_shape=[out, out],
        scratch_shapes=[pltpu.VMEM((GLA_DV, GLA_DK), F32), pltpu.VMEM((GLA_DV, GLA_DK), F32)],
        compiler_params=pltpu.CompilerParams(
            dimension_semantics=("arbitrary", "arbitrary", "arbitrary"), vmem_limit_bytes=VMEM_LIMIT),
        name="gla",
    )(main, main, main, lr, main, main, main, lr, wf, bf, wb, bb, sf0, sb0)


ATT_TQ = 256
ATT_TK = 512


ATT_UNSHIFTED_MAX_LOG2 = 56.0


def _attn_kernel(bound_ref, q_ref, k_ref, v_ref, kc_ref, vc_ref, o_ref, m_sc, acc_sc, *, seq):
    tq = ATT_TQ
    q4 = jnp.concatenate([q_ref[:, g * HEAD_DIM:(g + 1) * HEAD_DIM] for g in range(Q_PER_KV)], axis=0)
    acc_sc[...] = jnp.zeros(acc_sc.shape, F32)

    def with_ones(vc):
        return jnp.concatenate([vc, jnp.ones(vc.shape, BF16)], axis=-1)

    def over_keys(step):
        def body(j, carry):
            off = pl.multiple_of(j * ATT_TK, ATT_TK)
            step(k_ref[pl.ds(off, ATT_TK), :], v_ref[pl.ds(off, ATT_TK), :])
            return carry

        lax.fori_loop(0, seq // ATT_TK, body, 0)
        step(kc_ref[...], vc_ref[...])

    def unshifted_step(kc, vc):
        p = jnp.exp2(_dot_nt(q4, kc)).astype(BF16)
        acc_sc[...] += _dot(p, with_ones(vc))

    def online_step(kc, vc):
        s = _dot_nt(q4, kc)
        m_prev = m_sc[...]
        m_new = jnp.maximum(m_prev, jnp.max(s, axis=-1, keepdims=True))
        p = jnp.exp2(s - m_new).astype(BF16)
        acc_sc[...] = jnp.exp2(m_prev - m_new) * acc_sc[...] + _dot(p, with_ones(vc))
        m_sc[...] = m_new

    small = bound_ref[0, 0] <= ATT_UNSHIFTED_MAX_LOG2

    @pl.when(small)
    def _():
        over_keys(unshifted_step)

    @pl.when(jnp.logical_not(small))
    def _():
        m_sc[...] = jnp.full(m_sc.shape, -jnp.inf, F32)
        over_keys(online_step)

    acc = acc_sc[...]
    o = acc[:, :HEAD_DIM] / acc[:, HEAD_DIM:]
    for g in range(Q_PER_KV):
        o_ref[:, g * HEAD_DIM:(g + 1) * HEAD_DIM] = o[g * tq:(g + 1) * tq].astype(BF16)


def _attn(score_bound, main, ctx_main, *, batch, seq, n_ctx):
    nq = seq // ATT_TQ
    gw = Q_PER_KV * HEAD_DIM
    rows = Q_PER_KV * ATT_TQ
    return pl.pallas_call(
        functools.partial(_attn_kernel, seq=seq),
        grid=(batch, N_KV_HEADS, nq),
        in_specs=[pl.BlockSpec(memory_space=pltpu.SMEM),
                  pl.BlockSpec((ATT_TQ, gw), lambda b, h, i: (b * nq + i, COL_AQ // gw + h)),
                  pl.BlockSpec((seq, HEAD_DIM), lambda b, h, i: (b, COL_AK // HEAD_DIM + h)),
                  pl.BlockSpec((seq, HEAD_DIM), lambda b, h, i: (b, COL_AV // HEAD_DIM + h)),
                  pl.BlockSpec((n_ctx, HEAD_DIM), lambda b, h, i: (b, COL_AK // HEAD_DIM + h)),
                  pl.BlockSpec((n_ctx, HEAD_DIM), lambda b, h, i: (b, COL_AV // HEAD_DIM + h))],
        out_specs=pl.BlockSpec((ATT_TQ, gw), lambda b, h, i: (b * nq + i, h)),
        out_shape=jax.ShapeDtypeStruct((batch * seq, N_Q_HEADS * HEAD_DIM), BF16),
        scratch_shapes=[pltpu.VMEM((rows, 1), F32), pltpu.VMEM((rows, 2 * HEAD_DIM), F32)],
        compiler_params=pltpu.CompilerParams(vmem_limit_bytes=VMEM_LIMIT),
        name="attn",
    )(score_bound, main, main, main, ctx_main, ctx_main)


def _merge_kernel(x_ref, attn_ref, of_ref, ob_ref, go_ref, ga_ref, gg_ref, gn_ref, gt_ref,
                  wa_ref, wg_ref, wo_ref, o_ref):
    gn = gn_ref[...]
    heads = []
    for h in range(GLA_HEADS):
        s = slice(h * GLA_DV, (h + 1) * GLA_DV)
        o = of_ref[:, s] + ob_ref[:, s]
        y = _head_norm(o, gn)
        gate = go_ref[:, s].astype(F32)
        heads.append((y * (gate * _sigmoid(gate))).astype(BF16))
    gla = jnp.concatenate(heads, axis=-1)
    ya = _sigmoid(ga_ref[...].astype(F32)) * _dot(attn_ref[...], wa_ref[...])
    yg = _sigmoid(gg_ref[...].astype(F32)) * _dot(gla, wg_ref[...])
    out = _dot((ya + yg).astype(BF16), wo_ref[...])
    o_ref[...] = x_ref[...] + gt_ref[...] * out


def _merge(x2d, attn, o_f, o_b, main, gla_norm, mod4, wa, wg, wo, *, tm, seq_tiles):
    rows, d = x2d.shape
    row = lambda i: (i, 0)
    return pl.pallas_call(
        _merge_kernel,
        grid=(rows // tm,),
        in_specs=[pl.BlockSpec((tm, d), row), pl.BlockSpec((tm, d), row),
                  pl.BlockSpec((tm, d), row), pl.BlockSpec((tm, d), row),
                  pl.BlockSpec((tm, d), lambda i: (i, COL_GO // d)),
                  pl.BlockSpec((tm, d), lambda i: (i, COL_GA // d)),
                  pl.BlockSpec((tm, d), lambda i: (i, COL_GG // d)),
                  _resident((1, GLA_DV), lambda i: (0, 0)),
                  pl.BlockSpec((None, None, 1, d), lambda i: (i // seq_tiles, 2, 0, 0)),
                  _resident((d, d), lambda i: (0, 0)), _resident((d, d), lambda i: (0, 0)),
                  _resident((d, d), lambda i: (0, 0))],
        out_specs=pl.BlockSpec((tm, d), row),
        out_shape=jax.ShapeDtypeStruct((rows, d), F32),
        compiler_params=pltpu.CompilerParams(vmem_limit_bytes=VMEM_LIMIT),
        name="merge_out",
    )(x2d, attn, o_f, o_b, main, main, main, gla_norm, mod4, wa, wg, wo)


FF_CHUNK = 1024


def _mlp_kernel(x_ref, sh_ref, sc_ref, gt_ref, n2_ref, w1_ref, w2_ref, o_ref, *, d_ff):
    x = x_ref[...]
    ms = jnp.mean(x * x, axis=-1, keepdims=True)
    h = x * lax.rsqrt(ms + EPS) * n2_ref[...]
    hb = (h * (1.0 + sc_ref[...]) + sh_ref[...]).astype(BF16)
    acc = jnp.zeros(x.shape, F32)
    for c in range(d_ff // FF_CHUNK):
        s = slice(c * FF_CHUNK, (c + 1) * FF_CHUNK)
        u = jnp.maximum(_dot(hb, w1_ref[:, s]), 0.0)
        acc = acc + _dot((u * u).astype(BF16), w2_ref[s, :])
    o_ref[...] = x + gt_ref[...] * acc


def _mlp(x2d, mod4, norm2, w1, w2, *, tm, seq_tiles):
    rows, d = x2d.shape
    d_ff = w1.shape[1]
    mod = lambda g: pl.BlockSpec((None, None, 1, d), lambda i: (i // seq_tiles, g, 0, 0))
    return pl.pallas_call(
        functools.partial(_mlp_kernel, d_ff=d_ff),
        grid=(rows // tm,),
        in_specs=[pl.BlockSpec((tm, d), lambda i: (i, 0)), mod(3), mod(4), mod(5),
                  _resident((1, d), lambda i: (0, 0)),
                  _resident((d, d_ff), lambda i: (0, 0)), _resident((d_ff, d), lambda i: (0, 0))],
        out_specs=pl.BlockSpec((tm, d), lambda i: (i, 0)),
        out_shape=jax.ShapeDtypeStruct((rows, d), F32),
        compiler_params=pltpu.CompilerParams(vmem_limit_bytes=VMEM_LIMIT),
        name="mlp",
    )(x2d, mod4, mod4, mod4, norm2, w1, w2)


def _rope_tables(seq):
    t = np.arange(seq)
    half = HEAD_DIM // 2
    freqs = ROPE_THETA ** (-np.arange(0, half, 2, dtype=np.float32) / half)
    ang_r = (t // GRID_W).astype(np.float32)[:, None] * freqs
    ang_c = (t % GRID_W).astype(np.float32)[:, None] * freqs
    ang_r, ang_c = jnp.asarray(ang_r, F32), jnp.asarray(ang_c, F32)
    cos = jnp.concatenate([jnp.cos(ang_r)] * 2 + [jnp.cos(ang_c)] * 2, axis=-1)
    sin = jnp.concatenate([-jnp.sin(ang_r), jnp.sin(ang_r), -jnp.sin(ang_c), jnp.sin(ang_c)], axis=-1)
    return cos, sin


def _pad_rows(w, row0):
    return jnp.pad(w, ((row0, LR_PAD - row0 - w.shape[0]), (0, 0)))


def kernel(x, c, ctx, c_ctx, w_ada, b_ada, norm1, w_in, q_norm, k_norm, w_gk_fwd, b_gk_fwd,
           w_gk_bwd, b_gk_bwd, gla_norm, w_br_attn, w_br_gla, w_out, norm2, w_mlp1, w_mlp2):
    batch, seq, d = x.shape
    n_ctx = ctx.shape[1]
    depth = w_ada.shape[0]
    assert depth == 1 and seq % GLA_TBLOCK == 0 and seq % ATT_TK == 0 and batch <= 7
    tm = 512
    seq_tiles = seq // tm
    ctx_tm = n_ctx
    l = 0

    cc = jnp.zeros((8, d), F32).at[:batch].set(c).at[batch].set(c_ctx)
    mod4 = _ada(cc, w_ada[l], b_ada[l][None, :]).reshape(8, 6, 1, d)

    w = w_in[l]
    w_main = jnp.concatenate(
        [w[:, :2048], w[:, 2080:3104], w[:, 3616:4640], w[:, 4640:5664], w[:, 5664:6688], w[:, 3104:3616]],
        axis=1).astype(BF16)
    w_lr = jnp.pad(w[:, 2048:2080], ((0, 0), (0, LR_PAD - 2 * GLA_GATE_RANK))).astype(BF16)
    cos_t, sin_t = _rope_tables(seq)
    q_scale = (HEAD_DIM ** -0.5) * float(np.log2(np.e))
    q_gain = (q_norm[l] * q_scale)[None, :]
    k_gain = k_norm[l][None, :]
    score_bound = (1.01 * HEAD_DIM * q_scale * jnp.max(jnp.abs(q_norm[l])) * jnp.max(jnp.abs(k_norm[l])))
    score_bound = score_bound.reshape(1, 1).astype(F32)
    n1 = norm1[l][None, :]

    main, lr = _inproj(x.reshape(batch * seq, d), mod4, lambda i: i // seq_tiles, n1, w_main, w_lr,
                       cos_t, sin_t, q_gain, k_gain, n_cols=N_MAIN, rope=True, tm=tm, seq_tiles=seq_tiles)
    ctx_main, ctx_lr = _inproj(ctx.reshape(batch * n_ctx, d), mod4, lambda i: batch, n1, w_main, w_lr,
                               cos_t, sin_t, q_gain, k_gain, n_cols=N_CTX_MAIN, rope=False, tm=ctx_tm,
                               seq_tiles=1)

    wf = _pad_rows(w_gk_fwd[l], 0).astype(BF16)
    wb = _pad_rows(w_gk_bwd[l], GLA_GATE_RANK).astype(BF16)
    bf, bb = b_gk_fwd[l][None, :], b_gk_bwd[l][None, :]
    sf0, sb0 = _glastate(ctx_main, ctx_lr, wf, bf, wb, bb, batch=batch, n_ctx=n_ctx)
    o_f, o_b = _gla(main, lr, wf, bf, wb, bb, sf0, sb0, batch=batch, seq=seq)

    attn = _attn(score_bound, main, ctx_main, batch=batch, seq=seq, n_ctx=n_ctx)

    x1 = _merge(x.reshape(batch * seq, d), attn, o_f, o_b, main, gla_norm[l][None, :], mod4,
                w_br_attn[l].astype(BF16), w_br_gla[l].astype(BF16), w_out[l].astype(BF16),
                tm=tm, seq_tiles=seq_tiles)
    x2 = _mlp(x1, mod4, norm2[l][None, :], w_mlp1[l].astype(BF16), w_mlp2[l].astype(BF16),
              tm=tm, seq_tiles=seq_tiles)
    return x2.reshape(batch, seq, d)
```

```python
import functools

import numpy as np
import jax
import jax.numpy as jnp
from jax import lax
from jax.experimental import pallas as pl
from jax.experimental.pallas import tpu as pltpu

F32 = jnp.float32
BF16 = jnp.bfloat16

GRID_W = 64
HEAD_DIM = 128
N_Q_HEADS = 8
N_KV_HEADS = 2
Q_PER_KV = N_Q_HEADS // N_KV_HEADS
ROPE_THETA = 10000.0
GLA_HEADS = 4
GLA_DK = 128
GLA_DV = 256
GLA_GATE_RANK = 16
GLA_GATE_NORM = 16.0
EPS = 1e-6

COL_AK, COL_AV, COL_GK, COL_GV = 0, 256, 512, 1024
COL_AQ, COL_GO, COL_GA, COL_GG, COL_GQ = 2048, 3072, 4096, 5120, 6144
N_MAIN = 6656
N_CTX_MAIN = 2048
LR_PAD = 128
PROJ_CHUNK = 512

VMEM_LIMIT = 56 * 1024 * 1024


def _dot(a, b):
    return jnp.dot(a, b, preferred_element_type=F32)


def _dot_nt(a, b):
    return lax.dot_general(a, b, (((1,), (1,)), ((), ())), preferred_element_type=F32)


def _dot_tn(a, b):
    return lax.dot_general(a, b, (((0,), (0,)), ((), ())), preferred_element_type=F32)


def _sigmoid(x):
    return 1.0 / (1.0 + jnp.exp(-x))


def _resident(shape, index_map):
    return pl.BlockSpec(shape, index_map, pipeline_mode=pl.Buffered(1))


def _ada_kernel(c_ref, w_ref, b_ref, o_ref):
    c = c_ref[...]
    s = (c * _sigmoid(c)).astype(BF16)
    o_ref[...] = _dot(s, w_ref[...].astype(BF16)) + b_ref[...]


def _ada(cc, w_ada, b_ada):
    d, n = w_ada.shape
    tn = 1536
    return pl.pallas_call(
        _ada_kernel,
        grid=(n // tn,),
        in_specs=[pl.BlockSpec((8, d), lambda j: (0, 0)),
                  pl.BlockSpec((d, tn), lambda j: (0, j)),
                  pl.BlockSpec((1, tn), lambda j: (0, j))],
        out_specs=pl.BlockSpec((8, tn), lambda j: (0, j)),
        out_shape=jax.ShapeDtypeStruct((8, n), F32),
        compiler_params=pltpu.CompilerParams(vmem_limit_bytes=VMEM_LIMIT),
        name="ada",
    )(cc, w_ada, b_ada)


def _head_norm(a, gain):
    ms = jnp.mean(a * a, axis=-1, keepdims=True)
    return a * lax.rsqrt(ms + EPS) * gain


def _rope(n, cos, sin_signed):
    lane = lax.broadcasted_iota(jnp.int32, n.shape, 1)
    partner = jnp.where((lane % 64) < 32, pltpu.roll(n, 96, 1), pltpu.roll(n, 32, 1))
    return n * cos + partner * sin_signed


def _inproj_kernel(x_ref, sh_ref, sc_ref, n1_ref, w_ref, wlr_ref, cos_ref, sin_ref, qg_ref, kg_ref,
                   main_ref, lr_ref, *, n_cols, rope):
    x = x_ref[...]
    ms = jnp.mean(x * x, axis=-1, keepdims=True)
    h = x * lax.rsqrt(ms + EPS) * n1_ref[...]
    hb = (h * (1.0 + sc_ref[...]) + sh_ref[...]).astype(BF16)
    lr_ref[...] = _dot(hb, wlr_ref[...])

    def qk_head(a, gain):
        n = _head_norm(a, gain)
        if rope:
            n = _rope(n, cos_ref[...], sin_ref[...])
        return n.astype(BF16)

    for c in range(n_cols // PROJ_CHUNK):
        lo = c * PROJ_CHUNK
        acc = _dot(hb, w_ref[:, lo:lo + PROJ_CHUNK])
        for s in range(PROJ_CHUNK // HEAD_DIM):
            col = lo + s * HEAD_DIM
            a = acc[:, s * HEAD_DIM:(s + 1) * HEAD_DIM]
            if COL_AK <= col < COL_AV:
                main_ref[:, col:col + HEAD_DIM] = qk_head(a, kg_ref[...])
            elif COL_AQ <= col < COL_GO:
                main_ref[:, col:col + HEAD_DIM] = qk_head(a, qg_ref[...])
            else:
                main_ref[:, col:col + HEAD_DIM] = a.astype(BF16)


def _inproj(x2d, mod4, mod_row_of_tile, norm1, w_main, w_lr, cos_t, sin_t, q_gain, k_gain,
            *, n_cols, rope, tm, seq_tiles):
    rows, d = x2d.shape
    kern = functools.partial(_inproj_kernel, n_cols=n_cols, rope=rope)
    return pl.pallas_call(
        kern,
        grid=(rows // tm,),
        in_specs=[
            pl.BlockSpec((tm, d), lambda i: (i, 0)),
            pl.BlockSpec((None, None, 1, d), lambda i: (mod_row_of_tile(i), 0, 0, 0)),
            pl.BlockSpec((None, None, 1, d), lambda i: (mod_row_of_tile(i), 1, 0, 0)),
            _resident((1, d), lambda i: (0, 0)),
            _resident((d, n_cols), lambda i: (0, 0)),
            _resident((d, LR_PAD), lambda i: (0, 0)),
            pl.BlockSpec((tm, HEAD_DIM), lambda i: (i % seq_tiles, 0)),
            pl.BlockSpec((tm, HEAD_DIM), lambda i: (i % seq_tiles, 0)),
            _resident((1, HEAD_DIM), lambda i: (0, 0)),
            _resident((1, HEAD_DIM), lambda i: (0, 0)),
        ],
        out_specs=[pl.BlockSpec((tm, n_cols), lambda i: (i, 0)),
                   pl.BlockSpec((tm, LR_PAD), lambda i: (i, 0))],
        out_shape=[jax.ShapeDtypeStruct((rows, n_cols), BF16),
                   jax.ShapeDtypeStruct((rows, LR_PAD), F32)],
        compiler_params=pltpu.CompilerParams(vmem_limit_bytes=VMEM_LIMIT),
        name="inproj_rope" if rope else "inproj_ctx",
    )(x2d, mod4, mod4, norm1, w_main, w_lr, cos_t, sin_t, q_gain, k_gain)


def _log_decay(lr, w, b):
    z = _dot(lr.astype(BF16), w) + b
    return (jnp.minimum(z, 0.0) - jnp.log(1.0 + jnp.exp(-jnp.abs(z)))) * (1.0 / GLA_GATE_NORM)


def _tri(n, upper):
    r = lax.broadcasted_iota(jnp.int32, (n, n), 0)
    c = lax.broadcasted_iota(jnp.int32, (n, n), 1)
    return (c >= r) if upper else (r >= c)


def _cumsum_rows(tri_bf16, g):
    g_hi = g.astype(BF16)
    g_lo = (g - g_hi.astype(F32)).astype(BF16)
    return _dot(tri_bf16, g_hi) + _dot(tri_bf16, g_lo)


def _glastate_kernel(k_ref, v_ref, lr_ref, wf_ref, bf_ref, wb_ref, bb_ref, sf_ref, sb_ref, *, n):
    k = k_ref[...].astype(F32)
    v = v_ref[...]
    lr = lr_ref[...]
    gf = _log_decay(lr, wf_ref[...], bf_ref[...])
    b = _cumsum_rows(_tri(n, False).astype(BF16), gf)
    sf_ref[...] = _dot_tn(v, (k * jnp.exp(b[n - 1:n] - b)).astype(BF16))
    gb = _log_decay(lr, wb_ref[...], bb_ref[...])
    e = _cumsum_rows(_tri(n, True).astype(BF16), gb)
    sb_ref[...] = _dot_tn(v, (k * jnp.exp(e[0:1] - e)).astype(BF16))


def _glastate(ctx_main, ctx_lr, wf, bf, wb, bb, *, batch, n_ctx):
    kb = COL_GK // GLA_DK
    vb = COL_GV // GLA_DV
    st = jax.ShapeDtypeStruct((batch, GLA_HEADS, GLA_DV, GLA_DK), F32)
    st_spec = pl.BlockSpec((None, None, GLA_DV, GLA_DK), lambda b, h: (b, h, 0, 0))
    w_spec = pl.BlockSpec((LR_PAD, GLA_DK), lambda b, h: (0, h))
    b_spec = pl.BlockSpec((1, GLA_DK), lambda b, h: (0, h))
    return pl.pallas_call(
        functools.partial(_glastate_kernel, n=n_ctx),
        grid=(batch, GLA_HEADS),
        in_specs=[pl.BlockSpec((n_ctx, GLA_DK), lambda b, h: (b, kb + h)),
                  pl.BlockSpec((n_ctx, GLA_DV), lambda b, h: (b, vb + h)),
                  pl.BlockSpec((n_ctx, LR_PAD), lambda b, h: (b, 0)),
                  w_spec, b_spec, w_spec, b_spec],
        out_specs=[st_spec, st_spec],
        out_shape=[st, st],
        compiler_params=pltpu.CompilerParams(vmem_limit_bytes=VMEM_LIMIT),
        name="gla_ctx_state",
    )(ctx_main, ctx_main, ctx_lr, wf, bf, wb, bb)


GLA_CHUNK = 64
GLA_TBLOCK = 512


def _gla_kernel(qf_ref, kf_ref, vf_ref, lrf_ref, qb_ref, kb_ref, vb_ref, lrb_ref,
                wf_ref, bf_ref, wb_ref, bb_ref, sf0_ref, sb0_ref, of_ref, ob_ref, stf, stb):
    @pl.when(pl.program_id(2) == 0)
    def _():
        stf[...] = sf0_ref[...]
        stb[...] = sb0_ref[...]

    c, n = GLA_CHUNK, GLA_TBLOCK // GLA_CHUNK
    streams = ((lrf_ref, qf_ref, kf_ref, vf_ref, wf_ref, bf_ref, of_ref, stf, False),
               (lrb_ref, qb_ref, kb_ref, vb_ref, wb_ref, bb_ref, ob_ref, stb, True))
    tri, rows, q, k, v, b, b_mid, b_edge = [], [], [], [], [], [], [], []
    for lr_ref, q_ref, k_ref, v_ref, w_ref, bias_ref, _, _, upper in streams:
        t = _tri(c, upper)
        edge = 0 if upper else c - 1
        g = _log_decay(lr_ref[...], w_ref[...], bias_ref[...])
        bw = _cumsum_rows(t.astype(BF16), jnp.concatenate([g[i * c:(i + 1) * c] for i in range(n)], axis=1))
        for i in range(n):
            r = slice(i * c, (i + 1) * c)
            x = bw[:, i * GLA_DK:(i + 1) * GLA_DK]
            tri.append(t), rows.append(r), b.append(x)
            b_mid.append(x[c // 2:c // 2 + 1]), b_edge.append(x[edge:edge + 1])
            q.append(q_ref[r, :]), k.append(k_ref[r, :]), v.append(v_ref[r, :])
    u = range(2 * n)
    qt = [q[j].astype(F32) * (GLA_DK ** -0.5) * jnp.exp(b[j] - b_mid[j]) for j in u]
    kt = [k[j].astype(F32) * jnp.exp(b_mid[j] - b[j]) for j in u]
    s = [_dot_nt(qt[j].astype(BF16), kt[j].astype(BF16)) for j in u]
    a = [jnp.where(tri[j], s[j], 0.0).astype(BF16) for j in u]
    k_out = [(kt[j] * jnp.exp(b_edge[j] - b_mid[j])).astype(BF16) for j in u]
    upd = [_dot_tn(v[j], k_out[j]) for j in u]
    intra = [_dot(a[j], v[j]) for j in u]
    q_in = [(qt[j] * jnp.exp(b_mid[j])).astype(BF16) for j in u]
    dec = [jnp.exp(x) for x in b_edge]
    st = [stf[...], stb[...]]
    for step in range(n):
        for d, stream in enumerate(streams):
            j = d * n + (n - 1 - step if stream[8] else step)
            stream[6][rows[j], :] = intra[j] + _dot_nt(q_in[j], st[d].astype(BF16))
            st[d] = st[d] * dec[j] + upd[j]
    stf[...] = st[0]
    stb[...] = st[1]


def _gla(main, lr, wf, bf, wb, bb, sf0, sb0, *, batch, seq):
    nt = seq // GLA_TBLOCK
    tb = GLA_TBLOCK
    qb_, kb_, vb_ = COL_GQ // GLA_DK, COL_GK // GLA_DK, COL_GV // GLA_DV

    def fwd(col, per_head=True):
        return lambda b, h, t: (b * nt + t, col + (h if per_head else 0))

    def bwd(col, per_head=True):
        return lambda b, h, t: (b * nt + nt - 1 - t, col + (h if per_head else 0))

    def seq_specs(m):
        return [pl.BlockSpec((tb, GLA_DK), m(qb_)), pl.BlockSpec((tb, GLA_DK), m(kb_)),
                pl.BlockSpec((tb, GLA_DV), m(vb_)), pl.BlockSpec((tb, LR_PAD), m(0, False))]

    w_spec = pl.BlockSpec((LR_PAD, GLA_DK), lambda b, h, t: (0, h))
    b_spec = pl.BlockSpec((1, GLA_DK), lambda b, h, t: (0, h))
    st_spec = pl.BlockSpec((None, None, GLA_DV, GLA_DK), lambda b, h, t: (b, h, 0, 0))
    out = jax.ShapeDtypeStruct((batch * seq, GLA_HEADS * GLA_DV), F32)
    return pl.pallas_call(
        _gla_kernel,
        grid=(batch, GLA_HEADS, nt),
        in_specs=seq_specs(fwd) + seq_specs(bwd) + [w_spec, b_spec, w_spec, b_spec, st_spec, st_spec],
        out_specs=[pl.BlockSpec((tb, GLA_DV), fwd(0)), pl.BlockSpec((tb, GLA_DV), bwd(0))],
        out_shape=[out, out],
        scratch_shapes=[pltpu.VMEM((GLA_DV, GLA_DK), F32), pltpu.VMEM((GLA_DV, GLA_DK), F32)],
        compiler_params=pltpu.CompilerParams(
            dimension_semantics=("arbitrary", "arbitrary", "arbitrary"), vmem_limit_bytes=VMEM_LIMIT),
        name="gla",
    )(main, main, main, lr, main, main, main, lr, wf, bf, wb, bb, sf0, sb0)


ATT_TQ = 256
ATT_TK = 512


ATT_UNSHIFTED_MAX_LOG2 = 56.0


def _attn_kernel(bound_ref, q_ref, k_ref, v_ref, kc_ref, vc_ref, o_ref, m_sc, acc_sc, *, seq):
    tq = ATT_TQ
    q4 = jnp.concatenate([q_ref[:, g * HEAD_DIM:(g + 1) * HEAD_DIM] for g in range(Q_PER_KV)], axis=0)
    acc_sc[...] = jnp.zeros(acc_sc.shape, F32)

    def with_ones(vc):
        return jnp.concatenate([vc, jnp.ones(vc.shape, BF16)], axis=-1)

    def over_keys(step):
        def body(j, carry):
            off = pl.multiple_of(j * ATT_TK, ATT_TK)
            step(k_ref[pl.ds(off, ATT_TK), :], v_ref[pl.ds(off, ATT_TK), :])
            return carry

        lax.fori_loop(0, seq // ATT_TK, body, 0, unroll=4)
        step(kc_ref[...], vc_ref[...])

    def unshifted_step(kc, vc):
        p = jnp.exp2(_dot_nt(q4, kc)).astype(BF16)
        acc_sc[...] += _dot(p, with_ones(vc))

    def online_step(kc, vc):
        s = _dot_nt(q4, kc)
        m_prev = m_sc[...]
        m_new = jnp.maximum(m_prev, jnp.max(s, axis=-1, keepdims=True))
        p = jnp.exp2(s - m_new).astype(BF16)
        acc_sc[...] = jnp.exp2(m_prev - m_new) * acc_sc[...] + _dot(p, with_ones(vc))
        m_sc[...] = m_new

    small = bound_ref[0, 0] <= ATT_UNSHIFTED_MAX_LOG2

    @pl.when(small)
    def _():
        over_keys(unshifted_step)

    @pl.when(jnp.logical_not(small))
    def _():
        m_sc[...] = jnp.full(m_sc.shape, -jnp.inf, F32)
        over_keys(online_step)

    acc = acc_sc[...]
    o = acc[:, :HEAD_DIM] / acc[:, HEAD_DIM:]
    for g in range(Q_PER_KV):
        o_ref[:, g * HEAD_DIM:(g + 1) * HEAD_DIM] = o[g * tq:(g + 1) * tq].astype(BF16)


def _attn(score_bound, main, ctx_main, *, batch, seq, n_ctx):
    nq = seq // ATT_TQ
    gw = Q_PER_KV * HEAD_DIM
    rows = Q_PER_KV * ATT_TQ
    return pl.pallas_call(
        functools.partial(_attn_kernel, seq=seq),
        grid=(batch, N_KV_HEADS, nq),
        in_specs=[pl.BlockSpec(memory_space=pltpu.SMEM),
                  pl.BlockSpec((ATT_TQ, gw), lambda b, h, i: (b * nq + i, COL_AQ // gw + h)),
                  pl.BlockSpec((seq, HEAD_DIM), lambda b, h, i: (b, COL_AK // HEAD_DIM + h)),
                  pl.BlockSpec((seq, HEAD_DIM), lambda b, h, i: (b, COL_AV // HEAD_DIM + h)),
                  pl.BlockSpec((n_ctx, HEAD_DIM), lambda b, h, i: (b, COL_AK // HEAD_DIM + h)),
                  pl.BlockSpec((n_ctx, HEAD_DIM), lambda b, h, i: (b, COL_AV // HEAD_DIM + h))],
        out_specs=pl.BlockSpec((ATT_TQ, gw), lambda b, h, i: (b * nq + i, h)),
        out_shape=jax.ShapeDtypeStruct((batch * seq, N_Q_HEADS * HEAD_DIM), BF16),
        scratch_shapes=[pltpu.VMEM((rows, 1), F32), pltpu.VMEM((rows, 2 * HEAD_DIM), F32)],
        compiler_params=pltpu.CompilerParams(vmem_limit_bytes=VMEM_LIMIT),
        name="attn",
    )(score_bound, main, main, main, ctx_main, ctx_main)


def _merge_kernel(x_ref, attn_ref, of_ref, ob_ref, go_ref, ga_ref, gg_ref, gn_ref, gt_ref,
                  wa_ref, wg_ref, wo_ref, o_ref):
    gn = gn_ref[...]
    heads = []
    for h in range(GLA_HEADS):
        s = slice(h * GLA_DV, (h + 1) * GLA_DV)
        o = of_ref[:, s] + ob_ref[:, s]
        y = _head_norm(o, gn)
        gate = go_ref[:, s].astype(F32)
        heads.append((y * (gate * _sigmoid(gate))).astype(BF16))
    gla = jnp.concatenate(heads, axis=-1)
    ya = _sigmoid(ga_ref[...].astype(F32)) * _dot(attn_ref[...], wa_ref[...])
    yg = _sigmoid(gg_ref[...].astype(F32)) * _dot(gla, wg_ref[...])
    out = _dot((ya + yg).astype(BF16), wo_ref[...])
    o_ref[...] = x_ref[...] + gt_ref[...] * out


def _merge(x2d, attn, o_f, o_b, main, gla_norm, mod4, wa, wg, wo, *, tm, seq_tiles):
    rows, d = x2d.shape
    row = lambda i: (i, 0)
    return pl.pallas_call(
        _merge_kernel,
        grid=(rows // tm,),
        in_specs=[pl.BlockSpec((tm, d), row), pl.BlockSpec((tm, d), row),
                  pl.BlockSpec((tm, d), row), pl.BlockSpec((tm, d), row),
                  pl.BlockSpec((tm, d), lambda i: (i, COL_GO // d)),
                  pl.BlockSpec((tm, d), lambda i: (i, COL_GA // d)),
                  pl.BlockSpec((tm, d), lambda i: (i, COL_GG // d)),
                  _resident((1, GLA_DV), lambda i: (0, 0)),
                  pl.BlockSpec((None, None, 1, d), lambda i: (i // seq_tiles, 2, 0, 0)),
                  _resident((d, d), lambda i: (0, 0)), _resident((d, d), lambda i: (0, 0)),
                  _resident((d, d), lambda i: (0, 0))],
        out_specs=pl.BlockSpec((tm, d), row),
        out_shape=jax.ShapeDtypeStruct((rows, d), F32),
        compiler_params=pltpu.CompilerParams(vmem_limit_bytes=VMEM_LIMIT),
        name="merge_out",
    )(x2d, attn, o_f, o_b, main, main, main, gla_norm, mod4, wa, wg, wo)


FF_CHUNK = 1024


def _mlp_kernel(x_ref, sh_ref, sc_ref, gt_ref, n2_ref, w1_ref, w2_ref, o_ref, *, d_ff):
    x = x_ref[...]
    ms = jnp.mean(x * x, axis=-1, keepdims=True)
    h = x * lax.rsqrt(ms + EPS) * n2_ref[...]
    hb = (h * (1.0 + sc_ref[...]) + sh_ref[...]).astype(BF16)
    acc = jnp.zeros(x.shape, F32)
    for c in range(d_ff // FF_CHUNK):
        s = slice(c * FF_CHUNK, (c + 1) * FF_CHUNK)
        u = jnp.maximum(_dot(hb, w1_ref[:, s]), 0.0)
        acc = acc + _dot((u * u).astype(BF16), w2_ref[s, :])
    o_ref[...] = x + gt_ref[...] * acc


def _mlp(x2d, mod4, norm2, w1, w2, *, tm, seq_tiles):
    rows, d = x2d.shape
    d_ff = w1.shape[1]
    mod = lambda g: pl.BlockSpec((None, None, 1, d), lambda i: (i // seq_tiles, g, 0, 0))
    return pl.pallas_call(
        functools.partial(_mlp_kernel, d_ff=d_ff),
        grid=(rows // tm,),
        in_specs=[pl.BlockSpec((tm, d), lambda i: (i, 0)), mod(3), mod(4), mod(5),
                  _resident((1, d), lambda i: (0, 0)),
                  _resident((d, d_ff), lambda i: (0, 0)), _resident((d_ff, d), lambda i: (0, 0))],
        out_specs=pl.BlockSpec((tm, d), lambda i: (i, 0)),
        out_shape=jax.ShapeDtypeStruct((rows, d), F32),
        compiler_params=pltpu.CompilerParams(vmem_limit_bytes=VMEM_LIMIT),
        name="mlp",
    )(x2d, mod4, mod4, mod4, norm2, w1, w2)


def _rope_tables(seq):
    t = np.arange(seq)
    half = HEAD_DIM // 2
    freqs = ROPE_THETA ** (-np.arange(0, half, 2, dtype=np.float32) / half)
    ang_r = (t // GRID_W).astype(np.float32)[:, None] * freqs
    ang_c = (t % GRID_W).astype(np.float32)[:, None] * freqs
    ang_r, ang_c = jnp.asarray(ang_r, F32), jnp.asarray(ang_c, F32)
    cos = jnp.concatenate([jnp.cos(ang_r)] * 2 + [jnp.cos(ang_c)] * 2, axis=-1)
    sin = jnp.concatenate([-jnp.sin(ang_r), jnp.sin(ang_r), -jnp.sin(ang_c), jnp.sin(ang_c)], axis=-1)
    return cos, sin


def _pad_rows(w, row0):
    return jnp.pad(w, ((row0, LR_PAD - row0 - w.shape[0]), (0, 0)))


def kernel(x, c, ctx, c_ctx, w_ada, b_ada, norm1, w_in, q_norm, k_norm, w_gk_fwd, b_gk_fwd,
           w_gk_bwd, b_gk_bwd, gla_norm, w_br_attn, w_br_gla, w_out, norm2, w_mlp1, w_mlp2):
    batch, seq, d = x.shape
    n_ctx = ctx.shape[1]
    depth = w_ada.shape[0]
    assert depth == 1 and seq % GLA_TBLOCK == 0 and seq % ATT_TK == 0 and batch <= 7
    tm = 512
    seq_tiles = seq // tm
    ctx_tm = n_ctx
    l = 0

    cc = jnp.zeros((8, d), F32).at[:batch].set(c).at[batch].set(c_ctx)
    mod4 = _ada(cc, w_ada[l], b_ada[l][None, :]).reshape(8, 6, 1, d)

    w = w_in[l]
    w_main = jnp.concatenate(
        [w[:, :2048], w[:, 2080:3104], w[:, 3616:4640], w[:, 4640:5664], w[:, 5664:6688], w[:, 3104:3616]],
        axis=1).astype(BF16)
    w_lr = jnp.pad(w[:, 2048:2080], ((0, 0), (0, LR_PAD - 2 * GLA_GATE_RANK))).astype(BF16)
    cos_t, sin_t = _rope_tables(seq)
    q_scale = (HEAD_DIM ** -0.5) * float(np.log2(np.e))
    q_gain = (q_norm[l] * q_scale)[None, :]
    k_gain = k_norm[l][None, :]
    score_bound = (1.01 * HEAD_DIM * q_scale * jnp.max(jnp.abs(q_norm[l])) * jnp.max(jnp.abs(k_norm[l])))
    score_bound = score_bound.reshape(1, 1).astype(F32)
    n1 = norm1[l][None, :]

    main, lr = _inproj(x.reshape(batch * seq, d), mod4, lambda i: i // seq_tiles, n1, w_main, w_lr,
                       cos_t, sin_t, q_gain, k_gain, n_cols=N_MAIN, rope=True, tm=tm, seq_tiles=seq_tiles)
    ctx_main, ctx_lr = _inproj(ctx.reshape(batch * n_ctx, d), mod4, lambda i: batch, n1, w_main, w_lr,
                               cos_t, sin_t, q_gain, k_gain, n_cols=N_CTX_MAIN, rope=False, tm=ctx_tm,
                               seq_tiles=1)

    wf = _pad_rows(w_gk_fwd[l], 0).astype(BF16)
    wb = _pad_rows(w_gk_bwd[l], GLA_GATE_RANK).astype(BF16)
    bf, bb = b_gk_fwd[l][None, :], b_gk_bwd[l][None, :]
    sf0, sb0 = _glastate(ctx_main, ctx_lr, wf, bf, wb, bb, batch=batch, n_ctx=n_ctx)
    o_f, o_b = _gla(main, lr, wf, bf, wb, bb, sf0, sb0, batch=batch, seq=seq)

    attn = _attn(score_bound, main, ctx_main, batch=batch, seq=seq, n_ctx=n_ctx)

    x1 = _merge(x.reshape(batch * seq, d), attn, o_f, o_b, main, gla_norm[l][None, :], mod4,
                w_br_attn[l].astype(BF16), w_br_gla[l].astype(BF16), w_out[l].astype(BF16),
                tm=tm, seq_tiles=seq_tiles)
    x2 = _mlp(x1, mod4, norm2[l][None, :], w_mlp1[l].astype(BF16), w_mlp2[l].astype(BF16),
              tm=tm, seq_tiles=seq_tiles)
    return x2.reshape(batch, seq, d)
```

```python
import functools

import numpy as np
import jax
import jax.numpy as jnp
from jax import lax
from jax.experimental import pallas as pl
from jax.experimental.pallas import tpu as pltpu

F32 = jnp.float32
BF16 = jnp.bfloat16

GRID_W = 64
HEAD_DIM = 128
N_Q_HEADS = 8
N_KV_HEADS = 2
Q_PER_KV = N_Q_HEADS // N_KV_HEADS
ROPE_THETA = 10000.0
GLA_HEADS = 4
GLA_DK = 128
GLA_DV = 256
GLA_GATE_RANK = 16
GLA_GATE_NORM = 16.0
EPS = 1e-6

COL_AK, COL_AV, COL_GK, COL_GV = 0, 256, 512, 1024
COL_AQ, COL_GO, COL_GA, COL_GG, COL_GQ = 2048, 3072, 4096, 5120, 6144
N_MAIN = 6656
N_CTX_MAIN = 2048
LR_PAD = 128
PROJ_CHUNK = 512

VMEM_LIMIT = 56 * 1024 * 1024


def _dot(a, b):
    return jnp.dot(a, b, preferred_element_type=F32)


def _dot_nt(a, b):
    return lax.dot_general(a, b, (((1,), (1,)), ((), ())), preferred_element_type=F32)


def _dot_tn(a, b):
    return lax.dot_general(a, b, (((0,), (0,)), ((), ())), preferred_element_type=F32)


def _sigmoid(x):
    return 1.0 / (1.0 + jnp.exp(-x))


def _resident(shape, index_map):
    return pl.BlockSpec(shape, index_map, pipeline_mode=pl.Buffered(1))


def _ada_kernel(c_ref, w_ref, b_ref, o_ref):
    c = c_ref[...]
    s = (c * _sigmoid(c)).astype(BF16)
    o_ref[...] = _dot(s, w_ref[...].astype(BF16)) + b_ref[...]


def _ada(cc, w_ada, b_ada):
    d, n = w_ada.shape
    tn = 1536
    return pl.pallas_call(
        _ada_kernel,
        grid=(n // tn,),
        in_specs=[pl.BlockSpec((8, d), lambda j: (0, 0)),
                  pl.BlockSpec((d, tn), lambda j: (0, j)),
                  pl.BlockSpec((1, tn), lambda j: (0, j))],
        out_specs=pl.BlockSpec((8, tn), lambda j: (0, j)),
        out_shape=jax.ShapeDtypeStruct((8, n), F32),
        compiler_params=pltpu.CompilerParams(vmem_limit_bytes=VMEM_LIMIT),
        name="ada",
    )(cc, w_ada, b_ada)


def _head_norm(a, gain):
    ms = jnp.mean(a * a, axis=-1, keepdims=True)
    return a * lax.rsqrt(ms + EPS) * gain


def _rope(n, cos, sin_signed):
    lane = lax.broadcasted_iota(jnp.int32, n.shape, 1)
    partner = jnp.where((lane % 64) < 32, pltpu.roll(n, 96, 1), pltpu.roll(n, 32, 1))
    return n * cos + partner * sin_signed


def _inproj_kernel(x_ref, sh_ref, sc_ref, n1_ref, w_ref, wlr_ref, cos_ref, sin_ref, qg_ref, kg_ref,
                   main_ref, lr_ref, *, n_cols, rope):
    x = x_ref[...]
    ms = jnp.mean(x * x, axis=-1, keepdims=True)
    h = x * lax.rsqrt(ms + EPS) * n1_ref[...]
    hb = (h * (1.0 + sc_ref[...]) + sh_ref[...]).astype(BF16)
    lr_ref[...] = _dot(hb, wlr_ref[...])

    def qk_head(a, gain):
        n = _head_norm(a, gain)
        if rope:
            n = _rope(n, cos_ref[...], sin_ref[...])
        return n.astype(BF16)

    for c in range(n_cols // PROJ_CHUNK):
        lo = c * PROJ_CHUNK
        acc = _dot(hb, w_ref[:, lo:lo + PROJ_CHUNK])
        for s in range(PROJ_CHUNK // HEAD_DIM):
            col = lo + s * HEAD_DIM
            a = acc[:, s * HEAD_DIM:(s + 1) * HEAD_DIM]
            if COL_AK <= col < COL_AV:
                main_ref[:, col:col + HEAD_DIM] = qk_head(a, kg_ref[...])
            elif COL_AQ <= col < COL_GO:
                main_ref[:, col:col + HEAD_DIM] = qk_head(a, qg_ref[...])
            else:
                main_ref[:, col:col + HEAD_DIM] = a.astype(BF16)


def _inproj(x2d, mod4, mod_row_of_tile, norm1, w_main, w_lr, cos_t, sin_t, q_gain, k_gain,
            *, n_cols, rope, tm, seq_tiles):
    rows, d = x2d.shape
    kern = functools.partial(_inproj_kernel, n_cols=n_cols, rope=rope)
    return pl.pallas_call(
        kern,
        grid=(rows // tm,),
        in_specs=[
            pl.BlockSpec((tm, d), lambda i: (i, 0)),
            pl.BlockSpec((None, None, 1, d), lambda i: (mod_row_of_tile(i), 0, 0, 0)),
            pl.BlockSpec((None, None, 1, d), lambda i: (mod_row_of_tile(i), 1, 0, 0)),
            _resident((1, d), lambda i: (0, 0)),
            _resident((d, n_cols), lambda i: (0, 0)),
            _resident((d, LR_PAD), lambda i: (0, 0)),
            pl.BlockSpec((tm, HEAD_DIM), lambda i: (i % seq_tiles, 0)),
            pl.BlockSpec((tm, HEAD_DIM), lambda i: (i % seq_tiles, 0)),
            _resident((1, HEAD_DIM), lambda i: (0, 0)),
            _resident((1, HEAD_DIM), lambda i: (0, 0)),
        ],
        out_specs=[pl.BlockSpec((tm, n_cols), lambda i: (i, 0)),
                   pl.BlockSpec((tm, LR_PAD), lambda i: (i, 0))],
        out_shape=[jax.ShapeDtypeStruct((rows, n_cols), BF16),
                   jax.ShapeDtypeStruct((rows, LR_PAD), F32)],
        compiler_params=pltpu.CompilerParams(vmem_limit_bytes=VMEM_LIMIT),
        name="inproj_rope" if rope else "inproj_ctx",
    )(x2d, mod4, mod4, norm1, w_main, w_lr, cos_t, sin_t, q_gain, k_gain)


def _log_decay(lr, w, b):
    z = _dot(lr.astype(BF16), w) + b
    return (jnp.minimum(z, 0.0) - jnp.log(1.0 + jnp.exp(-jnp.abs(z)))) * (1.0 / GLA_GATE_NORM)


def _tri(n, upper):
    r = lax.broadcasted_iota(jnp.int32, (n, n), 0)
    c = lax.broadcasted_iota(jnp.int32, (n, n), 1)
    return (c >= r) if upper else (r >= c)


def _cumsum_rows(tri_bf16, g):
    g_hi = g.astype(BF16)
    g_lo = (g - g_hi.astype(F32)).astype(BF16)
    return _dot(tri_bf16, g_hi) + _dot(tri_bf16, g_lo)


def _glastate_kernel(k_ref, v_ref, lr_ref, wf_ref, bf_ref, wb_ref, bb_ref, sf_ref, sb_ref, *, n):
    k = k_ref[...].astype(F32)
    v = v_ref[...]
    lr = lr_ref[...]
    gf = _log_decay(lr, wf_ref[...], bf_ref[...])
    b = _cumsum_rows(_tri(n, False).astype(BF16), gf)
    sf_ref[...] = _dot_tn(v, (k * jnp.exp(b[n - 1:n] - b)).astype(BF16))
    gb = _log_decay(lr, wb_ref[...], bb_ref[...])
    e = _cumsum_rows(_tri(n, True).astype(BF16), gb)
    sb_ref[...] = _dot_tn(v, (k * jnp.exp(e[0:1] - e)).astype(BF16))


def _glastate(ctx_main, ctx_lr, wf, bf, wb, bb, *, batch, n_ctx):
    kb = COL_GK // GLA_DK
    vb = COL_GV // GLA_DV
    st = jax.ShapeDtypeStruct((batch, GLA_HEADS, GLA_DV, GLA_DK), F32)
    st_spec = pl.BlockSpec((None, None, GLA_DV, GLA_DK), lambda b, h: (b, h, 0, 0))
    w_spec = pl.BlockSpec((LR_PAD, GLA_DK), lambda b, h: (0, h))
    b_spec = pl.BlockSpec((1, GLA_DK), lambda b, h: (0, h))
    return pl.pallas_call(
        functools.partial(_glastate_kernel, n=n_ctx),
        grid=(batch, GLA_HEADS),
        in_specs=[pl.BlockSpec((n_ctx, GLA_DK), lambda b, h: (b, kb + h)),
                  pl.BlockSpec((n_ctx, GLA_DV), lambda b, h: (b, vb + h)),
                  pl.BlockSpec((n_ctx, LR_PAD), lambda b, h: (b, 0)),
                  w_spec, b_spec, w_spec, b_spec],
        out_specs=[st_spec, st_spec],
        out_shape=[st, st],
        compiler_params=pltpu.CompilerParams(vmem_limit_bytes=VMEM_LIMIT),
        name="gla_ctx_state",
    )(ctx_main, ctx_main, ctx_lr, wf, bf, wb, bb)


GLA_CHUNK = 64
GLA_TBLOCK = 512


def _gla_kernel(qf_ref, kf_ref, vf_ref, lrf_ref, qb_ref, kb_ref, vb_ref, lrb_ref,
                wf_ref, bf_ref, wb_ref, bb_ref, sf0_ref, sb0_ref, of_ref, ob_ref, stf, stb):
    @pl.when(pl.program_id(2) == 0)
    def _():
        stf[...] = sf0_ref[...]
        stb[...] = sb0_ref[...]

    c, n = GLA_CHUNK, GLA_TBLOCK // GLA_CHUNK
    streams = ((lrf_ref, qf_ref, kf_ref, vf_ref, wf_ref, bf_ref, of_ref, stf, False),
               (lrb_ref, qb_ref, kb_ref, vb_ref, wb_ref, bb_ref, ob_ref, stb, True))
    tri, rows, q, k, v, b, b_mid, b_edge = [], [], [], [], [], [], [], []
    for lr_ref, q_ref, k_ref, v_ref, w_ref, bias_ref, _, _, upper in streams:
        t = _tri(c, upper)
        edge = 0 if upper else c - 1
        g = _log_decay(lr_ref[...], w_ref[...], bias_ref[...])
        bw = _cumsum_rows(t.astype(BF16), jnp.concatenate([g[i * c:(i + 1) * c] for i in range(n)], axis=1))
        for i in range(n):
            r = slice(i * c, (i + 1) * c)
            x = bw[:, i * GLA_DK:(i + 1) * GLA_DK]
            tri.append(t), rows.append(r), b.append(x)
            b_mid.append(x[c // 2:c // 2 + 1]), b_edge.append(x[edge:edge + 1])
            q.append(q_ref[r, :]), k.append(k_ref[r, :]), v.append(v_ref[r, :])
    u = range(2 * n)
    qt = [q[j].astype(F32) * (GLA_DK ** -0.5) * jnp.exp(b[j] - b_mid[j]) for j in u]
    kt = [k[j].astype(F32) * jnp.exp(b_mid[j] - b[j]) for j in u]
    s = [_dot_nt(qt[j].astype(BF16), kt[j].astype(BF16)) for j in u]
    a = [jnp.where(tri[j], s[j], 0.0).astype(BF16) for j in u]
    k_out = [(kt[j] * jnp.exp(b_edge[j] - b_mid[j])).astype(BF16) for j in u]
    upd = [_dot_tn(v[j], k_out[j]) for j in u]
    intra = [_dot(a[j], v[j]) for j in u]
    q_in = [(qt[j] * jnp.exp(b_mid[j])).astype(BF16) for j in u]
    dec = [jnp.exp(x) for x in b_edge]
    st = [stf[...], stb[...]]
    for step in range(n):
        for d, stream in enumerate(streams):
            j = d * n + (n - 1 - step if stream[8] else step)
            stream[6][rows[j], :] = intra[j] + _dot_nt(q_in[j], st[d].astype(BF16))
            st[d] = st[d] * dec[j] + upd[j]
    stf[...] = st[0]
    stb[...] = st[1]


def _gla(main, lr, wf, bf, wb, bb, sf0, sb0, *, batch, seq):
    nt = seq // GLA_TBLOCK
    tb = GLA_TBLOCK
    qb_, kb_, vb_ = COL_GQ // GLA_DK, COL_GK // GLA_DK, COL_GV // GLA_DV

    def fwd(col, per_head=True):
        return lambda b, h, t: (b * nt + t, col + (h if per_head else 0))

    def bwd(col, per_head=True):
        return lambda b, h, t: (b * nt + nt - 1 - t, col + (h if per_head else 0))

    def seq_specs(m):
        return [pl.BlockSpec((tb, GLA_DK), m(qb_)), pl.BlockSpec((tb, GLA_DK), m(kb_)),
                pl.BlockSpec((tb, GLA_DV), m(vb_)), pl.BlockSpec((tb, LR_PAD), m(0, False))]

    w_spec = pl.BlockSpec((LR_PAD, GLA_DK), lambda b, h, t: (0, h))
    b_spec = pl.BlockSpec((1, GLA_DK), lambda b, h, t: (0, h))
    st_spec = pl.BlockSpec((None, None, GLA_DV, GLA_DK), lambda b, h, t: (b, h, 0, 0))
    out = jax.ShapeDtypeStruct((batch * seq, GLA_HEADS * GLA_DV), F32)
    return pl.pallas_call(
        _gla_kernel,
        grid=(batch, GLA_HEADS, nt),
        in_specs=seq_specs(fwd) + seq_specs(bwd) + [w_spec, b_spec, w_spec, b_spec, st_spec, st_spec],
        out_specs=[pl.BlockSpec((tb, GLA_DV), fwd(0)), pl.BlockSpec((tb, GLA_DV), bwd(0))],
        out_shape=[out, out],
        scratch_shapes=[pltpu.VMEM((GLA_DV, GLA_DK), F32), pltpu.VMEM((GLA_DV, GLA_DK), F32)],
        compiler_params=pltpu.CompilerParams(
            dimension_semantics=("arbitrary", "arbitrary", "arbitrary"), vmem_limit_bytes=VMEM_LIMIT),
        name="gla",
    )(main, main, main, lr, main, main, main, lr, wf, bf, wb, bb, sf0, sb0)


ATT_TQ = 256
ATT_TK = 512


ATT_UNSHIFTED_MAX_LOG2 = 56.0
ATT_ONES_ROWS = 16


def _attn_kernel(bound_ref, q_ref, k_ref, v_ref, kc_ref, vc_ref, o_ref, vt_sc, vct_sc, m_sc, acc_sc, *, seq):
    tq, tk, hd = ATT_TQ, ATT_TK, HEAD_DIM
    nk = seq // tk

    def transposed(v):
        return v.astype(F32).T.astype(BF16)

    @pl.when(pl.program_id(2) == 0)
    def _():
        for j in range(nk):
            vt_sc[j, 0:hd, :] = transposed(v_ref[j * tk:(j + 1) * tk, :])
            vt_sc[j, hd:, :] = jnp.ones((ATT_ONES_ROWS, tk), BF16)
        vct_sc[0:hd, :] = transposed(vc_ref[...])
        vct_sc[hd:, :] = jnp.ones((ATT_ONES_ROWS, vct_sc.shape[1]), BF16)

    q4 = jnp.concatenate([q_ref[:, g * hd:(g + 1) * hd] for g in range(Q_PER_KV)], axis=0)
    acc_sc[...] = jnp.zeros(acc_sc.shape, F32)

    def over_keys(step):
        def body(j, carry):
            step(k_ref[pl.ds(pl.multiple_of(j * tk, tk), tk), :], vt_sc[j])
            return carry

        lax.fori_loop(0, nk, body, 0, unroll=4)
        step(kc_ref[...], vct_sc[...])

    def unshifted_step(kc, vt):
        p = jnp.exp2(_dot_nt(kc, q4)).astype(BF16)
        acc_sc[...] += _dot(vt, p)

    def online_step(kc, vt):
        s = _dot_nt(kc, q4)
        m_prev = m_sc[...]
        m_new = jnp.maximum(m_prev, jnp.max(s, axis=0, keepdims=True))
        p = jnp.exp2(s - m_new).astype(BF16)
        acc_sc[...] = jnp.exp2(m_prev - m_new) * acc_sc[...] + _dot(vt, p)
        m_sc[...] = m_new

    small = bound_ref[0, 0] <= ATT_UNSHIFTED_MAX_LOG2

    @pl.when(small)
    def _():
        over_keys(unshifted_step)

    @pl.when(jnp.logical_not(small))
    def _():
        m_sc[...] = jnp.full(m_sc.shape, -jnp.inf, F32)
        over_keys(online_step)

    acc = acc_sc[...]
    o = (acc[:hd] / acc[hd:hd + 1]).T
    for g in range(Q_PER_KV):
        o_ref[:, g * hd:(g + 1) * hd] = o[g * tq:(g + 1) * tq].astype(BF16)


def _attn(score_bound, main, ctx_main, *, batch, seq, n_ctx):
    nq = seq // ATT_TQ
    gw = Q_PER_KV * HEAD_DIM
    rows = Q_PER_KV * ATT_TQ
    vt_rows = HEAD_DIM + ATT_ONES_ROWS
    return pl.pallas_call(
        functools.partial(_attn_kernel, seq=seq),
        grid=(batch, N_KV_HEADS, nq),
        in_specs=[pl.BlockSpec(memory_space=pltpu.SMEM),
                  pl.BlockSpec((ATT_TQ, gw), lambda b, h, i: (b * nq + i, COL_AQ // gw + h)),
                  pl.BlockSpec((seq, HEAD_DIM), lambda b, h, i: (b, COL_AK // HEAD_DIM + h)),
                  pl.BlockSpec((seq, HEAD_DIM), lambda b, h, i: (b, COL_AV // HEAD_DIM + h)),
                  pl.BlockSpec((n_ctx, HEAD_DIM), lambda b, h, i: (b, COL_AK // HEAD_DIM + h)),
                  pl.BlockSpec((n_ctx, HEAD_DIM), lambda b, h, i: (b, COL_AV // HEAD_DIM + h))],
        out_specs=pl.BlockSpec((ATT_TQ, gw), lambda b, h, i: (b * nq + i, h)),
        out_shape=jax.ShapeDtypeStruct((batch * seq, N_Q_HEADS * HEAD_DIM), BF16),
        scratch_shapes=[pltpu.VMEM((seq // ATT_TK, vt_rows, ATT_TK), BF16),
                        pltpu.VMEM((vt_rows, n_ctx), BF16),
                        pltpu.VMEM((1, rows), F32), pltpu.VMEM((vt_rows, rows), F32)],
        compiler_params=pltpu.CompilerParams(
            dimension_semantics=("arbitrary", "arbitrary", "arbitrary"), vmem_limit_bytes=VMEM_LIMIT),
        name="attn",
    )(score_bound, main, main, main, ctx_main, ctx_main)


def _merge_kernel(x_ref, attn_ref, of_ref, ob_ref, go_ref, ga_ref, gg_ref, gn_ref, gt_ref,
                  wa_ref, wg_ref, wo_ref, o_ref):
    gn = gn_ref[...]
    heads = []
    for h in range(GLA_HEADS):
        s = slice(h * GLA_DV, (h + 1) * GLA_DV)
        o = of_ref[:, s] + ob_ref[:, s]
        y = _head_norm(o, gn)
        gate = go_ref[:, s].astype(F32)
        heads.append((y * (gate * _sigmoid(gate))).astype(BF16))
    gla = jnp.concatenate(heads, axis=-1)
    ya = _sigmoid(ga_ref[...].astype(F32)) * _dot(attn_ref[...], wa_ref[...])
    yg = _sigmoid(gg_ref[...].astype(F32)) * _dot(gla, wg_ref[...])
    out = _dot((ya + yg).astype(BF16), wo_ref[...])
    o_ref[...] = x_ref[...] + gt_ref[...] * out


def _merge(x2d, attn, o_f, o_b, main, gla_norm, mod4, wa, wg, wo, *, tm, seq_tiles):
    rows, d = x2d.shape
    row = lambda i: (i, 0)
    return pl.pallas_call(
        _merge_kernel,
        grid=(rows // tm,),
        in_specs=[pl.BlockSpec((tm, d), row), pl.BlockSpec((tm, d), row),
                  pl.BlockSpec((tm, d), row), pl.BlockSpec((tm, d), row),
                  pl.BlockSpec((tm, d), lambda i: (i, COL_GO // d)),
                  pl.BlockSpec((tm, d), lambda i: (i, COL_GA // d)),
                  pl.BlockSpec((tm, d), lambda i: (i, COL_GG // d)),
                  _resident((1, GLA_DV), lambda i: (0, 0)),
                  pl.BlockSpec((None, None, 1, d), lambda i: (i // seq_tiles, 2, 0, 0)),
                  _resident((d, d), lambda i: (0, 0)), _resident((d, d), lambda i: (0, 0)),
                  _resident((d, d), lambda i: (0, 0))],
        out_specs=pl.BlockSpec((tm, d), row),
        out_shape=jax.ShapeDtypeStruct((rows, d), F32),
        compiler_params=pltpu.CompilerParams(vmem_limit_bytes=VMEM_LIMIT),
        name="merge_out",
    )(x2d, attn, o_f, o_b, main, main, main, gla_norm, mod4, wa, wg, wo)


FF_CHUNK = 1024


def _mlp_kernel(x_ref, sh_ref, sc_ref, gt_ref, n2_ref, w1_ref, w2_ref, o_ref, *, d_ff):
    x = x_ref[...]
    ms = jnp.mean(x * x, axis=-1, keepdims=True)
    h = x * lax.rsqrt(ms + EPS) * n2_ref[...]
    hb = (h * (1.0 + sc_ref[...]) + sh_ref[...]).astype(BF16)
    acc = jnp.zeros(x.shape, F32)
    for c in range(d_ff // FF_CHUNK):
        s = slice(c * FF_CHUNK, (c + 1) * FF_CHUNK)
        u = jnp.maximum(_dot(hb, w1_ref[:, s]), 0.0)
        acc = acc + _dot((u * u).astype(BF16), w2_ref[s, :])
    o_ref[...] = x + gt_ref[...] * acc


def _mlp(x2d, mod4, norm2, w1, w2, *, tm, seq_tiles):
    rows, d = x2d.shape
    d_ff = w1.shape[1]
    mod = lambda g: pl.BlockSpec((None, None, 1, d), lambda i: (i // seq_tiles, g, 0, 0))
    return pl.pallas_call(
        functools.partial(_mlp_kernel, d_ff=d_ff),
        grid=(rows // tm,),
        in_specs=[pl.BlockSpec((tm, d), lambda i: (i, 0)), mod(3), mod(4), mod(5),
                  _resident((1, d), lambda i: (0, 0)),
                  _resident((d, d_ff), lambda i: (0, 0)), _resident((d_ff, d), lambda i: (0, 0))],
        out_specs=pl.BlockSpec((tm, d), lambda i: (i, 0)),
        out_shape=jax.ShapeDtypeStruct((rows, d), F32),
        compiler_params=pltpu.CompilerParams(vmem_limit_bytes=VMEM_LIMIT),
        name="mlp",
    )(x2d, mod4, mod4, mod4, norm2, w1, w2)


def _rope_tables(seq):
    t = np.arange(seq)
    half = HEAD_DIM // 2
    freqs = ROPE_THETA ** (-np.arange(0, half, 2, dtype=np.float32) / half)
    ang_r = (t // GRID_W).astype(np.float32)[:, None] * freqs
    ang_c = (t % GRID_W).astype(np.float32)[:, None] * freqs
    ang_r, ang_c = jnp.asarray(ang_r, F32), jnp.asarray(ang_c, F32)
    cos = jnp.concatenate([jnp.cos(ang_r)] * 2 + [jnp.cos(ang_c)] * 2, axis=-1)
    sin = jnp.concatenate([-jnp.sin(ang_r), jnp.sin(ang_r), -jnp.sin(ang_c), jnp.sin(ang_c)], axis=-1)
    return cos, sin


def _pad_rows(w, row0):
    return jnp.pad(w, ((row0, LR_PAD - row0 - w.shape[0]), (0, 0)))


def kernel(x, c, ctx, c_ctx, w_ada, b_ada, norm1, w_in, q_norm, k_norm, w_gk_fwd, b_gk_fwd,
           w_gk_bwd, b_gk_bwd, gla_norm, w_br_attn, w_br_gla, w_out, norm2, w_mlp1, w_mlp2):
    batch, seq, d = x.shape
    n_ctx = ctx.shape[1]
    depth = w_ada.shape[0]
    assert depth == 1 and seq % GLA_TBLOCK == 0 and seq % ATT_TK == 0 and batch <= 7
    tm = 512
    seq_tiles = seq // tm
    ctx_tm = n_ctx
    l = 0

    cc = jnp.zeros((8, d), F32).at[:batch].set(c).at[batch].set(c_ctx)
    mod4 = _ada(cc, w_ada[l], b_ada[l][None, :]).reshape(8, 6, 1, d)

    w = w_in[l]
    w_main = jnp.concatenate(
        [w[:, :2048], w[:, 2080:3104], w[:, 3616:4640], w[:, 4640:5664], w[:, 5664:6688], w[:, 3104:3616]],
        axis=1).astype(BF16)
    w_lr = jnp.pad(w[:, 2048:2080], ((0, 0), (0, LR_PAD - 2 * GLA_GATE_RANK))).astype(BF16)
    cos_t, sin_t = _rope_tables(seq)
    q_scale = (HEAD_DIM ** -0.5) * float(np.log2(np.e))
    q_gain = (q_norm[l] * q_scale)[None, :]
    k_gain = k_norm[l][None, :]
    score_bound = (1.01 * HEAD_DIM * q_scale * jnp.max(jnp.abs(q_norm[l])) * jnp.max(jnp.abs(k_norm[l])))
    score_bound = score_bound.reshape(1, 1).astype(F32)
    n1 = norm1[l][None, :]

    main, lr = _inproj(x.reshape(batch * seq, d), mod4, lambda i: i // seq_tiles, n1, w_main, w_lr,
                       cos_t, sin_t, q_gain, k_gain, n_cols=N_MAIN, rope=True, tm=tm, seq_tiles=seq_tiles)
    ctx_main, ctx_lr = _inproj(ctx.reshape(batch * n_ctx, d), mod4, lambda i: batch, n1, w_main, w_lr,
                               cos_t, sin_t, q_gain, k_gain, n_cols=N_CTX_MAIN, rope=False, tm=ctx_tm,
                               seq_tiles=1)

    wf = _pad_rows(w_gk_fwd[l], 0).astype(BF16)
    wb = _pad_rows(w_gk_bwd[l], GLA_GATE_RANK).astype(BF16)
    bf, bb = b_gk_fwd[l][None, :], b_gk_bwd[l][None, :]
    sf0, sb0 = _glastate(ctx_main, ctx_lr, wf, bf, wb, bb, batch=batch, n_ctx=n_ctx)
    o_f, o_b = _gla(main, lr, wf, bf, wb, bb, sf0, sb0, batch=batch, seq=seq)

    attn = _attn(score_bound, main, ctx_main, batch=batch, seq=seq, n_ctx=n_ctx)

    x1 = _merge(x.reshape(batch * seq, d), attn, o_f, o_b, main, gla_norm[l][None, :], mod4,
                w_br_attn[l].astype(BF16), w_br_gla[l].astype(BF16), w_out[l].astype(BF16),
                tm=tm, seq_tiles=seq_tiles)
    x2 = _mlp(x1, mod4, norm2[l][None, :], w_mlp1[l].astype(BF16), w_mlp2[l].astype(BF16),
              tm=tm, seq_tiles=seq_tiles)
    return x2.reshape(batch, seq, d)
```

```python
import functools

import numpy as np
import jax
import jax.numpy as jnp
from jax import lax
from jax.experimental import pallas as pl
from jax.experimental.pallas import tpu as pltpu

F32 = jnp.float32
BF16 = jnp.bfloat16

GRID_W = 64
HEAD_DIM = 128
N_Q_HEADS = 8
N_KV_HEADS = 2
Q_PER_KV = N_Q_HEADS // N_KV_HEADS
ROPE_THETA = 10000.0
GLA_HEADS = 4
GLA_DK = 128
GLA_DV = 256
GLA_GATE_RANK = 16
GLA_GATE_NORM = 16.0
EPS = 1e-6
LOG2_E = float(np.log2(np.e))

COL_AK, COL_AV, COL_GK, COL_GV = 0, 256, 512, 1024
COL_AQ, COL_GO, COL_GA, COL_GG, COL_GQ = 2048, 3072, 4096, 5120, 6144
N_MAIN = 6656
N_CTX_MAIN = 2048
LR_PAD = 128
PROJ_CHUNK = 512

VMEM_LIMIT = 56 * 1024 * 1024


def _dot(a, b):
    return jnp.dot(a, b, preferred_element_type=F32)


def _dot_nt(a, b):
    return lax.dot_general(a, b, (((1,), (1,)), ((), ())), preferred_element_type=F32)


def _dot_tn(a, b):
    return lax.dot_general(a, b, (((0,), (0,)), ((), ())), preferred_element_type=F32)


def _sigmoid(x):
    return 0.5 * jnp.tanh(0.5 * x) + 0.5


def _resident(shape, index_map):
    return pl.BlockSpec(shape, index_map, pipeline_mode=pl.Buffered(1))


def _ada_kernel(c_ref, w_ref, b_ref, o_ref):
    c = c_ref[...]
    s = (c * _sigmoid(c)).astype(BF16)
    o_ref[...] = _dot(s, w_ref[...].astype(BF16)) + b_ref[...]


def _ada(cc, w_ada, b_ada):
    d, n = w_ada.shape
    tn = 1536
    return pl.pallas_call(
        _ada_kernel,
        grid=(n // tn,),
        in_specs=[pl.BlockSpec((8, d), lambda j: (0, 0)),
                  pl.BlockSpec((d, tn), lambda j: (0, j)),
                  pl.BlockSpec((1, tn), lambda j: (0, j))],
        out_specs=pl.BlockSpec((8, tn), lambda j: (0, j)),
        out_shape=jax.ShapeDtypeStruct((8, n), F32),
        compiler_params=pltpu.CompilerParams(vmem_limit_bytes=VMEM_LIMIT),
        name="ada",
    )(cc, w_ada, b_ada)


def _head_norm(a, gain):
    ms = jnp.mean(a * a, axis=-1, keepdims=True)
    return a * lax.rsqrt(ms + EPS) * gain


def _rope(n, cos, sin_signed):
    lane = lax.broadcasted_iota(jnp.int32, n.shape, 1)
    partner = jnp.where((lane % 64) < 32, pltpu.roll(n, 96, 1), pltpu.roll(n, 32, 1))
    return n * cos + partner * sin_signed


def _inproj_kernel(x_ref, sh_ref, sc_ref, n1_ref, w_ref, wlr_ref, cos_ref, sin_ref, qg_ref, kg_ref,
                   main_ref, lr_ref, *, n_cols, rope):
    x = x_ref[...]
    ms = jnp.mean(x * x, axis=-1, keepdims=True)
    h = x * lax.rsqrt(ms + EPS) * n1_ref[...]
    hb = (h * (1.0 + sc_ref[...]) + sh_ref[...]).astype(BF16)
    lr_ref[...] = _dot(hb, wlr_ref[...])

    def qk_head(a, gain):
        n = _head_norm(a, gain)
        if rope:
            n = _rope(n, cos_ref[...], sin_ref[...])
        return n.astype(BF16)

    for c in range(n_cols // PROJ_CHUNK):
        lo = c * PROJ_CHUNK
        acc = _dot(hb, w_ref[:, lo:lo + PROJ_CHUNK])
        for s in range(PROJ_CHUNK // HEAD_DIM):
            col = lo + s * HEAD_DIM
            a = acc[:, s * HEAD_DIM:(s + 1) * HEAD_DIM]
            if COL_AK <= col < COL_AV:
                main_ref[:, col:col + HEAD_DIM] = qk_head(a, kg_ref[...])
            elif COL_AQ <= col < COL_GO:
                main_ref[:, col:col + HEAD_DIM] = qk_head(a, qg_ref[...])
            elif COL_GO <= col < COL_GA:
                main_ref[:, col:col + HEAD_DIM] = (a * _sigmoid(a)).astype(BF16)
            elif COL_GA <= col < COL_GQ:
                main_ref[:, col:col + HEAD_DIM] = _sigmoid(a).astype(BF16)
            else:
                main_ref[:, col:col + HEAD_DIM] = a.astype(BF16)


def _inproj(x2d, mod4, mod_row_of_tile, norm1, w_main, w_lr, cos_t, sin_t, q_gain, k_gain,
            *, n_cols, rope, tm, seq_tiles):
    rows, d = x2d.shape
    kern = functools.partial(_inproj_kernel, n_cols=n_cols, rope=rope)
    return pl.pallas_call(
        kern,
        grid=(rows // tm,),
        in_specs=[
            pl.BlockSpec((tm, d), lambda i: (i, 0)),
            pl.BlockSpec((None, None, 1, d), lambda i: (mod_row_of_tile(i), 0, 0, 0)),
            pl.BlockSpec((None, None, 1, d), lambda i: (mod_row_of_tile(i), 1, 0, 0)),
            _resident((1, d), lambda i: (0, 0)),
            _resident((d, n_cols), lambda i: (0, 0)),
            _resident((d, LR_PAD), lambda i: (0, 0)),
            pl.BlockSpec((tm, HEAD_DIM), lambda i: (i % seq_tiles, 0)),
            pl.BlockSpec((tm, HEAD_DIM), lambda i: (i % seq_tiles, 0)),
            _resident((1, HEAD_DIM), lambda i: (0, 0)),
            _resident((1, HEAD_DIM), lambda i: (0, 0)),
        ],
        out_specs=[pl.BlockSpec((tm, n_cols), lambda i: (i, 0)),
                   pl.BlockSpec((tm, LR_PAD), lambda i: (i, 0))],
        out_shape=[jax.ShapeDtypeStruct((rows, n_cols), BF16),
                   jax.ShapeDtypeStruct((rows, LR_PAD), F32)],
        compiler_params=pltpu.CompilerParams(vmem_limit_bytes=VMEM_LIMIT),
        name="inproj_rope" if rope else "inproj_ctx",
    )(x2d, mod4, mod4, norm1, w_main, w_lr, cos_t, sin_t, q_gain, k_gain)


def _log_decay(lr, w, b):
    z = _dot(lr.astype(BF16), w) + b
    return (jnp.minimum(z, 0.0) - jnp.log(1.0 + jnp.exp(-jnp.abs(z)))) * (LOG2_E / GLA_GATE_NORM)


def _tri(n, upper):
    r = lax.broadcasted_iota(jnp.int32, (n, n), 0)
    c = lax.broadcasted_iota(jnp.int32, (n, n), 1)
    return (c >= r) if upper else (r >= c)


def _cumsum_rows(tri_bf16, g):
    g_hi = g.astype(BF16)
    g_lo = (g - g_hi.astype(F32)).astype(BF16)
    return _dot(tri_bf16, g_hi) + _dot(tri_bf16, g_lo)


def _glastate_kernel(k_ref, v_ref, lr_ref, wf_ref, bf_ref, wb_ref, bb_ref, sf_ref, sb_ref, *, n):
    k = k_ref[...].astype(F32)
    v = v_ref[...]
    lr = lr_ref[...]
    gf = _log_decay(lr, wf_ref[...], bf_ref[...])
    b = _cumsum_rows(_tri(n, False).astype(BF16), gf)
    sf_ref[...] = _dot_tn(v, (k * jnp.exp2(b[n - 1:n] - b)).astype(BF16))
    gb = _log_decay(lr, wb_ref[...], bb_ref[...])
    e = _cumsum_rows(_tri(n, True).astype(BF16), gb)
    sb_ref[...] = _dot_tn(v, (k * jnp.exp2(e[0:1] - e)).astype(BF16))


def _glastate(ctx_main, ctx_lr, wf, bf, wb, bb, *, batch, n_ctx):
    kb = COL_GK // GLA_DK
    vb = COL_GV // GLA_DV
    st = jax.ShapeDtypeStruct((batch, GLA_HEADS, GLA_DV, GLA_DK), F32)
    st_spec = pl.BlockSpec((None, None, GLA_DV, GLA_DK), lambda b, h: (b, h, 0, 0))
    w_spec = pl.BlockSpec((LR_PAD, GLA_DK), lambda b, h: (0, h))
    b_spec = pl.BlockSpec((1, GLA_DK), lambda b, h: (0, h))
    return pl.pallas_call(
        functools.partial(_glastate_kernel, n=n_ctx),
        grid=(batch, GLA_HEADS),
        in_specs=[pl.BlockSpec((n_ctx, GLA_DK), lambda b, h: (b, kb + h)),
                  pl.BlockSpec((n_ctx, GLA_DV), lambda b, h: (b, vb + h)),
                  pl.BlockSpec((n_ctx, LR_PAD), lambda b, h: (b, 0)),
                  w_spec, b_spec, w_spec, b_spec],
        out_specs=[st_spec, st_spec],
        out_shape=[st, st],
        compiler_params=pltpu.CompilerParams(vmem_limit_bytes=VMEM_LIMIT),
        name="gla_ctx_state",
    )(ctx_main, ctx_main, ctx_lr, wf, bf, wb, bb)


GLA_CHUNK = 64
GLA_TBLOCK = 1024


def _gla_kernel(qf_ref, kf_ref, vf_ref, lrf_ref, qb_ref, kb_ref, vb_ref, lrb_ref,
                wf_ref, bf_ref, wb_ref, bb_ref, sf0_ref, sb0_ref, of_ref, ob_ref, stf, stb):
    @pl.when(pl.program_id(2) == 0)
    def _():
        stf[...] = sf0_ref[...]
        stb[...] = sb0_ref[...]

    c, n = GLA_CHUNK, GLA_TBLOCK // GLA_CHUNK
    streams = ((lrf_ref, qf_ref, kf_ref, vf_ref, wf_ref, bf_ref, of_ref, stf, False),
               (lrb_ref, qb_ref, kb_ref, vb_ref, wb_ref, bb_ref, ob_ref, stb, True))
    tri, rows, q, k, v, b, b_mid, b_edge = [], [], [], [], [], [], [], []
    for lr_ref, q_ref, k_ref, v_ref, w_ref, bias_ref, _, _, upper in streams:
        t = _tri(c, upper)
        edge = 0 if upper else c - 1
        g = _log_decay(lr_ref[...], w_ref[...], bias_ref[...])
        bw = _cumsum_rows(t.astype(BF16), jnp.concatenate([g[i * c:(i + 1) * c] for i in range(n)], axis=1))
        for i in range(n):
            r = slice(i * c, (i + 1) * c)
            x = bw[:, i * GLA_DK:(i + 1) * GLA_DK]
            tri.append(t), rows.append(r), b.append(x)
            b_mid.append(x[c // 2:c // 2 + 1]), b_edge.append(x[edge:edge + 1])
            q.append(q_ref[r, :]), k.append(k_ref[r, :]), v.append(v_ref[r, :])
    u = range(2 * n)
    qt = [q[j].astype(F32) * jnp.exp2(b[j] - b_mid[j]) for j in u]
    kt = [k[j].astype(F32) * jnp.exp2(b_mid[j] - b[j]) for j in u]
    s = [_dot_nt(qt[j].astype(BF16), kt[j].astype(BF16)) for j in u]
    a = [jnp.where(tri[j], s[j], 0.0).astype(BF16) for j in u]
    k_out = [(kt[j] * jnp.exp2(b_edge[j] - b_mid[j])).astype(BF16) for j in u]
    upd = [_dot_tn(v[j], k_out[j]) for j in u]
    intra = [_dot(a[j], v[j]) for j in u]
    q_in = [(qt[j] * jnp.exp2(b_mid[j])).astype(BF16) for j in u]
    dec = [jnp.exp2(x) for x in b_edge]
    st = [stf[...], stb[...]]
    for step in range(n):
        for d, stream in enumerate(streams):
            j = d * n + (n - 1 - step if stream[8] else step)
            stream[6][rows[j], :] = intra[j] + _dot_nt(q_in[j], st[d].astype(BF16))
            st[d] = st[d] * dec[j] + upd[j]
    stf[...] = st[0]
    stb[...] = st[1]


def _gla(main, lr, wf, bf, wb, bb, sf0, sb0, *, batch, seq):
    nt = seq // GLA_TBLOCK
    tb = GLA_TBLOCK
    qb_, kb_, vb_ = COL_GQ // GLA_DK, COL_GK // GLA_DK, COL_GV // GLA_DV

    def fwd(col, per_head=True):
        return lambda b, h, t: (b * nt + t, col + (h if per_head else 0))

    def bwd(col, per_head=True):
        return lambda b, h, t: (b * nt + nt - 1 - t, col + (h if per_head else 0))

    def seq_specs(m):
        return [pl.BlockSpec((tb, GLA_DK), m(qb_)), pl.BlockSpec((tb, GLA_DK), m(kb_)),
                pl.BlockSpec((tb, GLA_DV), m(vb_)), pl.BlockSpec((tb, LR_PAD), m(0, False))]

    w_spec = pl.BlockSpec((LR_PAD, GLA_DK), lambda b, h, t: (0, h))
    b_spec = pl.BlockSpec((1, GLA_DK), lambda b, h, t: (0, h))
    st_spec = pl.BlockSpec((None, None, GLA_DV, GLA_DK), lambda b, h, t: (b, h, 0, 0))
    out = jax.ShapeDtypeStruct((batch * seq, GLA_HEADS * GLA_DV), F32)
    return pl.pallas_call(
        _gla_kernel,
        grid=(batch, GLA_HEADS, nt),
        in_specs=seq_specs(fwd) + seq_specs(bwd) + [w_spec, b_spec, w_spec, b_spec, st_spec, st_spec],
        out_specs=[pl.BlockSpec((tb, GLA_DV), fwd(0)), pl.BlockSpec((tb, GLA_DV), bwd(0))],
        out_shape=[out, out],
        scratch_shapes=[pltpu.VMEM((GLA_DV, GLA_DK), F32), pltpu.VMEM((GLA_DV, GLA_DK), F32)],
        compiler_params=pltpu.CompilerParams(
            dimension_semantics=("arbitrary", "arbitrary", "arbitrary"), vmem_limit_bytes=VMEM_LIMIT),
        name="gla",
    )(main, main, main, lr, main, main, main, lr, wf, bf, wb, bb, sf0, sb0)


ATT_TQ = 512
ATT_TK = 512


ATT_UNSHIFTED_MAX_LOG2 = 56.0
ATT_ONES_ROWS = 16


def _attn_kernel(bound_ref, q_ref, k_ref, v_ref, kc_ref, vc_ref, o_ref, vt_sc, vct_sc, m_sc, acc_sc, *, seq):
    tq, tk, hd = ATT_TQ, ATT_TK, HEAD_DIM
    nk = seq // tk

    def transposed(v):
        return v.astype(F32).T.astype(BF16)

    @pl.when(pl.program_id(2) == 0)
    def _():
        for j in range(nk):
            vt_sc[j, 0:hd, :] = transposed(v_ref[j * tk:(j + 1) * tk, :])
            vt_sc[j, hd:, :] = jnp.ones((ATT_ONES_ROWS, tk), BF16)
        vct_sc[0:hd, :] = transposed(vc_ref[...])
        vct_sc[hd:, :] = jnp.ones((ATT_ONES_ROWS, vct_sc.shape[1]), BF16)

    q4 = jnp.concatenate([q_ref[:, g * hd:(g + 1) * hd] for g in range(Q_PER_KV)], axis=0)
    acc_sc[...] = jnp.zeros(acc_sc.shape, F32)

    def over_keys(step, unroll):
        def body(j, carry):
            step(k_ref[pl.ds(pl.multiple_of(j * tk, tk), tk), :], vt_sc[j])
            return carry

        lax.fori_loop(0, nk, body, 0, unroll=unroll)
        step(kc_ref[...], vct_sc[...])

    def unshifted_step(kc, vt):
        p = jnp.exp2(_dot_nt(kc, q4)).astype(BF16)
        acc_sc[...] += _dot(vt, p)

    def online_step(kc, vt):
        s = _dot_nt(kc, q4)
        m_prev = m_sc[...]
        m_new = jnp.maximum(m_prev, jnp.max(s, axis=0, keepdims=True))
        p = jnp.exp2(s - m_new).astype(BF16)
        acc_sc[...] = jnp.exp2(m_prev - m_new) * acc_sc[...] + _dot(vt, p)
        m_sc[...] = m_new

    small = bound_ref[0, 0] <= ATT_UNSHIFTED_MAX_LOG2

    @pl.when(small)
    def _():
        over_keys(unshifted_step, True)

    @pl.when(jnp.logical_not(small))
    def _():
        m_sc[...] = jnp.full(m_sc.shape, -jnp.inf, F32)
        over_keys(online_step, 1)

    acc = acc_sc[...]
    o = (acc[:hd] / acc[hd:hd + 1]).T
    for g in range(Q_PER_KV):
        o_ref[:, g * hd:(g + 1) * hd] = o[g * tq:(g + 1) * tq].astype(BF16)


def _attn(score_bound, main, ctx_main, *, batch, seq, n_ctx):
    nq = seq // ATT_TQ
    gw = Q_PER_KV * HEAD_DIM
    rows = Q_PER_KV * ATT_TQ
    vt_rows = HEAD_DIM + ATT_ONES_ROWS
    return pl.pallas_call(
        functools.partial(_attn_kernel, seq=seq),
        grid=(batch, N_KV_HEADS, nq),
        in_specs=[pl.BlockSpec(memory_space=pltpu.SMEM),
                  pl.BlockSpec((ATT_TQ, gw), lambda b, h, i: (b * nq + i, COL_AQ // gw + h)),
                  pl.BlockSpec((seq, HEAD_DIM), lambda b, h, i: (b, COL_AK // HEAD_DIM + h)),
                  pl.BlockSpec((seq, HEAD_DIM), lambda b, h, i: (b, COL_AV // HEAD_DIM + h)),
                  pl.BlockSpec((n_ctx, HEAD_DIM), lambda b, h, i: (b, COL_AK // HEAD_DIM + h)),
                  pl.BlockSpec((n_ctx, HEAD_DIM), lambda b, h, i: (b, COL_AV // HEAD_DIM + h))],
        out_specs=pl.BlockSpec((ATT_TQ, gw), lambda b, h, i: (b * nq + i, h)),
        out_shape=jax.ShapeDtypeStruct((batch * seq, N_Q_HEADS * HEAD_DIM), BF16),
        scratch_shapes=[pltpu.VMEM((seq // ATT_TK, vt_rows, ATT_TK), BF16),
                        pltpu.VMEM((vt_rows, n_ctx), BF16),
                        pltpu.VMEM((1, rows), F32), pltpu.VMEM((vt_rows, rows), F32)],
        compiler_params=pltpu.CompilerParams(
            dimension_semantics=("arbitrary", "arbitrary", "arbitrary"), vmem_limit_bytes=VMEM_LIMIT),
        name="attn",
    )(score_bound, main, main, main, ctx_main, ctx_main)


def _merge_kernel(x_ref, attn_ref, of_ref, ob_ref, go_ref, ga_ref, gg_ref, gn_ref, gt_ref,
                  wa_ref, wg_ref, wo_ref, o_ref):
    gn = gn_ref[...]
    heads = []
    for h in range(GLA_HEADS):
        s = slice(h * GLA_DV, (h + 1) * GLA_DV)
        o = of_ref[:, s] + ob_ref[:, s]
        heads.append((_head_norm(o, gn) * go_ref[:, s].astype(F32)).astype(BF16))
    gla = jnp.concatenate(heads, axis=-1)
    ya = ga_ref[...].astype(F32) * _dot(attn_ref[...], wa_ref[...])
    yg = gg_ref[...].astype(F32) * _dot(gla, wg_ref[...])
    out = _dot((ya + yg).astype(BF16), wo_ref[...])
    o_ref[...] = x_ref[...] + gt_ref[...] * out


def _merge(x2d, attn, o_f, o_b, main, gla_norm, mod4, wa, wg, wo, *, tm, seq_tiles):
    rows, d = x2d.shape
    row = lambda i: (i, 0)
    return pl.pallas_call(
        _merge_kernel,
        grid=(rows // tm,),
        in_specs=[pl.BlockSpec((tm, d), row), pl.BlockSpec((tm, d), row),
                  pl.BlockSpec((tm, d), row), pl.BlockSpec((tm, d), row),
                  pl.BlockSpec((tm, d), lambda i: (i, COL_GO // d)),
                  pl.BlockSpec((tm, d), lambda i: (i, COL_GA // d)),
                  pl.BlockSpec((tm, d), lambda i: (i, COL_GG // d)),
                  _resident((1, GLA_DV), lambda i: (0, 0)),
                  pl.BlockSpec((None, None, 1, d), lambda i: (i // seq_tiles, 2, 0, 0)),
                  _resident((d, d), lambda i: (0, 0)), _resident((d, d), lambda i: (0, 0)),
                  _resident((d, d), lambda i: (0, 0))],
        out_specs=pl.BlockSpec((tm, d), row),
        out_shape=jax.ShapeDtypeStruct((rows, d), F32),
        compiler_params=pltpu.CompilerParams(vmem_limit_bytes=VMEM_LIMIT),
        name="merge_out",
    )(x2d, attn, o_f, o_b, main, main, main, gla_norm, mod4, wa, wg, wo)


FF_CHUNK = 1024


def _mlp_kernel(x_ref, sh_ref, sc_ref, gt_ref, n2_ref, w1_ref, w2_ref, o_ref, *, d_ff):
    x = x_ref[...]
    ms = jnp.mean(x * x, axis=-1, keepdims=True)
    h = x * lax.rsqrt(ms + EPS) * n2_ref[...]
    hb = (h * (1.0 + sc_ref[...]) + sh_ref[...]).astype(BF16)
    acc = jnp.zeros(x.shape, F32)
    for c in range(d_ff // FF_CHUNK):
        s = slice(c * FF_CHUNK, (c + 1) * FF_CHUNK)
        u = jnp.maximum(_dot(hb, w1_ref[:, s]), 0.0)
        acc = acc + _dot((u * u).astype(BF16), w2_ref[s, :])
    o_ref[...] = x + gt_ref[...] * acc


def _mlp(x2d, mod4, norm2, w1, w2, *, tm, seq_tiles):
    rows, d = x2d.shape
    d_ff = w1.shape[1]
    mod = lambda g: pl.BlockSpec((None, None, 1, d), lambda i: (i // seq_tiles, g, 0, 0))
    return pl.pallas_call(
        functools.partial(_mlp_kernel, d_ff=d_ff),
        grid=(rows // tm,),
        in_specs=[pl.BlockSpec((tm, d), lambda i: (i, 0)), mod(3), mod(4), mod(5),
                  _resident((1, d), lambda i: (0, 0)),
                  _resident((d, d_ff), lambda i: (0, 0)), _resident((d_ff, d), lambda i: (0, 0))],
        out_specs=pl.BlockSpec((tm, d), lambda i: (i, 0)),
        out_shape=jax.ShapeDtypeStruct((rows, d), F32),
        compiler_params=pltpu.CompilerParams(vmem_limit_bytes=VMEM_LIMIT),
        name="mlp",
    )(x2d, mod4, mod4, mod4, norm2, w1, w2)


def _rope_tables(seq):
    t = np.arange(seq)
    half = HEAD_DIM // 2
    freqs = ROPE_THETA ** (-np.arange(0, half, 2, dtype=np.float32) / half)
    ang_r = (t // GRID_W).astype(np.float32)[:, None] * freqs
    ang_c = (t % GRID_W).astype(np.float32)[:, None] * freqs
    cos = np.concatenate([np.cos(ang_r)] * 2 + [np.cos(ang_c)] * 2, axis=-1)
    sin = np.concatenate([-np.sin(ang_r), np.sin(ang_r), -np.sin(ang_c), np.sin(ang_c)], axis=-1)
    return jnp.asarray(cos, F32), jnp.asarray(sin, F32)


def _pad_rows(w, row0):
    return jnp.pad(w, ((row0, LR_PAD - row0 - w.shape[0]), (0, 0)))


def kernel(x, c, ctx, c_ctx, w_ada, b_ada, norm1, w_in, q_norm, k_norm, w_gk_fwd, b_gk_fwd,
           w_gk_bwd, b_gk_bwd, gla_norm, w_br_attn, w_br_gla, w_out, norm2, w_mlp1, w_mlp2):
    batch, seq, d = x.shape
    n_ctx = ctx.shape[1]
    depth = w_ada.shape[0]
    assert depth == 1 and seq % GLA_TBLOCK == 0 and seq % ATT_TK == 0 and batch <= 7
    tm = 512
    seq_tiles = seq // tm
    ctx_tm = n_ctx
    l = 0

    cc = jnp.zeros((8, d), F32).at[:batch].set(c).at[batch].set(c_ctx)
    mod4 = _ada(cc, w_ada[l], b_ada[l][None, :]).reshape(8, 6, 1, d)

    w = w_in[l]
    w_main = jnp.concatenate(
        [w[:, :2048], w[:, 2080:3104], w[:, 3616:4640], w[:, 4640:5664], w[:, 5664:6688],
         w[:, 3104:3616] * (GLA_DK ** -0.5)], axis=1).astype(BF16)
    w_lr = jnp.pad(w[:, 2048:2080], ((0, 0), (0, LR_PAD - 2 * GLA_GATE_RANK))).astype(BF16)
    cos_t, sin_t = _rope_tables(seq)
    q_scale = (HEAD_DIM ** -0.5) * float(np.log2(np.e))
    q_gain = (q_norm[l] * q_scale)[None, :]
    k_gain = k_norm[l][None, :]
    score_bound = (1.01 * HEAD_DIM * q_scale * jnp.max(jnp.abs(q_norm[l])) * jnp.max(jnp.abs(k_norm[l])))
    score_bound = score_bound.reshape(1, 1).astype(F32)
    n1 = norm1[l][None, :]

    main, lr = _inproj(x.reshape(batch * seq, d), mod4, lambda i: i // seq_tiles, n1, w_main, w_lr,
                       cos_t, sin_t, q_gain, k_gain, n_cols=N_MAIN, rope=True, tm=tm, seq_tiles=seq_tiles)
    ctx_main, ctx_lr = _inproj(ctx.reshape(batch * n_ctx, d), mod4, lambda i: batch, n1, w_main, w_lr,
                               cos_t, sin_t, q_gain, k_gain, n_cols=N_CTX_MAIN, rope=False, tm=ctx_tm,
                               seq_tiles=1)

    wf = _pad_rows(w_gk_fwd[l], 0).astype(BF16)
    wb = _pad_rows(w_gk_bwd[l], GLA_GATE_RANK).astype(BF16)
    bf, bb = b_gk_fwd[l][None, :], b_gk_bwd[l][None, :]
    sf0, sb0 = _glastate(ctx_main, ctx_lr, wf, bf, wb, bb, batch=batch, n_ctx=n_ctx)
    o_f, o_b = _gla(main, lr, wf, bf, wb, bb, sf0, sb0, batch=batch, seq=seq)

    attn = _attn(score_bound, main, ctx_main, batch=batch, seq=seq, n_ctx=n_ctx)

    x1 = _merge(x.reshape(batch * seq, d), attn, o_f, o_b, main, gla_norm[l][None, :], mod4,
                w_br_attn[l].astype(BF16), w_br_gla[l].astype(BF16), w_out[l].astype(BF16),
                tm=tm, seq_tiles=seq_tiles)
    x2 = _mlp(x1, mod4, norm2[l][None, :], w_mlp1[l].astype(BF16), w_mlp2[l].astype(BF16),
              tm=tm, seq_tiles=seq_tiles)
    return x2.reshape(batch, seq, d)
```

```python
import functools

import numpy as np
import jax
import jax.numpy as jnp
from jax import lax
from jax.experimental import pallas as pl
from jax.experimental.pallas import tpu as pltpu

F32 = jnp.float32
BF16 = jnp.bfloat16

GRID_W = 64
HEAD_DIM = 128
N_Q_HEADS = 8
N_KV_HEADS = 2
Q_PER_KV = N_Q_HEADS // N_KV_HEADS
ROPE_THETA = 10000.0
GLA_HEADS = 4
GLA_DK = 128
GLA_DV = 256
GLA_GATE_RANK = 16
GLA_GATE_NORM = 16.0
EPS = 1e-6
LOG2_E = float(np.log2(np.e))

COL_AK, COL_AV, COL_GK, COL_GV = 0, 256, 512, 1024
COL_AQ, COL_GO, COL_GA, COL_GG, COL_GQ = 2048, 3072, 4096, 5120, 6144
N_MAIN = 6656
N_CTX_MAIN = 2048
LR_PAD = 128
PROJ_CHUNK = 512

VMEM_LIMIT = 56 * 1024 * 1024


def _dot(a, b):
    return jnp.dot(a, b, preferred_element_type=F32)


def _dot_nt(a, b):
    return lax.dot_general(a, b, (((1,), (1,)), ((), ())), preferred_element_type=F32)


def _dot_tn(a, b):
    return lax.dot_general(a, b, (((0,), (0,)), ((), ())), preferred_element_type=F32)


def _sigmoid(x):
    return 0.5 * jnp.tanh(0.5 * x) + 0.5


def _resident(shape, index_map):
    return pl.BlockSpec(shape, index_map, pipeline_mode=pl.Buffered(1))


def _ada_kernel(c_ref, w_ref, b_ref, o_ref):
    c = c_ref[...]
    s = (c * _sigmoid(c)).astype(BF16)
    o_ref[...] = _dot(s, w_ref[...].astype(BF16)) + b_ref[...]


def _ada(cc, w_ada, b_ada):
    d, n = w_ada.shape
    tn = 1536
    return pl.pallas_call(
        _ada_kernel,
        grid=(n // tn,),
        in_specs=[pl.BlockSpec((8, d), lambda j: (0, 0)),
                  pl.BlockSpec((d, tn), lambda j: (0, j)),
                  pl.BlockSpec((1, tn), lambda j: (0, j))],
        out_specs=pl.BlockSpec((8, tn), lambda j: (0, j)),
        out_shape=jax.ShapeDtypeStruct((8, n), F32),
        compiler_params=pltpu.CompilerParams(vmem_limit_bytes=VMEM_LIMIT),
        name="ada",
    )(cc, w_ada, b_ada)


W_IN_LR = (2048, 2080)
W_IN_GROUPS = (((0, 2048), 1.0), ((2080, 3104), 1.0), ((3616, 4640), 1.0), ((4640, 5664), 1.0),
               ((5664, 6688), 1.0), ((3104, 3616), GLA_DK ** -0.5))
W_IN_CHUNKS = tuple((lo + o, scale) for (lo, hi), scale in W_IN_GROUPS for o in range(0, hi - lo, PROJ_CHUNK))
assert len(W_IN_CHUNKS) * PROJ_CHUNK == N_MAIN


def _chunk_table(j, column, divisor=1):
    entry = lambda idx: W_IN_CHUNKS[idx][column] // divisor if divisor != 1 else W_IN_CHUNKS[idx][column]
    out = entry(len(W_IN_CHUNKS) - 1)
    for idx in range(len(W_IN_CHUNKS) - 2, -1, -1):
        out = jnp.where(j == idx, entry(idx), out)
    return out


def _regroup_kernel(src_ref, lr_src_ref, main_ref, lr_ref):
    scale = _chunk_table(pl.program_id(0), 1).astype(F32)
    main_ref[...] = (src_ref[...] * scale).T.astype(BF16)
    rank2 = W_IN_LR[1] - W_IN_LR[0]
    lr_rows = jnp.concatenate([lr_src_ref[...], jnp.zeros((LR_PAD - rank2, lr_src_ref.shape[1]), F32)], axis=0)
    lr_ref[...] = lr_rows.T.astype(BF16)


def _regroup_w_in(w_t):
    n_in, d = w_t.shape
    rank2 = W_IN_LR[1] - W_IN_LR[0]
    return pl.pallas_call(
        _regroup_kernel,
        grid=(len(W_IN_CHUNKS),),
        in_specs=[pl.BlockSpec((pl.Element(PROJ_CHUNK), pl.Element(d)),
                               lambda j: (_chunk_table(j, 0, rank2) * rank2, 0)),
                  pl.BlockSpec((rank2, d), lambda j: (W_IN_LR[0] // rank2, 0))],
        out_specs=[pl.BlockSpec((d, PROJ_CHUNK), lambda j: (0, j)), pl.BlockSpec((d, LR_PAD), lambda j: (0, 0))],
        out_shape=[jax.ShapeDtypeStruct((d, N_MAIN), BF16), jax.ShapeDtypeStruct((d, LR_PAD), BF16)],
        compiler_params=pltpu.CompilerParams(vmem_limit_bytes=VMEM_LIMIT),
        name="regroup_w_in",
    )(w_t, w_t)


def _head_norm(a, gain):
    ms = jnp.mean(a * a, axis=-1, keepdims=True)
    return a * lax.rsqrt(ms + EPS) * gain


def _rope(n, cos, sin_signed):
    lane = lax.broadcasted_iota(jnp.int32, n.shape, 1)
    partner = jnp.where((lane % 64) < 32, pltpu.roll(n, 96, 1), pltpu.roll(n, 32, 1))
    return n * cos + partner * sin_signed


def _inproj_kernel(x_ref, sh_ref, sc_ref, n1_ref, w_ref, wlr_ref, cos_ref, sin_ref, qg_ref, kg_ref,
                   main_ref, lr_ref, *, n_cols, rope):
    x = x_ref[...]
    ms = jnp.mean(x * x, axis=-1, keepdims=True)
    h = x * lax.rsqrt(ms + EPS) * n1_ref[...]
    hb = (h * (1.0 + sc_ref[...]) + sh_ref[...]).astype(BF16)
    lr_ref[...] = _dot(hb, wlr_ref[...])

    def qk_head(a, gain):
        n = _head_norm(a, gain)
        if rope:
            n = _rope(n, cos_ref[...], sin_ref[...])
        return n.astype(BF16)

    for c in range(n_cols // PROJ_CHUNK):
        lo = c * PROJ_CHUNK
        acc = _dot(hb, w_ref[:, lo:lo + PROJ_CHUNK])
        for s in range(PROJ_CHUNK // HEAD_DIM):
            col = lo + s * HEAD_DIM
            a = acc[:, s * HEAD_DIM:(s + 1) * HEAD_DIM]
            if COL_AK <= col < COL_AV:
                main_ref[:, col:col + HEAD_DIM] = qk_head(a, kg_ref[...])
            elif COL_AQ <= col < COL_GO:
                main_ref[:, col:col + HEAD_DIM] = qk_head(a, qg_ref[...])
            elif COL_GO <= col < COL_GA:
                main_ref[:, col:col + HEAD_DIM] = (a * _sigmoid(a)).astype(BF16)
            elif COL_GA <= col < COL_GQ:
                main_ref[:, col:col + HEAD_DIM] = _sigmoid(a).astype(BF16)
            else:
                main_ref[:, col:col + HEAD_DIM] = a.astype(BF16)


def _inproj(x2d, mod4, mod_row_of_tile, norm1, w_main, w_lr, cos_t, sin_t, q_gain, k_gain,
            *, n_cols, rope, tm, seq_tiles):
    rows, d = x2d.shape
    kern = functools.partial(_inproj_kernel, n_cols=n_cols, rope=rope)
    return pl.pallas_call(
        kern,
        grid=(rows // tm,),
        in_specs=[
            pl.BlockSpec((tm, d), lambda i: (i, 0)),
            pl.BlockSpec((None, None, 1, d), lambda i: (mod_row_of_tile(i), 0, 0, 0)),
            pl.BlockSpec((None, None, 1, d), lambda i: (mod_row_of_tile(i), 1, 0, 0)),
            _resident((1, d), lambda i: (0, 0)),
            _resident((d, n_cols), lambda i: (0, 0)),
            _resident((d, LR_PAD), lambda i: (0, 0)),
            pl.BlockSpec((tm, HEAD_DIM), lambda i: (i % seq_tiles, 0)),
            pl.BlockSpec((tm, HEAD_DIM), lambda i: (i % seq_tiles, 0)),
            _resident((1, HEAD_DIM), lambda i: (0, 0)),
            _resident((1, HEAD_DIM), lambda i: (0, 0)),
        ],
        out_specs=[pl.BlockSpec((tm, n_cols), lambda i: (i, 0)),
                   pl.BlockSpec((tm, LR_PAD), lambda i: (i, 0))],
        out_shape=[jax.ShapeDtypeStruct((rows, n_cols), BF16),
                   jax.ShapeDtypeStruct((rows, LR_PAD), F32)],
        compiler_params=pltpu.CompilerParams(vmem_limit_bytes=VMEM_LIMIT),
        name="inproj_rope" if rope else "inproj_ctx",
    )(x2d, mod4, mod4, norm1, w_main, w_lr, cos_t, sin_t, q_gain, k_gain)


def _log_decay(lr, w, b):
    z = _dot(lr.astype(BF16), w) + b
    return (jnp.minimum(z, 0.0) - jnp.log(1.0 + jnp.exp(-jnp.abs(z)))) * (LOG2_E / GLA_GATE_NORM)


def _tri(n, upper):
    r = lax.broadcasted_iota(jnp.int32, (n, n), 0)
    c = lax.broadcasted_iota(jnp.int32, (n, n), 1)
    return (c >= r) if upper else (r >= c)


def _cumsum_rows(tri_bf16, g):
    g_hi = g.astype(BF16)
    g_lo = (g - g_hi.astype(F32)).astype(BF16)
    return _dot(tri_bf16, g_hi) + _dot(tri_bf16, g_lo)


def _glastate_kernel(k_ref, v_ref, lr_ref, wf_ref, bf_ref, wb_ref, bb_ref, sf_ref, sb_ref, *, n):
    k = k_ref[...].astype(F32)
    v = v_ref[...]
    lr = lr_ref[...]
    gf = _log_decay(lr, wf_ref[...], bf_ref[...])
    b = _cumsum_rows(_tri(n, False).astype(BF16), gf)
    sf_ref[...] = _dot_tn(v, (k * jnp.exp2(b[n - 1:n] - b)).astype(BF16))
    gb = _log_decay(lr, wb_ref[...], bb_ref[...])
    e = _cumsum_rows(_tri(n, True).astype(BF16), gb)
    sb_ref[...] = _dot_tn(v, (k * jnp.exp2(e[0:1] - e)).astype(BF16))


def _glastate(ctx_main, ctx_lr, wf, bf, wb, bb, *, batch, n_ctx):
    kb = COL_GK // GLA_DK
    vb = COL_GV // GLA_DV
    st = jax.ShapeDtypeStruct((batch, GLA_HEADS, GLA_DV, GLA_DK), F32)
    st_spec = pl.BlockSpec((None, None, GLA_DV, GLA_DK), lambda b, h: (b, h, 0, 0))
    w_spec = pl.BlockSpec((LR_PAD, GLA_DK), lambda b, h: (0, h))
    b_spec = pl.BlockSpec((1, GLA_DK), lambda b, h: (0, h))
    return pl.pallas_call(
        functools.partial(_glastate_kernel, n=n_ctx),
        grid=(batch, GLA_HEADS),
        in_specs=[pl.BlockSpec((n_ctx, GLA_DK), lambda b, h: (b, kb + h)),
                  pl.BlockSpec((n_ctx, GLA_DV), lambda b, h: (b, vb + h)),
                  pl.BlockSpec((n_ctx, LR_PAD), lambda b, h: (b, 0)),
                  w_spec, b_spec, w_spec, b_spec],
        out_specs=[st_spec, st_spec],
        out_shape=[st, st],
        compiler_params=pltpu.CompilerParams(vmem_limit_bytes=VMEM_LIMIT),
        name="gla_ctx_state",
    )(ctx_main, ctx_main, ctx_lr, wf, bf, wb, bb)


GLA_CHUNK = 64
GLA_TBLOCK = 1024


def _gla_kernel(qf_ref, kf_ref, vf_ref, lrf_ref, qb_ref, kb_ref, vb_ref, lrb_ref,
                wf_ref, bf_ref, wb_ref, bb_ref, sf0_ref, sb0_ref, of_ref, ob_ref, stf, stb):
    @pl.when(pl.program_id(2) == 0)
    def _():
        stf[...] = sf0_ref[...]
        stb[...] = sb0_ref[...]

    c, n = GLA_CHUNK, GLA_TBLOCK // GLA_CHUNK
    streams = ((lrf_ref, qf_ref, kf_ref, vf_ref, wf_ref, bf_ref, of_ref, stf, False),
               (lrb_ref, qb_ref, kb_ref, vb_ref, wb_ref, bb_ref, ob_ref, stb, True))
    tri, rows, q, k, v, b, b_mid, b_edge = [], [], [], [], [], [], [], []
    for lr_ref, q_ref, k_ref, v_ref, w_ref, bias_ref, _, _, upper in streams:
        t = _tri(c, upper)
        edge = 0 if upper else c - 1
        g = _log_decay(lr_ref[...], w_ref[...], bias_ref[...])
        bw = _cumsum_rows(t.astype(BF16), jnp.concatenate([g[i * c:(i + 1) * c] for i in range(n)], axis=1))
        for i in range(n):
            r = slice(i * c, (i + 1) * c)
            x = bw[:, i * GLA_DK:(i + 1) * GLA_DK]
            tri.append(t), rows.append(r), b.append(x)
            b_mid.append(x[c // 2:c // 2 + 1]), b_edge.append(x[edge:edge + 1])
            q.append(q_ref[r, :]), k.append(k_ref[r, :]), v.append(v_ref[r, :])
    u = range(2 * n)
    qt = [q[j].astype(F32) * jnp.exp2(b[j] - b_mid[j]) for j in u]
    kt = [k[j].astype(F32) * jnp.exp2(b_mid[j] - b[j]) for j in u]
    s = [_dot_nt(qt[j].astype(BF16), kt[j].astype(BF16)) for j in u]
    a = [jnp.where(tri[j], s[j], 0.0).astype(BF16) for j in u]
    k_out = [(kt[j] * jnp.exp2(b_edge[j] - b_mid[j])).astype(BF16) for j in u]
    upd = [_dot_tn(v[j], k_out[j]) for j in u]
    intra = [_dot(a[j], v[j]) for j in u]
    q_in = [(qt[j] * jnp.exp2(b_mid[j])).astype(BF16) for j in u]
    dec = [jnp.exp2(x) for x in b_edge]
    st = [stf[...], stb[...]]
    for step in range(n):
        for d, stream in enumerate(streams):
            j = d * n + (n - 1 - step if stream[8] else step)
            stream[6][rows[j], :] = intra[j] + _dot_nt(q_in[j], st[d].astype(BF16))
            st[d] = st[d] * dec[j] + upd[j]
    stf[...] = st[0]
    stb[...] = st[1]


def _gla(main, lr, wf, bf, wb, bb, sf0, sb0, *, batch, seq):
    nt = seq // GLA_TBLOCK
    tb = GLA_TBLOCK
    qb_, kb_, vb_ = COL_GQ // GLA_DK, COL_GK // GLA_DK, COL_GV // GLA_DV

    def fwd(col, per_head=True):
        return lambda b, h, t: (b * nt + t, col + (h if per_head else 0))

    def bwd(col, per_head=True):
        return lambda b, h, t: (b * nt + nt - 1 - t, col + (h if per_head else 0))

    def seq_specs(m):
        return [pl.BlockSpec((tb, GLA_DK), m(qb_)), pl.BlockSpec((tb, GLA_DK), m(kb_)),
                pl.BlockSpec((tb, GLA_DV), m(vb_)), pl.BlockSpec((tb, LR_PAD), m(0, False))]

    w_spec = pl.BlockSpec((LR_PAD, GLA_DK), lambda b, h, t: (0, h))
    b_spec = pl.BlockSpec((1, GLA_DK), lambda b, h, t: (0, h))
    st_spec = pl.BlockSpec((None, None, GLA_DV, GLA_DK), lambda b, h, t: (b, h, 0, 0))
    out = jax.ShapeDtypeStruct((batch * seq, GLA_HEADS * GLA_DV), F32)
    return pl.pallas_call(
        _gla_kernel,
        grid=(batch, GLA_HEADS, nt),
        in_specs=seq_specs(fwd) + seq_specs(bwd) + [w_spec, b_spec, w_spec, b_spec, st_spec, st_spec],
        out_specs=[pl.BlockSpec((tb, GLA_DV), fwd(0)), pl.BlockSpec((tb, GLA_DV), bwd(0))],
        out_shape=[out, out],
        scratch_shapes=[pltpu.VMEM((GLA_DV, GLA_DK), F32), pltpu.VMEM((GLA_DV, GLA_DK), F32)],
        compiler_params=pltpu.CompilerParams(
            dimension_semantics=("arbitrary", "arbitrary", "arbitrary"), vmem_limit_bytes=VMEM_LIMIT),
        name="gla",
    )(main, main, main, lr, main, main, main, lr, wf, bf, wb, bb, sf0, sb0)


ATT_TQ = 512
ATT_TK = 512


ATT_UNSHIFTED_MAX_LOG2 = 56.0
ATT_ONES_ROWS = 16


def _attn_kernel(bound_ref, q_ref, k_ref, v_ref, kc_ref, vc_ref, o_ref, vt_sc, vct_sc, m_sc, acc_sc, *, seq):
    tq, tk, hd = ATT_TQ, ATT_TK, HEAD_DIM
    nk = seq // tk

    def transposed(v):
        return v.astype(F32).T.astype(BF16)

    @pl.when(pl.program_id(2) == 0)
    def _():
        for j in range(nk):
            vt_sc[j, 0:hd, :] = transposed(v_ref[j * tk:(j + 1) * tk, :])
            vt_sc[j, hd:, :] = jnp.ones((ATT_ONES_ROWS, tk), BF16)
        vct_sc[0:hd, :] = transposed(vc_ref[...])
        vct_sc[hd:, :] = jnp.ones((ATT_ONES_ROWS, vct_sc.shape[1]), BF16)

    q4 = jnp.concatenate([q_ref[:, g * hd:(g + 1) * hd] for g in range(Q_PER_KV)], axis=0)
    acc_sc[...] = jnp.zeros(acc_sc.shape, F32)

    def over_keys(step, unroll):
        def body(j, carry):
            step(k_ref[pl.ds(pl.multiple_of(j * tk, tk), tk), :], vt_sc[j])
            return carry

        lax.fori_loop(0, nk, body, 0, unroll=unroll)
        step(kc_ref[...], vct_sc[...])

    def unshifted_step(kc, vt):
        p = jnp.exp2(_dot_nt(kc, q4)).astype(BF16)
        acc_sc[...] += _dot(vt, p)

    def online_step(kc, vt):
        s = _dot_nt(kc, q4)
        m_prev = m_sc[...]
        m_new = jnp.maximum(m_prev, jnp.max(s, axis=0, keepdims=True))
        p = jnp.exp2(s - m_new).astype(BF16)
        acc_sc[...] = jnp.exp2(m_prev - m_new) * acc_sc[...] + _dot(vt, p)
        m_sc[...] = m_new

    small = bound_ref[0, 0] <= ATT_UNSHIFTED_MAX_LOG2

    @pl.when(small)
    def _():
        over_keys(unshifted_step, True)

    @pl.when(jnp.logical_not(small))
    def _():
        m_sc[...] = jnp.full(m_sc.shape, -jnp.inf, F32)
        over_keys(online_step, 1)

    acc = acc_sc[...]
    o = (acc[:hd] / acc[hd:hd + 1]).T
    for g in range(Q_PER_KV):
        o_ref[:, g * hd:(g + 1) * hd] = o[g * tq:(g + 1) * tq].astype(BF16)


def _attn(score_bound, main, ctx_main, *, batch, seq, n_ctx):
    nq = seq // ATT_TQ
    gw = Q_PER_KV * HEAD_DIM
    rows = Q_PER_KV * ATT_TQ
    vt_rows = HEAD_DIM + ATT_ONES_ROWS
    return pl.pallas_call(
        functools.partial(_attn_kernel, seq=seq),
        grid=(batch, N_KV_HEADS, nq),
        in_specs=[pl.BlockSpec(memory_space=pltpu.SMEM),
                  pl.BlockSpec((ATT_TQ, gw), lambda b, h, i: (b * nq + i, COL_AQ // gw + h)),
                  pl.BlockSpec((seq, HEAD_DIM), lambda b, h, i: (b, COL_AK // HEAD_DIM + h)),
                  pl.BlockSpec((seq, HEAD_DIM), lambda b, h, i: (b, COL_AV // HEAD_DIM + h)),
                  pl.BlockSpec((n_ctx, HEAD_DIM), lambda b, h, i: (b, COL_AK // HEAD_DIM + h)),
                  pl.BlockSpec((n_ctx, HEAD_DIM), lambda b, h, i: (b, COL_AV // HEAD_DIM + h))],
        out_specs=pl.BlockSpec((ATT_TQ, gw), lambda b, h, i: (b * nq + i, h)),
        out_shape=jax.ShapeDtypeStruct((batch * seq, N_Q_HEADS * HEAD_DIM), BF16),
        scratch_shapes=[pltpu.VMEM((seq // ATT_TK, vt_rows, ATT_TK), BF16),
                        pltpu.VMEM((vt_rows, n_ctx), BF16),
                        pltpu.VMEM((1, rows), F32), pltpu.VMEM((vt_rows, rows), F32)],
        compiler_params=pltpu.CompilerParams(
            dimension_semantics=("arbitrary", "arbitrary", "arbitrary"), vmem_limit_bytes=VMEM_LIMIT),
        name="attn",
    )(score_bound, main, main, main, ctx_main, ctx_main)


def _merge_kernel(x_ref, attn_ref, of_ref, ob_ref, go_ref, ga_ref, gg_ref, gn_ref, gt_ref,
                  wa_ref, wg_ref, wo_ref, o_ref):
    gn = gn_ref[...]
    heads = []
    for h in range(GLA_HEADS):
        s = slice(h * GLA_DV, (h + 1) * GLA_DV)
        o = of_ref[:, s] + ob_ref[:, s]
        heads.append((_head_norm(o, gn) * go_ref[:, s].astype(F32)).astype(BF16))
    gla = jnp.concatenate(heads, axis=-1)
    ya = ga_ref[...].astype(F32) * _dot(attn_ref[...], wa_ref[...])
    yg = gg_ref[...].astype(F32) * _dot(gla, wg_ref[...])
    out = _dot((ya + yg).astype(BF16), wo_ref[...])
    o_ref[...] = x_ref[...] + gt_ref[...] * out


def _merge(x2d, attn, o_f, o_b, main, gla_norm, mod4, wa, wg, wo, *, tm, seq_tiles):
    rows, d = x2d.shape
    row = lambda i: (i, 0)
    return pl.pallas_call(
        _merge_kernel,
        grid=(rows // tm,),
        in_specs=[pl.BlockSpec((tm, d), row), pl.BlockSpec((tm, d), row),
                  pl.BlockSpec((tm, d), row), pl.BlockSpec((tm, d), row),
                  pl.BlockSpec((tm, d), lambda i: (i, COL_GO // d)),
                  pl.BlockSpec((tm, d), lambda i: (i, COL_GA // d)),
                  pl.BlockSpec((tm, d), lambda i: (i, COL_GG // d)),
                  _resident((1, GLA_DV), lambda i: (0, 0)),
                  pl.BlockSpec((None, None, 1, d), lambda i: (i // seq_tiles, 2, 0, 0)),
                  _resident((d, d), lambda i: (0, 0)), _resident((d, d), lambda i: (0, 0)),
                  _resident((d, d), lambda i: (0, 0))],
        out_specs=pl.BlockSpec((tm, d), row),
        out_shape=jax.ShapeDtypeStruct((rows, d), F32),
        compiler_params=pltpu.CompilerParams(vmem_limit_bytes=VMEM_LIMIT),
        name="merge_out",
    )(x2d, attn, o_f, o_b, main, main, main, gla_norm, mod4, wa, wg, wo)


FF_CHUNK = 1024


def _mlp_kernel(x_ref, sh_ref, sc_ref, gt_ref, n2_ref, w1_ref, w2_ref, o_ref, *, d_ff):
    x = x_ref[...]
    ms = jnp.mean(x * x, axis=-1, keepdims=True)
    h = x * lax.rsqrt(ms + EPS) * n2_ref[...]
    hb = (h * (1.0 + sc_ref[...]) + sh_ref[...]).astype(BF16)
    acc = jnp.zeros(x.shape, F32)
    for c in range(d_ff // FF_CHUNK):
        s = slice(c * FF_CHUNK, (c + 1) * FF_CHUNK)
        u = jnp.maximum(_dot(hb, w1_ref[:, s]), 0.0)
        acc = acc + _dot((u * u).astype(BF16), w2_ref[s, :])
    o_ref[...] = x + gt_ref[...] * acc


def _mlp(x2d, mod4, norm2, w1, w2, *, tm, seq_tiles):
    rows, d = x2d.shape
    d_ff = w1.shape[1]
    mod = lambda g: pl.BlockSpec((None, None, 1, d), lambda i: (i // seq_tiles, g, 0, 0))
    return pl.pallas_call(
        functools.partial(_mlp_kernel, d_ff=d_ff),
        grid=(rows // tm,),
        in_specs=[pl.BlockSpec((tm, d), lambda i: (i, 0)), mod(3), mod(4), mod(5),
                  _resident((1, d), lambda i: (0, 0)),
                  _resident((d, d_ff), lambda i: (0, 0)), _resident((d_ff, d), lambda i: (0, 0))],
        out_specs=pl.BlockSpec((tm, d), lambda i: (i, 0)),
        out_shape=jax.ShapeDtypeStruct((rows, d), F32),
        compiler_params=pltpu.CompilerParams(vmem_limit_bytes=VMEM_LIMIT),
        name="mlp",
    )(x2d, mod4, mod4, mod4, norm2, w1, w2)


def _rope_tables(seq):
    t = np.arange(seq)
    half = HEAD_DIM // 2
    freqs = ROPE_THETA ** (-np.arange(0, half, 2, dtype=np.float32) / half)
    ang_r = (t // GRID_W).astype(np.float32)[:, None] * freqs
    ang_c = (t % GRID_W).astype(np.float32)[:, None] * freqs
    cos = np.concatenate([np.cos(ang_r)] * 2 + [np.cos(ang_c)] * 2, axis=-1)
    sin = np.concatenate([-np.sin(ang_r), np.sin(ang_r), -np.sin(ang_c), np.sin(ang_c)], axis=-1)
    return jnp.asarray(cos, F32), jnp.asarray(sin, F32)


def _pad_rows(w, row0):
    return jnp.pad(w, ((row0, LR_PAD - row0 - w.shape[0]), (0, 0)))


def kernel(x, c, ctx, c_ctx, w_ada, b_ada, norm1, w_in, q_norm, k_norm, w_gk_fwd, b_gk_fwd,
           w_gk_bwd, b_gk_bwd, gla_norm, w_br_attn, w_br_gla, w_out, norm2, w_mlp1, w_mlp2):
    batch, seq, d = x.shape
    n_ctx = ctx.shape[1]
    depth = w_ada.shape[0]
    assert depth == 1 and seq % GLA_TBLOCK == 0 and seq % ATT_TK == 0 and batch <= 7
    tm = 512
    seq_tiles = seq // tm
    ctx_tm = n_ctx
    l = 0

    cc = jnp.zeros((8, d), F32).at[:batch].set(c).at[batch].set(c_ctx)
    mod4 = _ada(cc, w_ada[l], b_ada[l][None, :]).reshape(8, 6, 1, d)

    w_main, w_lr = _regroup_w_in(jnp.transpose(w_in[l]))
    cos_t, sin_t = _rope_tables(seq)
    q_scale = (HEAD_DIM ** -0.5) * float(np.log2(np.e))
    q_gain = (q_norm[l] * q_scale)[None, :]
    k_gain = k_norm[l][None, :]
    score_bound = (1.01 * HEAD_DIM * q_scale * jnp.max(jnp.abs(q_norm[l])) * jnp.max(jnp.abs(k_norm[l])))
    score_bound = score_bound.reshape(1, 1).astype(F32)
    n1 = norm1[l][None, :]

    main, lr = _inproj(x.reshape(batch * seq, d), mod4, lambda i: i // seq_tiles, n1, w_main, w_lr,
                       cos_t, sin_t, q_gain, k_gain, n_cols=N_MAIN, rope=True, tm=tm, seq_tiles=seq_tiles)
    ctx_main, ctx_lr = _inproj(ctx.reshape(batch * n_ctx, d), mod4, lambda i: batch, n1, w_main, w_lr,
                               cos_t, sin_t, q_gain, k_gain, n_cols=N_CTX_MAIN, rope=False, tm=ctx_tm,
                               seq_tiles=1)

    wf = _pad_rows(w_gk_fwd[l], 0).astype(BF16)
    wb = _pad_rows(w_gk_bwd[l], GLA_GATE_RANK).astype(BF16)
    bf, bb = b_gk_fwd[l][None, :], b_gk_bwd[l][None, :]
    sf0, sb0 = _glastate(ctx_main, ctx_lr, wf, bf, wb, bb, batch=batch, n_ctx=n_ctx)
    o_f, o_b = _gla(main, lr, wf, bf, wb, bb, sf0, sb0, batch=batch, seq=seq)

    attn = _attn(score_bound, main, ctx_main, batch=batch, seq=seq, n_ctx=n_ctx)

    x1 = _merge(x.reshape(batch * seq, d), attn, o_f, o_b, main, gla_norm[l][None, :], mod4,
                w_br_attn[l].astype(BF16), w_br_gla[l].astype(BF16), w_out[l].astype(BF16),
                tm=tm, seq_tiles=seq_tiles)
    x2 = _mlp(x1, mod4, norm2[l][None, :], w_mlp1[l].astype(BF16), w_mlp2[l].astype(BF16),
              tm=tm, seq_tiles=seq_tiles)
    return x2.reshape(batch, seq, d)
```

```python
import functools

import numpy as np
import jax
import jax.numpy as jnp
from jax import lax
from jax.experimental import pallas as pl
from jax.experimental.pallas import tpu as pltpu

F32 = jnp.float32
BF16 = jnp.bfloat16

GRID_W = 64
HEAD_DIM = 128
N_Q_HEADS = 8
N_KV_HEADS = 2
Q_PER_KV = N_Q_HEADS // N_KV_HEADS
ROPE_THETA = 10000.0
GLA_HEADS = 4
GLA_DK = 128
GLA_DV = 256
GLA_GATE_RANK = 16
GLA_GATE_NORM = 16.0
EPS = 1e-6
LOG2_E = float(np.log2(np.e))

COL_AK, COL_AV, COL_GK, COL_GV = 0, 256, 512, 1024
COL_AQ, COL_GO, COL_GA, COL_GG, COL_GQ = 2048, 3072, 4096, 5120, 6144
N_MAIN = 6656
N_CTX_MAIN = 2048
LR_PAD = 128
PROJ_CHUNK = 512

VMEM_LIMIT = 56 * 1024 * 1024


def _dot(a, b):
    return jnp.dot(a, b, preferred_element_type=F32)


def _dot_nt(a, b):
    return lax.dot_general(a, b, (((1,), (1,)), ((), ())), preferred_element_type=F32)


def _dot_tn(a, b):
    return lax.dot_general(a, b, (((0,), (0,)), ((), ())), preferred_element_type=F32)


def _sigmoid(x):
    return 0.5 * jnp.tanh(0.5 * x) + 0.5


def _resident(shape, index_map):
    return pl.BlockSpec(shape, index_map, pipeline_mode=pl.Buffered(1))


def _ada_kernel(c_ref, w_ref, b_ref, o_ref):
    c = c_ref[...]
    s = (c * _sigmoid(c)).astype(BF16)
    o_ref[...] = _dot(s, w_ref[...].astype(BF16)) + b_ref[...]


def _ada(cc, w_ada, b_ada):
    d, n = w_ada.shape
    tn = 1536
    return pl.pallas_call(
        _ada_kernel,
        grid=(n // tn,),
        in_specs=[pl.BlockSpec((8, d), lambda j: (0, 0)),
                  pl.BlockSpec((d, tn), lambda j: (0, j)),
                  pl.BlockSpec((1, tn), lambda j: (0, j))],
        out_specs=pl.BlockSpec((8, tn), lambda j: (0, j)),
        out_shape=jax.ShapeDtypeStruct((8, n), F32),
        compiler_params=pltpu.CompilerParams(vmem_limit_bytes=VMEM_LIMIT),
        name="ada",
    )(cc, w_ada, b_ada)


W_IN_LR = (2048, 2080)
W_IN_GROUPS = (((0, 2048), 1.0), ((2080, 3104), 1.0), ((3616, 4640), 1.0), ((4640, 5664), 1.0),
               ((5664, 6688), 1.0), ((3104, 3616), GLA_DK ** -0.5))
W_IN_CHUNKS = tuple((lo + o, scale) for (lo, hi), scale in W_IN_GROUPS for o in range(0, hi - lo, PROJ_CHUNK))
assert len(W_IN_CHUNKS) * PROJ_CHUNK == N_MAIN


def _chunk_table(j, column, divisor=1):
    entry = lambda idx: W_IN_CHUNKS[idx][column] // divisor if divisor != 1 else W_IN_CHUNKS[idx][column]
    out = entry(len(W_IN_CHUNKS) - 1)
    for idx in range(len(W_IN_CHUNKS) - 2, -1, -1):
        out = jnp.where(j == idx, entry(idx), out)
    return out


def _regroup_kernel(src_ref, lr_src_ref, main_ref, lr_ref):
    scale = _chunk_table(pl.program_id(0), 1).astype(F32)
    main_ref[...] = (src_ref[...] * scale).T.astype(BF16)
    rank2 = W_IN_LR[1] - W_IN_LR[0]
    lr_rows = jnp.concatenate([lr_src_ref[...], jnp.zeros((LR_PAD - rank2, lr_src_ref.shape[1]), F32)], axis=0)
    lr_ref[...] = lr_rows.T.astype(BF16)


def _regroup_w_in(w_t):
    n_in, d = w_t.shape
    rank2 = W_IN_LR[1] - W_IN_LR[0]
    return pl.pallas_call(
        _regroup_kernel,
        grid=(len(W_IN_CHUNKS),),
        in_specs=[pl.BlockSpec((pl.Element(PROJ_CHUNK), pl.Element(d)),
                               lambda j: (_chunk_table(j, 0, rank2) * rank2, 0)),
                  pl.BlockSpec((rank2, d), lambda j: (W_IN_LR[0] // rank2, 0))],
        out_specs=[pl.BlockSpec((d, PROJ_CHUNK), lambda j: (0, j)), pl.BlockSpec((d, LR_PAD), lambda j: (0, 0))],
        out_shape=[jax.ShapeDtypeStruct((d, N_MAIN), BF16), jax.ShapeDtypeStruct((d, LR_PAD), BF16)],
        compiler_params=pltpu.CompilerParams(vmem_limit_bytes=VMEM_LIMIT),
        name="regroup_w_in",
    )(w_t, w_t)


def _head_norm(a, gain):
    ms = jnp.mean(a * a, axis=-1, keepdims=True)
    return a * lax.rsqrt(ms + EPS) * gain


def _rope(n, cos, sin_signed):
    lane = lax.broadcasted_iota(jnp.int32, n.shape, 1)
    partner = jnp.where((lane % 64) < 32, pltpu.roll(n, 96, 1), pltpu.roll(n, 32, 1))
    return n * cos + partner * sin_signed


def _inproj_kernel(x_ref, sh_ref, sc_ref, n1_ref, w_ref, wlr_ref, cos_ref, sin_ref, qg_ref, kg_ref,
                   main_ref, lr_ref, *, n_cols, rope):
    x = x_ref[...]
    ms = jnp.mean(x * x, axis=-1, keepdims=True)
    h = x * lax.rsqrt(ms + EPS) * n1_ref[...]
    hb = (h * (1.0 + sc_ref[...]) + sh_ref[...]).astype(BF16)
    lr_ref[...] = _dot(hb, wlr_ref[...])

    def qk_head(a, gain):
        n = _head_norm(a, gain)
        if rope:
            n = _rope(n, cos_ref[...], sin_ref[...])
        return n.astype(BF16)

    for c in range(n_cols // PROJ_CHUNK):
        lo = c * PROJ_CHUNK
        acc = _dot(hb, w_ref[:, lo:lo + PROJ_CHUNK])
        for s in range(PROJ_CHUNK // HEAD_DIM):
            col = lo + s * HEAD_DIM
            a = acc[:, s * HEAD_DIM:(s + 1) * HEAD_DIM]
            if COL_AK <= col < COL_AV:
                main_ref[:, col:col + HEAD_DIM] = qk_head(a, kg_ref[...])
            elif COL_AQ <= col < COL_GO:
                main_ref[:, col:col + HEAD_DIM] = qk_head(a, qg_ref[...])
            elif COL_GO <= col < COL_GA:
                main_ref[:, col:col + HEAD_DIM] = (a * _sigmoid(a)).astype(BF16)
            elif COL_GA <= col < COL_GQ:
                main_ref[:, col:col + HEAD_DIM] = _sigmoid(a).astype(BF16)
            else:
                main_ref[:, col:col + HEAD_DIM] = a.astype(BF16)


def _inproj(x2d, mod4, mod_row_of_tile, norm1, w_main, w_lr, cos_t, sin_t, q_gain, k_gain,
            *, n_cols, rope, tm, seq_tiles):
    rows, d = x2d.shape
    kern = functools.partial(_inproj_kernel, n_cols=n_cols, rope=rope)
    return pl.pallas_call(
        kern,
        grid=(rows // tm,),
        in_specs=[
            pl.BlockSpec((tm, d), lambda i: (i, 0)),
            pl.BlockSpec((None, None, 1, d), lambda i: (mod_row_of_tile(i), 0, 0, 0)),
            pl.BlockSpec((None, None, 1, d), lambda i: (mod_row_of_tile(i), 1, 0, 0)),
            _resident((1, d), lambda i: (0, 0)),
            _resident((d, n_cols), lambda i: (0, 0)),
            _resident((d, LR_PAD), lambda i: (0, 0)),
            pl.BlockSpec((tm, HEAD_DIM), lambda i: (i % seq_tiles, 0)),
            pl.BlockSpec((tm, HEAD_DIM), lambda i: (i % seq_tiles, 0)),
            _resident((1, HEAD_DIM), lambda i: (0, 0)),
            _resident((1, HEAD_DIM), lambda i: (0, 0)),
        ],
        out_specs=[pl.BlockSpec((tm, n_cols), lambda i: (i, 0)),
                   pl.BlockSpec((tm, LR_PAD), lambda i: (i, 0))],
        out_shape=[jax.ShapeDtypeStruct((rows, n_cols), BF16),
                   jax.ShapeDtypeStruct((rows, LR_PAD), F32)],
        compiler_params=pltpu.CompilerParams(vmem_limit_bytes=VMEM_LIMIT),
        name="inproj_rope" if rope else "inproj_ctx",
    )(x2d, mod4, mod4, norm1, w_main, w_lr, cos_t, sin_t, q_gain, k_gain)


def _log_decay(lr, w, b):
    z = _dot(lr.astype(BF16), w) + b
    return (jnp.minimum(z, 0.0) - jnp.log(1.0 + jnp.exp(-jnp.abs(z)))) * (LOG2_E / GLA_GATE_NORM)


def _tri(n, upper):
    r = lax.broadcasted_iota(jnp.int32, (n, n), 0)
    c = lax.broadcasted_iota(jnp.int32, (n, n), 1)
    return (c >= r) if upper else (r >= c)


def _cumsum_rows(tri_bf16, g):
    g_hi = g.astype(BF16)
    g_lo = (g - g_hi.astype(F32)).astype(BF16)
    return _dot(tri_bf16, g_hi) + _dot(tri_bf16, g_lo)


def _glastate_kernel(k_ref, v_ref, lr_ref, wf_ref, bf_ref, wb_ref, bb_ref, sf_ref, sb_ref, *, n):
    k = k_ref[...].astype(F32)
    lr = lr_ref[...]
    b = _cumsum_rows(_tri(n, False).astype(BF16), _log_decay(lr, wf_ref[...], bf_ref[...]))
    k_fwd = (k * jnp.exp2(b[n - 1:n] - b)).astype(BF16)
    e = _cumsum_rows(_tri(n, True).astype(BF16), _log_decay(lr, wb_ref[...], bb_ref[...]))
    k_bwd = (k * jnp.exp2(e[0:1] - e)).astype(BF16)
    for h in range(GLA_HEADS):
        v = v_ref[:, h * GLA_DV:(h + 1) * GLA_DV]
        kc = slice(h * GLA_DK, (h + 1) * GLA_DK)
        sf_ref[h] = _dot_tn(v, k_fwd[:, kc])
        sb_ref[h] = _dot_tn(v, k_bwd[:, kc])


def _glastate(ctx_main, ctx_lr, wf, bf, wb, bb, *, batch, n_ctx):
    kw, vw = GLA_HEADS * GLA_DK, GLA_HEADS * GLA_DV
    st = jax.ShapeDtypeStruct((batch, GLA_HEADS, GLA_DV, GLA_DK), F32)
    st_spec = pl.BlockSpec((None, GLA_HEADS, GLA_DV, GLA_DK), lambda b: (b, 0, 0, 0))
    w_spec = pl.BlockSpec((LR_PAD, kw), lambda b: (0, 0))
    b_spec = pl.BlockSpec((1, kw), lambda b: (0, 0))
    return pl.pallas_call(
        functools.partial(_glastate_kernel, n=n_ctx),
        grid=(batch,),
        in_specs=[pl.BlockSpec((n_ctx, kw), lambda b: (b, COL_GK // kw)),
                  pl.BlockSpec((n_ctx, vw), lambda b: (b, COL_GV // vw)),
                  pl.BlockSpec((n_ctx, LR_PAD), lambda b: (b, 0)),
                  w_spec, b_spec, w_spec, b_spec],
        out_specs=[st_spec, st_spec],
        out_shape=[st, st],
        compiler_params=pltpu.CompilerParams(vmem_limit_bytes=VMEM_LIMIT),
        name="gla_ctx_state",
    )(ctx_main, ctx_main, ctx_lr, wf, bf, wb, bb)


GLA_CHUNK = 64
GLA_TBLOCK = 1024


def _gla_kernel(qf_ref, kf_ref, vf_ref, lrf_ref, qb_ref, kb_ref, vb_ref, lrb_ref,
                wf_ref, bf_ref, wb_ref, bb_ref, sf0_ref, sb0_ref, of_ref, ob_ref, stf, stb):
    @pl.when(pl.program_id(2) == 0)
    def _():
        stf[...] = sf0_ref[...]
        stb[...] = sb0_ref[...]

    c, n = GLA_CHUNK, GLA_TBLOCK // GLA_CHUNK
    streams = ((lrf_ref, qf_ref, kf_ref, vf_ref, wf_ref, bf_ref, of_ref, stf, False),
               (lrb_ref, qb_ref, kb_ref, vb_ref, wb_ref, bb_ref, ob_ref, stb, True))
    tri, rows, q, k, v, b, b_mid, b_edge = [], [], [], [], [], [], [], []
    for lr_ref, q_ref, k_ref, v_ref, w_ref, bias_ref, _, _, upper in streams:
        t = _tri(c, upper)
        edge = 0 if upper else c - 1
        g = _log_decay(lr_ref[...], w_ref[...], bias_ref[...])
        bw = _cumsum_rows(t.astype(BF16), jnp.concatenate([g[i * c:(i + 1) * c] for i in range(n)], axis=1))
        for i in range(n):
            r = slice(i * c, (i + 1) * c)
            x = bw[:, i * GLA_DK:(i + 1) * GLA_DK]
            tri.append(t), rows.append(r), b.append(x)
            b_mid.append(x[c // 2:c // 2 + 1]), b_edge.append(x[edge:edge + 1])
            q.append(q_ref[r, :]), k.append(k_ref[r, :]), v.append(v_ref[r, :])
    u = range(2 * n)
    qt = [q[j].astype(F32) * jnp.exp2(b[j] - b_mid[j]) for j in u]
    kt = [k[j].astype(F32) * jnp.exp2(b_mid[j] - b[j]) for j in u]
    s = [_dot_nt(qt[j].astype(BF16), kt[j].astype(BF16)) for j in u]
    a = [jnp.where(tri[j], s[j], 0.0).astype(BF16) for j in u]
    k_out = [(kt[j] * jnp.exp2(b_edge[j] - b_mid[j])).astype(BF16) for j in u]
    upd = [_dot_tn(v[j], k_out[j]) for j in u]
    intra = [_dot(a[j], v[j]) for j in u]
    q_in = [(qt[j] * jnp.exp2(b_mid[j])).astype(BF16) for j in u]
    dec = [jnp.exp2(x) for x in b_edge]
    st = [stf[...], stb[...]]
    for step in range(n):
        for d, stream in enumerate(streams):
            j = d * n + (n - 1 - step if stream[8] else step)
            stream[6][rows[j], :] = (intra[j] + _dot_nt(q_in[j], st[d].astype(BF16))).astype(BF16)
            st[d] = st[d] * dec[j] + upd[j]
    stf[...] = st[0]
    stb[...] = st[1]


def _gla(main, lr, wf, bf, wb, bb, sf0, sb0, *, batch, seq):
    nt = seq // GLA_TBLOCK
    tb = GLA_TBLOCK
    qb_, kb_, vb_ = COL_GQ // GLA_DK, COL_GK // GLA_DK, COL_GV // GLA_DV

    def fwd(col, per_head=True):
        return lambda b, h, t: (b * nt + t, col + (h if per_head else 0))

    def bwd(col, per_head=True):
        return lambda b, h, t: (b * nt + nt - 1 - t, col + (h if per_head else 0))

    def seq_specs(m):
        return [pl.BlockSpec((tb, GLA_DK), m(qb_)), pl.BlockSpec((tb, GLA_DK), m(kb_)),
                pl.BlockSpec((tb, GLA_DV), m(vb_)), pl.BlockSpec((tb, LR_PAD), m(0, False))]

    w_spec = pl.BlockSpec((LR_PAD, GLA_DK), lambda b, h, t: (0, h))
    b_spec = pl.BlockSpec((1, GLA_DK), lambda b, h, t: (0, h))
    st_spec = pl.BlockSpec((None, None, GLA_DV, GLA_DK), lambda b, h, t: (b, h, 0, 0))
    out = jax.ShapeDtypeStruct((batch * seq, GLA_HEADS * GLA_DV), BF16)
    return pl.pallas_call(
        _gla_kernel,
        grid=(batch, GLA_HEADS, nt),
        in_specs=seq_specs(fwd) + seq_specs(bwd) + [w_spec, b_spec, w_spec, b_spec, st_spec, st_spec],
        out_specs=[pl.BlockSpec((tb, GLA_DV), fwd(0)), pl.BlockSpec((tb, GLA_DV), bwd(0))],
        out_shape=[out, out],
        scratch_shapes=[pltpu.VMEM((GLA_DV, GLA_DK), F32), pltpu.VMEM((GLA_DV, GLA_DK), F32)],
        compiler_params=pltpu.CompilerParams(
            dimension_semantics=("arbitrary", "arbitrary", "arbitrary"), vmem_limit_bytes=VMEM_LIMIT),
        name="gla",
    )(main, main, main, lr, main, main, main, lr, wf, bf, wb, bb, sf0, sb0)


ATT_TQ = 512
ATT_TK = 512


ATT_UNSHIFTED_MAX_LOG2 = 56.0
ATT_ONES_ROWS = 16


def _attn_kernel(bound_ref, q_ref, k_ref, v_ref, kc_ref, vc_ref, o_ref, vt_sc, vct_sc, m_sc, acc_sc, *, seq):
    tq, tk, hd = ATT_TQ, ATT_TK, HEAD_DIM
    nk = seq // tk

    def transposed(v):
        return v.astype(F32).T.astype(BF16)

    @pl.when(pl.program_id(2) == 0)
    def _():
        for j in range(nk):
            vt_sc[j, 0:hd, :] = transposed(v_ref[j * tk:(j + 1) * tk, :])
            vt_sc[j, hd:, :] = jnp.ones((ATT_ONES_ROWS, tk), BF16)
        vct_sc[0:hd, :] = transposed(vc_ref[...])
        vct_sc[hd:, :] = jnp.ones((ATT_ONES_ROWS, vct_sc.shape[1]), BF16)

    q4 = jnp.concatenate([q_ref[:, g * hd:(g + 1) * hd] for g in range(Q_PER_KV)], axis=0)
    acc_sc[...] = jnp.zeros(acc_sc.shape, F32)

    def over_keys(step, unroll):
        def body(j, carry):
            step(k_ref[pl.ds(pl.multiple_of(j * tk, tk), tk), :], vt_sc[j])
            return carry

        lax.fori_loop(0, nk, body, 0, unroll=unroll)
        step(kc_ref[...], vct_sc[...])

    def unshifted_step(kc, vt):
        p = jnp.exp2(_dot_nt(kc, q4)).astype(BF16)
        acc_sc[...] += _dot(vt, p)

    def online_step(kc, vt):
        s = _dot_nt(kc, q4)
        m_prev = m_sc[...]
        m_new = jnp.maximum(m_prev, jnp.max(s, axis=0, keepdims=True))
        p = jnp.exp2(s - m_new).astype(BF16)
        acc_sc[...] = jnp.exp2(m_prev - m_new) * acc_sc[...] + _dot(vt, p)
        m_sc[...] = m_new

    small = bound_ref[0, 0] <= ATT_UNSHIFTED_MAX_LOG2

    @pl.when(small)
    def _():
        over_keys(unshifted_step, True)

    @pl.when(jnp.logical_not(small))
    def _():
        m_sc[...] = jnp.full(m_sc.shape, -jnp.inf, F32)
        over_keys(online_step, 1)

    acc = acc_sc[...]
    o = (acc[:hd] / acc[hd:hd + 1]).T
    for g in range(Q_PER_KV):
        o_ref[:, g * hd:(g + 1) * hd] = o[g * tq:(g + 1) * tq].astype(BF16)


def _attn(score_bound, main, ctx_main, *, batch, seq, n_ctx):
    nq = seq // ATT_TQ
    gw = Q_PER_KV * HEAD_DIM
    rows = Q_PER_KV * ATT_TQ
    vt_rows = HEAD_DIM + ATT_ONES_ROWS
    return pl.pallas_call(
        functools.partial(_attn_kernel, seq=seq),
        grid=(batch, N_KV_HEADS, nq),
        in_specs=[pl.BlockSpec(memory_space=pltpu.SMEM),
                  pl.BlockSpec((ATT_TQ, gw), lambda b, h, i: (b * nq + i, COL_AQ // gw + h)),
                  pl.BlockSpec((seq, HEAD_DIM), lambda b, h, i: (b, COL_AK // HEAD_DIM + h)),
                  pl.BlockSpec((seq, HEAD_DIM), lambda b, h, i: (b, COL_AV // HEAD_DIM + h)),
                  pl.BlockSpec((n_ctx, HEAD_DIM), lambda b, h, i: (b, COL_AK // HEAD_DIM + h)),
                  pl.BlockSpec((n_ctx, HEAD_DIM), lambda b, h, i: (b, COL_AV // HEAD_DIM + h))],
        out_specs=pl.BlockSpec((ATT_TQ, gw), lambda b, h, i: (b * nq + i, h)),
        out_shape=jax.ShapeDtypeStruct((batch * seq, N_Q_HEADS * HEAD_DIM), BF16),
        scratch_shapes=[pltpu.VMEM((seq // ATT_TK, vt_rows, ATT_TK), BF16),
                        pltpu.VMEM((vt_rows, n_ctx), BF16),
                        pltpu.VMEM((1, rows), F32), pltpu.VMEM((vt_rows, rows), F32)],
        compiler_params=pltpu.CompilerParams(
            dimension_semantics=("arbitrary", "arbitrary", "arbitrary"), vmem_limit_bytes=VMEM_LIMIT),
        name="attn",
    )(score_bound, main, main, main, ctx_main, ctx_main)


FF_CHUNK = 1024


def _tail_kernel(x_ref, attn_ref, of_ref, ob_ref, go_ref, ga_ref, gg_ref, gn_ref, gt1_ref,
                 wa_ref, wg_ref, wo_ref, sh_ref, sc_ref, gt2_ref, n2_ref, w1_ref, w2_ref, o_ref, *, d_ff):
    gn = gn_ref[...]
    heads = []
    for h in range(GLA_HEADS):
        s = slice(h * GLA_DV, (h + 1) * GLA_DV)
        o = of_ref[:, s].astype(F32) + ob_ref[:, s].astype(F32)
        heads.append((_head_norm(o, gn) * go_ref[:, s].astype(F32)).astype(BF16))
    gla = jnp.concatenate(heads, axis=-1)
    ya = ga_ref[...].astype(F32) * _dot(attn_ref[...], wa_ref[...])
    yg = gg_ref[...].astype(F32) * _dot(gla, wg_ref[...])
    x1 = x_ref[...] + gt1_ref[...] * _dot((ya + yg).astype(BF16), wo_ref[...])
    ms = jnp.mean(x1 * x1, axis=-1, keepdims=True)
    h2 = x1 * lax.rsqrt(ms + EPS) * n2_ref[...]
    hb = (h2 * (1.0 + sc_ref[...]) + sh_ref[...]).astype(BF16)
    acc = jnp.zeros(x1.shape, F32)
    for c in range(d_ff // FF_CHUNK):
        s = slice(c * FF_CHUNK, (c + 1) * FF_CHUNK)
        u = jnp.maximum(_dot(hb, w1_ref[:, s]), 0.0)
        acc = acc + _dot((u * u).astype(BF16), w2_ref[s, :])
    o_ref[...] = x1 + gt2_ref[...] * acc


def _tail(x2d, attn, o_f, o_b, main, gla_norm, mod4, wa, wg, wo, norm2, w1, w2, *, tm, seq_tiles):
    rows, d = x2d.shape
    d_ff = w1.shape[1]
    tile = pl.BlockSpec((tm, d), lambda i: (i, 0))
    mod = lambda g: pl.BlockSpec((None, None, 1, d), lambda i: (i // seq_tiles, g, 0, 0))
    const = lambda shape: _resident(shape, lambda i: (0, 0))
    return pl.pallas_call(
        functools.partial(_tail_kernel, d_ff=d_ff),
        grid=(rows // tm,),
        in_specs=[tile, tile, tile, tile,
                  pl.BlockSpec((tm, d), lambda i: (i, COL_GO // d)),
                  pl.BlockSpec((tm, d), lambda i: (i, COL_GA // d)),
                  pl.BlockSpec((tm, d), lambda i: (i, COL_GG // d)),
                  const((1, GLA_DV)), mod(2), const((d, d)), const((d, d)), const((d, d)),
                  mod(3), mod(4), mod(5), const((1, d)), const((d, d_ff)), const((d_ff, d))],
        out_specs=tile,
        out_shape=jax.ShapeDtypeStruct((rows, d), F32),
        compiler_params=pltpu.CompilerParams(vmem_limit_bytes=VMEM_LIMIT),
        name="merge_out_mlp",
    )(x2d, attn, o_f, o_b, main, main, main, gla_norm, mod4, wa, wg, wo, mod4, mod4, mod4, norm2, w1, w2)


def _rope_tables(seq):
    t = np.arange(seq)
    half = HEAD_DIM // 2
    freqs = ROPE_THETA ** (-np.arange(0, half, 2, dtype=np.float32) / half)
    ang_r = (t // GRID_W).astype(np.float32)[:, None] * freqs
    ang_c = (t % GRID_W).astype(np.float32)[:, None] * freqs
    cos = np.concatenate([np.cos(ang_r)] * 2 + [np.cos(ang_c)] * 2, axis=-1)
    sin = np.concatenate([-np.sin(ang_r), np.sin(ang_r), -np.sin(ang_c), np.sin(ang_c)], axis=-1)
    return jnp.asarray(cos, F32), jnp.asarray(sin, F32)


def _pad_rows(w, row0):
    return jnp.pad(w, ((row0, LR_PAD - row0 - w.shape[0]), (0, 0)))


def kernel(x, c, ctx, c_ctx, w_ada, b_ada, norm1, w_in, q_norm, k_norm, w_gk_fwd, b_gk_fwd,
           w_gk_bwd, b_gk_bwd, gla_norm, w_br_attn, w_br_gla, w_out, norm2, w_mlp1, w_mlp2):
    batch, seq, d = x.shape
    n_ctx = ctx.shape[1]
    depth = w_ada.shape[0]
    assert depth == 1 and seq % GLA_TBLOCK == 0 and seq % ATT_TK == 0 and batch <= 7
    tm = 512
    seq_tiles = seq // tm
    ctx_tm = n_ctx
    l = 0

    cc = jnp.zeros((8, d), F32).at[:batch].set(c).at[batch].set(c_ctx)
    mod4 = _ada(cc, w_ada[l], b_ada[l][None, :]).reshape(8, 6, 1, d)

    w_main, w_lr = _regroup_w_in(jnp.transpose(w_in[l]))
    cos_t, sin_t = _rope_tables(seq)
    q_scale = (HEAD_DIM ** -0.5) * float(np.log2(np.e))
    q_gain = (q_norm[l] * q_scale)[None, :]
    k_gain = k_norm[l][None, :]
    score_bound = (1.01 * HEAD_DIM * q_scale * jnp.max(jnp.abs(q_norm[l])) * jnp.max(jnp.abs(k_norm[l])))
    score_bound = score_bound.reshape(1, 1).astype(F32)
    n1 = norm1[l][None, :]

    main, lr = _inproj(x.reshape(batch * seq, d), mod4, lambda i: i // seq_tiles, n1, w_main, w_lr,
                       cos_t, sin_t, q_gain, k_gain, n_cols=N_MAIN, rope=True, tm=tm, seq_tiles=seq_tiles)
    ctx_main, ctx_lr = _inproj(ctx.reshape(batch * n_ctx, d), mod4, lambda i: batch, n1, w_main, w_lr,
                               cos_t, sin_t, q_gain, k_gain, n_cols=N_CTX_MAIN, rope=False, tm=ctx_tm,
                               seq_tiles=1)

    wf = _pad_rows(w_gk_fwd[l], 0).astype(BF16)
    wb = _pad_rows(w_gk_bwd[l], GLA_GATE_RANK).astype(BF16)
    bf, bb = b_gk_fwd[l][None, :], b_gk_bwd[l][None, :]
    sf0, sb0 = _glastate(ctx_main, ctx_lr, wf, bf, wb, bb, batch=batch, n_ctx=n_ctx)
    o_f, o_b = _gla(main, lr, wf, bf, wb, bb, sf0, sb0, batch=batch, seq=seq)

    attn = _attn(score_bound, main, ctx_main, batch=batch, seq=seq, n_ctx=n_ctx)

    x2 = _tail(x.reshape(batch * seq, d), attn, o_f, o_b, main, gla_norm[l][None, :], mod4,
               w_br_attn[l].astype(BF16), w_br_gla[l].astype(BF16), w_out[l].astype(BF16),
               norm2[l][None, :], w_mlp1[l].astype(BF16), w_mlp2[l].astype(BF16),
               tm=tm, seq_tiles=seq_tiles)
    return x2.reshape(batch, seq, d)
```

```python
import functools

import numpy as np
import jax
import jax.numpy as jnp
from jax import lax
from jax.experimental import pallas as pl
from jax.experimental.pallas import tpu as pltpu

F32 = jnp.float32
BF16 = jnp.bfloat16

GRID_W = 64
HEAD_DIM = 128
N_Q_HEADS = 8
N_KV_HEADS = 2
Q_PER_KV = N_Q_HEADS // N_KV_HEADS
ROPE_THETA = 10000.0
GLA_HEADS = 4
GLA_DK = 128
GLA_DV = 256
GLA_GATE_RANK = 16
GLA_GATE_NORM = 16.0
EPS = 1e-6
LOG2_E = float(np.log2(np.e))

COL_AK, COL_AV, COL_GK, COL_GV = 0, 256, 512, 1024
COL_AQ, COL_GO, COL_GA, COL_GG, COL_GQ = 2048, 3072, 4096, 5120, 6144
N_MAIN = 6656
N_CTX_MAIN = 2048
LR_PAD = 128
PROJ_CHUNK = 512
PROJ_ROW_SPLIT = 2

VMEM_LIMIT = 56 * 1024 * 1024


def _dot(a, b):
    return jnp.dot(a, b, preferred_element_type=F32)


def _dot_nt(a, b):
    return lax.dot_general(a, b, (((1,), (1,)), ((), ())), preferred_element_type=F32)


def _dot_tn(a, b):
    return lax.dot_general(a, b, (((0,), (0,)), ((), ())), preferred_element_type=F32)


def _sigmoid(x):
    return 0.5 * jnp.tanh(0.5 * x) + 0.5


def _resident(shape, index_map):
    return pl.BlockSpec(shape, index_map, pipeline_mode=pl.Buffered(1))


def _ada_kernel(c_ref, w_ref, b_ref, o_ref):
    c = c_ref[...]
    s = (c * _sigmoid(c)).astype(BF16)
    o_ref[...] = _dot(s, w_ref[...].astype(BF16)) + b_ref[...]


def _ada(cc, w_ada, b_ada):
    d, n = w_ada.shape
    tn = 1536
    return pl.pallas_call(
        _ada_kernel,
        grid=(n // tn,),
        in_specs=[pl.BlockSpec((8, d), lambda j: (0, 0)),
                  pl.BlockSpec((d, tn), lambda j: (0, j)),
                  pl.BlockSpec((1, tn), lambda j: (0, j))],
        out_specs=pl.BlockSpec((8, tn), lambda j: (0, j)),
        out_shape=jax.ShapeDtypeStruct((8, n), F32),
        compiler_params=pltpu.CompilerParams(vmem_limit_bytes=VMEM_LIMIT),
        name="ada",
    )(cc, w_ada, b_ada)


W_IN_LR = (2048, 2080)
W_IN_GROUPS = (((0, 2048), 1.0), ((2080, 3104), 1.0), ((3616, 4640), 1.0), ((4640, 5664), 1.0),
               ((5664, 6688), 1.0), ((3104, 3616), GLA_DK ** -0.5))
W_IN_CHUNKS = tuple((lo + o, scale) for (lo, hi), scale in W_IN_GROUPS for o in range(0, hi - lo, PROJ_CHUNK))
assert len(W_IN_CHUNKS) * PROJ_CHUNK == N_MAIN


def _chunk_table(j, column, divisor=1):
    entry = lambda idx: W_IN_CHUNKS[idx][column] // divisor if divisor != 1 else W_IN_CHUNKS[idx][column]
    out = entry(len(W_IN_CHUNKS) - 1)
    for idx in range(len(W_IN_CHUNKS) - 2, -1, -1):
        out = jnp.where(j == idx, entry(idx), out)
    return out


def _regroup_kernel(src_ref, lr_src_ref, main_ref, lr_ref):
    scale = _chunk_table(pl.program_id(0), 1).astype(F32)
    main_ref[...] = (src_ref[...] * scale).T.astype(BF16)
    rank2 = W_IN_LR[1] - W_IN_LR[0]
    lr_rows = jnp.concatenate([lr_src_ref[...], jnp.zeros((LR_PAD - rank2, lr_src_ref.shape[1]), F32)], axis=0)
    lr_ref[...] = lr_rows.T.astype(BF16)


def _regroup_w_in(w_t):
    n_in, d = w_t.shape
    rank2 = W_IN_LR[1] - W_IN_LR[0]
    return pl.pallas_call(
        _regroup_kernel,
        grid=(len(W_IN_CHUNKS),),
        in_specs=[pl.BlockSpec((pl.Element(PROJ_CHUNK), pl.Element(d)),
                               lambda j: (_chunk_table(j, 0, rank2) * rank2, 0)),
                  pl.BlockSpec((rank2, d), lambda j: (W_IN_LR[0] // rank2, 0))],
        out_specs=[pl.BlockSpec((d, PROJ_CHUNK), lambda j: (0, j)), pl.BlockSpec((d, LR_PAD), lambda j: (0, 0))],
        out_shape=[jax.ShapeDtypeStruct((d, N_MAIN), BF16), jax.ShapeDtypeStruct((d, LR_PAD), BF16)],
        compiler_params=pltpu.CompilerParams(vmem_limit_bytes=VMEM_LIMIT),
        name="regroup_w_in",
    )(w_t, w_t)


def _head_norm(a, gain):
    ms = jnp.mean(a * a, axis=-1, keepdims=True)
    return a * lax.rsqrt(ms + EPS) * gain


def _rope(n, cos, sin_signed):
    lane = lax.broadcasted_iota(jnp.int32, n.shape, 1)
    partner = jnp.where((lane % 64) < 32, pltpu.roll(n, 96, 1), pltpu.roll(n, 32, 1))
    return n * cos + partner * sin_signed


def _log2_decay(lowrank, w, b):
    z = _dot(lowrank.astype(BF16), w) + b
    return (jnp.minimum(z, 0.0) - jnp.log(1.0 + jnp.exp(-jnp.abs(z)))) * (LOG2_E / GLA_GATE_NORM)


def _inproj_kernel(x_ref, sh_ref, sc_ref, n1_ref, w_ref, wlr_ref, wgk_ref, bgk_ref, cos_ref, sin_ref,
                   qg_ref, kg_ref, main_ref, g_ref, *, n_cols, rope):
    half = x_ref.shape[0] // PROJ_ROW_SPLIT
    for part in range(PROJ_ROW_SPLIT):
        r = slice(part * half, (part + 1) * half)
        x = x_ref[r, :]
        ms = jnp.mean(x * x, axis=-1, keepdims=True)
        h = x * lax.rsqrt(ms + EPS) * n1_ref[...]
        hb = (h * (1.0 + sc_ref[...]) + sh_ref[...]).astype(BF16)

        def qk_head(a, gain):
            n = _head_norm(a, gain)
            if rope:
                n = _rope(n, cos_ref[r, :], sin_ref[r, :])
            return n.astype(BF16)

        for c in range(n_cols // PROJ_CHUNK):
            lo = c * PROJ_CHUNK
            acc = _dot(hb, w_ref[:, lo:lo + PROJ_CHUNK])
            for s in range(PROJ_CHUNK // HEAD_DIM):
                col = lo + s * HEAD_DIM
                a = acc[:, s * HEAD_DIM:(s + 1) * HEAD_DIM]
                if COL_AK <= col < COL_AV:
                    main_ref[r, col:col + HEAD_DIM] = qk_head(a, kg_ref[...])
                elif COL_AQ <= col < COL_GO:
                    main_ref[r, col:col + HEAD_DIM] = qk_head(a, qg_ref[...])
                elif COL_GO <= col < COL_GA:
                    main_ref[r, col:col + HEAD_DIM] = (a * _sigmoid(a)).astype(BF16)
                elif COL_GA <= col < COL_GQ:
                    main_ref[r, col:col + HEAD_DIM] = _sigmoid(a).astype(BF16)
                else:
                    main_ref[r, col:col + HEAD_DIM] = a.astype(BF16)
            if lo == COL_GV:
                g_ref[r, :] = _log2_decay(_dot(hb, wlr_ref[...]), wgk_ref[...], bgk_ref[...]).astype(BF16)


def _inproj(x2d, mod4, mod_row_of_tile, norm1, w_main, w_lr, w_gk, b_gk, cos_t, sin_t, q_gain, k_gain,
            *, n_cols, rope, tm, seq_tiles):
    rows, d = x2d.shape
    n_g = w_gk.shape[1]
    kern = functools.partial(_inproj_kernel, n_cols=n_cols, rope=rope)
    return pl.pallas_call(
        kern,
        grid=(rows // tm,),
        in_specs=[
            pl.BlockSpec((tm, d), lambda i: (i, 0)),
            pl.BlockSpec((None, None, 1, d), lambda i: (mod_row_of_tile(i), 0, 0, 0)),
            pl.BlockSpec((None, None, 1, d), lambda i: (mod_row_of_tile(i), 1, 0, 0)),
            _resident((1, d), lambda i: (0, 0)),
            _resident((d, n_cols), lambda i: (0, 0)),
            _resident((d, LR_PAD), lambda i: (0, 0)),
            _resident((LR_PAD, n_g), lambda i: (0, 0)),
            _resident((1, n_g), lambda i: (0, 0)),
            pl.BlockSpec((tm, HEAD_DIM), lambda i: (i % seq_tiles, 0)),
            pl.BlockSpec((tm, HEAD_DIM), lambda i: (i % seq_tiles, 0)),
            _resident((1, HEAD_DIM), lambda i: (0, 0)),
            _resident((1, HEAD_DIM), lambda i: (0, 0)),
        ],
        out_specs=[pl.BlockSpec((tm, n_cols), lambda i: (i, 0)),
                   pl.BlockSpec((tm, n_g), lambda i: (i, 0))],
        out_shape=[jax.ShapeDtypeStruct((rows, n_cols), BF16),
                   jax.ShapeDtypeStruct((rows, n_g), BF16)],
        compiler_params=pltpu.CompilerParams(vmem_limit_bytes=VMEM_LIMIT),
        name="inproj_rope" if rope else "inproj_ctx",
    )(x2d, mod4, mod4, norm1, w_main, w_lr, w_gk, b_gk, cos_t, sin_t, q_gain, k_gain)


def _tri(n, upper):
    r = lax.broadcasted_iota(jnp.int32, (n, n), 0)
    c = lax.broadcasted_iota(jnp.int32, (n, n), 1)
    return (c >= r) if upper else (r >= c)


def _running_sums(tri, g):
    return _dot(tri.astype(BF16), g)


def _glastate_kernel(k_ref, v_ref, gf_ref, gb_ref, sf_ref, sb_ref, *, n):
    k = k_ref[...].astype(F32)
    b = _running_sums(_tri(n, False), gf_ref[...])
    k_fwd = (k * jnp.exp2(b[n - 1:n] - b)).astype(BF16)
    e = _running_sums(_tri(n, True), gb_ref[...])
    k_bwd = (k * jnp.exp2(e[0:1] - e)).astype(BF16)
    for h in range(GLA_HEADS):
        v = v_ref[:, h * GLA_DV:(h + 1) * GLA_DV]
        kc = slice(h * GLA_DK, (h + 1) * GLA_DK)
        sf_ref[h] = _dot_tn(v, k_fwd[:, kc])
        sb_ref[h] = _dot_tn(v, k_bwd[:, kc])


def _glastate(ctx_main, ctx_g, *, batch, n_ctx):
    kw, vw = GLA_HEADS * GLA_DK, GLA_HEADS * GLA_DV
    st = jax.ShapeDtypeStruct((batch, GLA_HEADS, GLA_DV, GLA_DK), F32)
    st_spec = pl.BlockSpec((None, GLA_HEADS, GLA_DV, GLA_DK), lambda b: (b, 0, 0, 0))
    return pl.pallas_call(
        functools.partial(_glastate_kernel, n=n_ctx),
        grid=(batch,),
        in_specs=[pl.BlockSpec((n_ctx, kw), lambda b: (b, COL_GK // kw)),
                  pl.BlockSpec((n_ctx, vw), lambda b: (b, COL_GV // vw)),
                  pl.BlockSpec((n_ctx, kw), lambda b: (b, 0)),
                  pl.BlockSpec((n_ctx, kw), lambda b: (b, 1))],
        out_specs=[st_spec, st_spec],
        out_shape=[st, st],
        compiler_params=pltpu.CompilerParams(vmem_limit_bytes=VMEM_LIMIT),
        name="gla_ctx_state",
    )(ctx_main, ctx_main, ctx_g, ctx_g)


GLA_CHUNK = 64
GLA_TBLOCK = 1024
GLA_WAVE = 4


def _gla_kernel(qf_ref, kf_ref, vf_ref, gf_ref, qb_ref, kb_ref, vb_ref, gb_ref,
                sf0_ref, sb0_ref, of_ref, ob_ref, stf, stb):
    @pl.when(pl.program_id(2) == 0)
    def _():
        stf[...] = sf0_ref[...]
        stb[...] = sb0_ref[...]

    c, n = GLA_CHUNK, GLA_TBLOCK // GLA_CHUNK
    streams = ((gf_ref, qf_ref, kf_ref, vf_ref, of_ref, False),
               (gb_ref, qb_ref, kb_ref, vb_ref, ob_ref, True))
    tri, rows, q, k, v, b, b_mid, b_edge = [], [], [], [], [], [], [], []
    for g_ref, q_ref, k_ref, v_ref, _, upper in streams:
        t = _tri(c, upper)
        edge = 0 if upper else c - 1
        bw = _running_sums(t, jnp.concatenate([g_ref[i * c:(i + 1) * c, :] for i in range(n)], axis=1))
        for i in range(n):
            r = slice(i * c, (i + 1) * c)
            x = bw[:, i * GLA_DK:(i + 1) * GLA_DK]
            tri.append(t), rows.append(r), b.append(x)
            b_mid.append(x[c // 2:c // 2 + 1]), b_edge.append(x[edge:edge + 1])
            q.append(q_ref[r, :]), k.append(k_ref[r, :]), v.append(v_ref[r, :])
    st = [stf[...], stb[...]]
    for w0 in range(0, n, GLA_WAVE):
        u = [(d, d * n + (n - 1 - step if streams[d][5] else step))
             for step in range(w0, w0 + GLA_WAVE) for d in range(len(streams))]
        qt = {j: q[j].astype(F32) * jnp.exp2(b[j] - b_mid[j]) for _, j in u}
        kt = {j: k[j].astype(F32) * jnp.exp2(b_mid[j] - b[j]) for _, j in u}
        s = {j: _dot_nt(qt[j].astype(BF16), kt[j].astype(BF16)) for _, j in u}
        a = {j: jnp.where(tri[j], s[j], 0.0).astype(BF16) for _, j in u}
        k_out = {j: (kt[j] * jnp.exp2(b_edge[j] - b_mid[j])).astype(BF16) for _, j in u}
        upd = {j: _dot_tn(v[j], k_out[j]) for _, j in u}
        intra = {j: _dot(a[j], v[j]) for _, j in u}
        q_in = {j: (qt[j] * jnp.exp2(b_mid[j])).astype(BF16) for _, j in u}
        for d, j in u:
            streams[d][4][rows[j], :] = (intra[j] + _dot_nt(q_in[j], st[d].astype(BF16))).astype(BF16)
            st[d] = st[d] * jnp.exp2(b_edge[j]) + upd[j]
    stf[...] = st[0]
    stb[...] = st[1]


def _gla(main, g, sf0, sb0, *, batch, seq):
    nt = seq // GLA_TBLOCK
    tb = GLA_TBLOCK
    qb_, kb_, vb_ = COL_GQ // GLA_DK, COL_GK // GLA_DK, COL_GV // GLA_DV

    def fwd(col):
        return lambda b, h, t: (b * nt + t, col + h)

    def bwd(col):
        return lambda b, h, t: (b * nt + nt - 1 - t, col + h)

    def seq_specs(m, g_col):
        return [pl.BlockSpec((tb, GLA_DK), m(qb_)), pl.BlockSpec((tb, GLA_DK), m(kb_)),
                pl.BlockSpec((tb, GLA_DV), m(vb_)), pl.BlockSpec((tb, GLA_DK), m(g_col))]

    st_spec = pl.BlockSpec((None, None, GLA_DV, GLA_DK), lambda b, h, t: (b, h, 0, 0))
    out = jax.ShapeDtypeStruct((batch * seq, GLA_HEADS * GLA_DV), BF16)
    return pl.pallas_call(
        _gla_kernel,
        grid=(batch, GLA_HEADS, nt),
        in_specs=seq_specs(fwd, 0) + seq_specs(bwd, GLA_HEADS) + [st_spec, st_spec],
        out_specs=[pl.BlockSpec((tb, GLA_DV), fwd(0)), pl.BlockSpec((tb, GLA_DV), bwd(0))],
        out_shape=[out, out],
        scratch_shapes=[pltpu.VMEM((GLA_DV, GLA_DK), F32), pltpu.VMEM((GLA_DV, GLA_DK), F32)],
        compiler_params=pltpu.CompilerParams(
            dimension_semantics=("arbitrary", "arbitrary", "arbitrary"), vmem_limit_bytes=VMEM_LIMIT),
        name="gla",
    )(main, main, main, g, main, main, main, g, sf0, sb0)


ATT_TQ = 512
ATT_TK = 512


ATT_UNSHIFTED_MAX_LOG2 = 56.0
ATT_ONES_ROWS = 16


def _attn_kernel(bound_ref, q_ref, k_ref, v_ref, kc_ref, vc_ref, o_ref, vt_sc, vct_sc, m_sc, acc_sc, *, seq):
    tq, tk, hd = ATT_TQ, ATT_TK, HEAD_DIM
    nk = seq // tk

    def transposed(v):
        return v.astype(F32).T.astype(BF16)

    @pl.when(pl.program_id(2) == 0)
    def _():
        for j in range(nk):
            vt_sc[j, 0:hd, :] = transposed(v_ref[j * tk:(j + 1) * tk, :])
            vt_sc[j, hd:, :] = jnp.ones((ATT_ONES_ROWS, tk), BF16)
        vct_sc[0:hd, :] = transposed(vc_ref[...])
        vct_sc[hd:, :] = jnp.ones((ATT_ONES_ROWS, vct_sc.shape[1]), BF16)

    q4 = jnp.concatenate([q_ref[:, g * hd:(g + 1) * hd] for g in range(Q_PER_KV)], axis=0)
    acc_sc[...] = jnp.zeros(acc_sc.shape, F32)

    def over_keys(step, unroll):
        def body(j, carry):
            step(k_ref[pl.ds(pl.multiple_of(j * tk, tk), tk), :], vt_sc[j])
            return carry

        lax.fori_loop(0, nk, body, 0, unroll=unroll)
        step(kc_ref[...], vct_sc[...])

    def unshifted_step(kc, vt):
        p = jnp.exp2(_dot_nt(kc, q4)).astype(BF16)
        acc_sc[...] += _dot(vt, p)

    def online_step(kc, vt):
        s = _dot_nt(kc, q4)
        m_prev = m_sc[...]
        m_new = jnp.maximum(m_prev, jnp.max(s, axis=0, keepdims=True))
        p = jnp.exp2(s - m_new).astype(BF16)
        acc_sc[...] = jnp.exp2(m_prev - m_new) * acc_sc[...] + _dot(vt, p)
        m_sc[...] = m_new

    small = bound_ref[0, 0] <= ATT_UNSHIFTED_MAX_LOG2

    @pl.when(small)
    def _():
        over_keys(unshifted_step, True)

    @pl.when(jnp.logical_not(small))
    def _():
        m_sc[...] = jnp.full(m_sc.shape, -jnp.inf, F32)
        over_keys(online_step, 1)

    acc = acc_sc[...]
    o = (acc[:hd] / acc[hd:hd + 1]).T
    for g in range(Q_PER_KV):
        o_ref[:, g * hd:(g + 1) * hd] = o[g * tq:(g + 1) * tq].astype(BF16)


def _attn(score_bound, main, ctx_main, *, batch, seq, n_ctx):
    nq = seq // ATT_TQ
    gw = Q_PER_KV * HEAD_DIM
    rows = Q_PER_KV * ATT_TQ
    vt_rows = HEAD_DIM + ATT_ONES_ROWS
    return pl.pallas_call(
        functools.partial(_attn_kernel, seq=seq),
        grid=(batch, N_KV_HEADS, nq),
        in_specs=[pl.BlockSpec(memory_space=pltpu.SMEM),
                  pl.BlockSpec((ATT_TQ, gw), lambda b, h, i: (b * nq + i, COL_AQ // gw + h)),
                  pl.BlockSpec((seq, HEAD_DIM), lambda b, h, i: (b, COL_AK // HEAD_DIM + h)),
                  pl.BlockSpec((seq, HEAD_DIM), lambda b, h, i: (b, COL_AV // HEAD_DIM + h)),
                  pl.BlockSpec((n_ctx, HEAD_DIM), lambda b, h, i: (b, COL_AK // HEAD_DIM + h)),
                  pl.BlockSpec((n_ctx, HEAD_DIM), lambda b, h, i: (b, COL_AV // HEAD_DIM + h))],
        out_specs=pl.BlockSpec((ATT_TQ, gw), lambda b, h, i: (b * nq + i, h)),
        out_shape=jax.ShapeDtypeStruct((batch * seq, N_Q_HEADS * HEAD_DIM), BF16),
        scratch_shapes=[pltpu.VMEM((seq // ATT_TK, vt_rows, ATT_TK), BF16),
                        pltpu.VMEM((vt_rows, n_ctx), BF16),
                        pltpu.VMEM((1, rows), F32), pltpu.VMEM((vt_rows, rows), F32)],
        compiler_params=pltpu.CompilerParams(
            dimension_semantics=("arbitrary", "arbitrary", "arbitrary"), vmem_limit_bytes=VMEM_LIMIT),
        name="attn",
    )(score_bound, main, main, main, ctx_main, ctx_main)


FF_CHUNK = 1024


def _tail_kernel(x_ref, attn_ref, of_ref, ob_ref, go_ref, ga_ref, gg_ref, gn_ref, gt1_ref,
                 wa_ref, wg_ref, wo_ref, sh_ref, sc_ref, gt2_ref, n2_ref, w1_ref, w2_ref, o_ref, *, d_ff):
    gn = gn_ref[...]
    heads = []
    for h in range(GLA_HEADS):
        s = slice(h * GLA_DV, (h + 1) * GLA_DV)
        o = of_ref[:, s].astype(F32) + ob_ref[:, s].astype(F32)
        heads.append((_head_norm(o, gn) * go_ref[:, s].astype(F32)).astype(BF16))
    gla = jnp.concatenate(heads, axis=-1)
    ya = ga_ref[...].astype(F32) * _dot(attn_ref[...], wa_ref[...])
    yg = gg_ref[...].astype(F32) * _dot(gla, wg_ref[...])
    x1 = x_ref[...] + gt1_ref[...] * _dot((ya + yg).astype(BF16), wo_ref[...])
    ms = jnp.mean(x1 * x1, axis=-1, keepdims=True)
    h2 = x1 * lax.rsqrt(ms + EPS) * n2_ref[...]
    hb = (h2 * (1.0 + sc_ref[...]) + sh_ref[...]).astype(BF16)
    acc = jnp.zeros(x1.shape, F32)
    for c in range(d_ff // FF_CHUNK):
        s = slice(c * FF_CHUNK, (c + 1) * FF_CHUNK)
        u = jnp.maximum(_dot(hb, w1_ref[:, s]), 0.0)
        acc = acc + _dot((u * u).astype(BF16), w2_ref[s, :])
    o_ref[...] = x1 + gt2_ref[...] * acc


def _tail(x2d, attn, o_f, o_b, main, gla_norm, mod4, wa, wg, wo, norm2, w1, w2, *, tm, seq_tiles):
    rows, d = x2d.shape
    d_ff = w1.shape[1]
    tile = pl.BlockSpec((tm, d), lambda i: (i, 0))
    mod = lambda g: pl.BlockSpec((None, None, 1, d), lambda i: (i // seq_tiles, g, 0, 0))
    const = lambda shape: _resident(shape, lambda i: (0, 0))
    return pl.pallas_call(
        functools.partial(_tail_kernel, d_ff=d_ff),
        grid=(rows // tm,),
        in_specs=[tile, tile, tile, tile,
                  pl.BlockSpec((tm, d), lambda i: (i, COL_GO // d)),
                  pl.BlockSpec((tm, d), lambda i: (i, COL_GA // d)),
                  pl.BlockSpec((tm, d), lambda i: (i, COL_GG // d)),
                  const((1, GLA_DV)), mod(2), const((d, d)), const((d, d)), const((d, d)),
                  mod(3), mod(4), mod(5), const((1, d)), const((d, d_ff)), const((d_ff, d))],
        out_specs=tile,
        out_shape=jax.ShapeDtypeStruct((rows, d), F32),
        compiler_params=pltpu.CompilerParams(vmem_limit_bytes=VMEM_LIMIT),
        name="merge_out_mlp",
    )(x2d, attn, o_f, o_b, main, main, main, gla_norm, mod4, wa, wg, wo, mod4, mod4, mod4, norm2, w1, w2)


def _rope_tables(seq):
    t = np.arange(seq)
    half = HEAD_DIM // 2
    freqs = ROPE_THETA ** (-np.arange(0, half, 2, dtype=np.float32) / half)
    ang_r = (t // GRID_W).astype(np.float32)[:, None] * freqs
    ang_c = (t % GRID_W).astype(np.float32)[:, None] * freqs
    cos = np.concatenate([np.cos(ang_r)] * 2 + [np.cos(ang_c)] * 2, axis=-1)
    sin = np.concatenate([-np.sin(ang_r), np.sin(ang_r), -np.sin(ang_c), np.sin(ang_c)], axis=-1)
    return jnp.asarray(cos, F32), jnp.asarray(sin, F32)


def _pad_rows(w, row0):
    return jnp.pad(w, ((row0, LR_PAD - row0 - w.shape[0]), (0, 0)))


def kernel(x, c, ctx, c_ctx, w_ada, b_ada, norm1, w_in, q_norm, k_norm, w_gk_fwd, b_gk_fwd,
           w_gk_bwd, b_gk_bwd, gla_norm, w_br_attn, w_br_gla, w_out, norm2, w_mlp1, w_mlp2):
    batch, seq, d = x.shape
    n_ctx = ctx.shape[1]
    depth = w_ada.shape[0]
    assert depth == 1 and seq % GLA_TBLOCK == 0 and seq % ATT_TK == 0 and batch <= 7
    tm = 512
    seq_tiles = seq // tm
    ctx_tm = n_ctx
    l = 0

    cc = jnp.zeros((8, d), F32).at[:batch].set(c).at[batch].set(c_ctx)
    mod4 = _ada(cc, w_ada[l], b_ada[l][None, :]).reshape(8, 6, 1, d)

    w_main, w_lr = _regroup_w_in(jnp.transpose(w_in[l]))
    cos_t, sin_t = _rope_tables(seq)
    q_scale = (HEAD_DIM ** -0.5) * float(np.log2(np.e))
    q_gain = (q_norm[l] * q_scale)[None, :]
    k_gain = k_norm[l][None, :]
    score_bound = (1.01 * HEAD_DIM * q_scale * jnp.max(jnp.abs(q_norm[l])) * jnp.max(jnp.abs(k_norm[l])))
    score_bound = score_bound.reshape(1, 1).astype(F32)
    n1 = norm1[l][None, :]

    w_gk = jnp.concatenate([_pad_rows(w_gk_fwd[l], 0), _pad_rows(w_gk_bwd[l], GLA_GATE_RANK)], axis=1).astype(BF16)
    b_gk = jnp.concatenate([b_gk_fwd[l], b_gk_bwd[l]])[None, :]

    main, g = _inproj(x.reshape(batch * seq, d), mod4, lambda i: i // seq_tiles, n1, w_main, w_lr, w_gk, b_gk,
                      cos_t, sin_t, q_gain, k_gain, n_cols=N_MAIN, rope=True, tm=tm, seq_tiles=seq_tiles)
    ctx_main, ctx_g = _inproj(ctx.reshape(batch * n_ctx, d), mod4, lambda i: batch, n1, w_main, w_lr, w_gk, b_gk,
                              cos_t, sin_t, q_gain, k_gain, n_cols=N_CTX_MAIN, rope=False, tm=ctx_tm,
                              seq_tiles=1)

    sf0, sb0 = _glastate(ctx_main, ctx_g, batch=batch, n_ctx=n_ctx)
    o_f, o_b = _gla(main, g, sf0, sb0, batch=batch, seq=seq)

    attn = _attn(score_bound, main, ctx_main, batch=batch, seq=seq, n_ctx=n_ctx)

    x2 = _tail(x.reshape(batch * seq, d), attn, o_f, o_b, main, gla_norm[l][None, :], mod4,
               w_br_attn[l].astype(BF16), w_br_gla[l].astype(BF16), w_out[l].astype(BF16),
               norm2[l][None, :], w_mlp1[l].astype(BF16), w_mlp2[l].astype(BF16),
               tm=tm, seq_tiles=seq_tiles)
    return x2.reshape(batch, seq, d)
```

```python
import functools

import numpy as np
import jax
import jax.numpy as jnp
from jax import lax
from jax.experimental import pallas as pl
from jax.experimental.pallas import tpu as pltpu

F32 = jnp.float32
BF16 = jnp.bfloat16

GRID_W = 64
HEAD_DIM = 128
N_Q_HEADS = 8
N_KV_HEADS = 2
Q_PER_KV = N_Q_HEADS // N_KV_HEADS
ROPE_THETA = 10000.0
GLA_HEADS = 4
GLA_DK = 128
GLA_DV = 256
GLA_GATE_RANK = 16
GLA_GATE_NORM = 16.0
EPS = 1e-6
LOG2_E = float(np.log2(np.e))

COL_AK, COL_AV, COL_GK, COL_GV = 0, 256, 512, 1024
COL_AQ, COL_GO, COL_GA, COL_GG, COL_GQ = 2048, 3072, 4096, 5120, 6144
N_MAIN = 6656
N_CTX_MAIN = 2048
LR_PAD = 128
PROJ_CHUNK = 512
PROJ_ROW_SPLIT = 2

VMEM_LIMIT = 56 * 1024 * 1024


def _dot(a, b):
    return jnp.dot(a, b, preferred_element_type=F32)


def _dot_nt(a, b):
    return lax.dot_general(a, b, (((1,), (1,)), ((), ())), preferred_element_type=F32)


def _dot_tn(a, b):
    return lax.dot_general(a, b, (((0,), (0,)), ((), ())), preferred_element_type=F32)


def _sigmoid(x):
    return 0.5 * jnp.tanh(0.5 * x) + 0.5


def _resident(shape, index_map):
    return pl.BlockSpec(shape, index_map, pipeline_mode=pl.Buffered(1))


def _ada_kernel(c_ref, w_ref, b_ref, o_ref):
    c = c_ref[...]
    s = (c * _sigmoid(c)).astype(BF16)
    o_ref[...] = _dot(s, w_ref[...].astype(BF16)) + b_ref[...]


def _ada(cc, w_ada, b_ada):
    d, n = w_ada.shape
    tn = 1536
    return pl.pallas_call(
        _ada_kernel,
        grid=(n // tn,),
        in_specs=[pl.BlockSpec((8, d), lambda j: (0, 0)),
                  pl.BlockSpec((d, tn), lambda j: (0, j)),
                  pl.BlockSpec((1, tn), lambda j: (0, j))],
        out_specs=pl.BlockSpec((8, tn), lambda j: (0, j)),
        out_shape=jax.ShapeDtypeStruct((8, n), F32),
        compiler_params=pltpu.CompilerParams(vmem_limit_bytes=VMEM_LIMIT),
        name="ada",
    )(cc, w_ada, b_ada)


W_IN_LR = (2048, 2080)
W_IN_GROUPS = (((0, 2048), 1.0), ((2080, 3104), 1.0), ((3616, 4640), 1.0), ((4640, 5664), 1.0),
               ((5664, 6688), 1.0), ((3104, 3616), GLA_DK ** -0.5))
W_IN_CHUNKS = tuple((lo + o, scale) for (lo, hi), scale in W_IN_GROUPS for o in range(0, hi - lo, PROJ_CHUNK))
assert len(W_IN_CHUNKS) * PROJ_CHUNK == N_MAIN


def _chunk_table(j, column, divisor=1):
    entry = lambda idx: W_IN_CHUNKS[idx][column] // divisor if divisor != 1 else W_IN_CHUNKS[idx][column]
    out = entry(len(W_IN_CHUNKS) - 1)
    for idx in range(len(W_IN_CHUNKS) - 2, -1, -1):
        out = jnp.where(j == idx, entry(idx), out)
    return out


def _regroup_kernel(src_ref, lr_src_ref, main_ref, lr_ref):
    scale = _chunk_table(pl.program_id(0), 1).astype(F32)
    main_ref[...] = (src_ref[...] * scale).T.astype(BF16)
    rank2 = W_IN_LR[1] - W_IN_LR[0]
    lr_rows = jnp.concatenate([lr_src_ref[...], jnp.zeros((LR_PAD - rank2, lr_src_ref.shape[1]), F32)], axis=0)
    lr_ref[...] = lr_rows.T.astype(BF16)


def _regroup_w_in(w_t):
    n_in, d = w_t.shape
    rank2 = W_IN_LR[1] - W_IN_LR[0]
    return pl.pallas_call(
        _regroup_kernel,
        grid=(len(W_IN_CHUNKS),),
        in_specs=[pl.BlockSpec((pl.Element(PROJ_CHUNK), pl.Element(d)),
                               lambda j: (_chunk_table(j, 0, rank2) * rank2, 0)),
                  pl.BlockSpec((rank2, d), lambda j: (W_IN_LR[0] // rank2, 0))],
        out_specs=[pl.BlockSpec((d, PROJ_CHUNK), lambda j: (0, j)), pl.BlockSpec((d, LR_PAD), lambda j: (0, 0))],
        out_shape=[jax.ShapeDtypeStruct((d, N_MAIN), BF16), jax.ShapeDtypeStruct((d, LR_PAD), BF16)],
        compiler_params=pltpu.CompilerParams(vmem_limit_bytes=VMEM_LIMIT),
        name="regroup_w_in",
    )(w_t, w_t)


def _head_norm(a, gain):
    ms = jnp.mean(a * a, axis=-1, keepdims=True)
    return a * lax.rsqrt(ms + EPS) * gain


def _rope(n, cos, sin_signed):
    lane = lax.broadcasted_iota(jnp.int32, n.shape, 1)
    partner = jnp.where((lane % 64) < 32, pltpu.roll(n, 96, 1), pltpu.roll(n, 32, 1))
    return n * cos + partner * sin_signed


def _log2_decay(lowrank, w, b):
    z = _dot(lowrank.astype(BF16), w) + b
    return (jnp.minimum(z, 0.0) - jnp.log(1.0 + jnp.exp(-jnp.abs(z)))) * (LOG2_E / GLA_GATE_NORM)


def _inproj_kernel(x_ref, sh_ref, sc_ref, n1_ref, w_ref, wlr_ref, wgk_ref, bgk_ref, cos_ref, sin_ref,
                   qg_ref, kg_ref, main_ref, g_ref, *, n_cols, rope):
    half = x_ref.shape[0] // PROJ_ROW_SPLIT
    g_slices = min(8, n_cols // PROJ_CHUNK)
    g_width = g_ref.shape[1] // g_slices
    for part in range(PROJ_ROW_SPLIT):
        r = slice(part * half, (part + 1) * half)
        x = x_ref[r, :]
        ms = jnp.mean(x * x, axis=-1, keepdims=True)
        h = x * lax.rsqrt(ms + EPS) * n1_ref[...]
        hb = (h * (1.0 + sc_ref[...]) + sh_ref[...]).astype(BF16)

        def qk_head(a, gain):
            n = _head_norm(a, gain)
            if rope:
                n = _rope(n, cos_ref[r, :], sin_ref[r, :])
            return n.astype(BF16)

        for c in range(n_cols // PROJ_CHUNK):
            lo = c * PROJ_CHUNK
            acc = _dot(hb, w_ref[:, lo:lo + PROJ_CHUNK])
            for s in range(PROJ_CHUNK // HEAD_DIM):
                col = lo + s * HEAD_DIM
                a = acc[:, s * HEAD_DIM:(s + 1) * HEAD_DIM]
                if COL_AK <= col < COL_AV:
                    main_ref[r, col:col + HEAD_DIM] = qk_head(a, kg_ref[...])
                elif COL_AQ <= col < COL_GO:
                    main_ref[r, col:col + HEAD_DIM] = qk_head(a, qg_ref[...])
                elif COL_GO <= col < COL_GA:
                    main_ref[r, col:col + HEAD_DIM] = (a * _sigmoid(a)).astype(BF16)
                elif COL_GA <= col < COL_GQ:
                    main_ref[r, col:col + HEAD_DIM] = _sigmoid(a).astype(BF16)
                else:
                    main_ref[r, col:col + HEAD_DIM] = a.astype(BF16)
            if c < g_slices:
                if c == 0:
                    lowrank = _dot(hb, wlr_ref[...])
                gs = slice(c * g_width, (c + 1) * g_width)
                g_ref[r, gs] = _log2_decay(lowrank, wgk_ref[:, gs], bgk_ref[:, gs]).astype(BF16)


def _inproj(x2d, mod4, mod_row_of_tile, norm1, w_main, w_lr, w_gk, b_gk, cos_t, sin_t, q_gain, k_gain,
            *, n_cols, rope, tm, seq_tiles):
    rows, d = x2d.shape
    n_g = w_gk.shape[1]
    kern = functools.partial(_inproj_kernel, n_cols=n_cols, rope=rope)
    return pl.pallas_call(
        kern,
        grid=(rows // tm,),
        in_specs=[
            pl.BlockSpec((tm, d), lambda i: (i, 0)),
            pl.BlockSpec((None, None, 1, d), lambda i: (mod_row_of_tile(i), 0, 0, 0)),
            pl.BlockSpec((None, None, 1, d), lambda i: (mod_row_of_tile(i), 1, 0, 0)),
            _resident((1, d), lambda i: (0, 0)),
            _resident((d, n_cols), lambda i: (0, 0)),
            _resident((d, LR_PAD), lambda i: (0, 0)),
            _resident((LR_PAD, n_g), lambda i: (0, 0)),
            _resident((1, n_g), lambda i: (0, 0)),
            pl.BlockSpec((tm, HEAD_DIM), lambda i: (i % seq_tiles, 0)),
            pl.BlockSpec((tm, HEAD_DIM), lambda i: (i % seq_tiles, 0)),
            _resident((1, HEAD_DIM), lambda i: (0, 0)),
            _resident((1, HEAD_DIM), lambda i: (0, 0)),
        ],
        out_specs=[pl.BlockSpec((tm, n_cols), lambda i: (i, 0)),
                   pl.BlockSpec((tm, n_g), lambda i: (i, 0))],
        out_shape=[jax.ShapeDtypeStruct((rows, n_cols), BF16),
                   jax.ShapeDtypeStruct((rows, n_g), BF16)],
        compiler_params=pltpu.CompilerParams(vmem_limit_bytes=VMEM_LIMIT),
        name="inproj_rope" if rope else "inproj_ctx",
    )(x2d, mod4, mod4, norm1, w_main, w_lr, w_gk, b_gk, cos_t, sin_t, q_gain, k_gain)


def _tri(n, upper):
    r = lax.broadcasted_iota(jnp.int32, (n, n), 0)
    c = lax.broadcasted_iota(jnp.int32, (n, n), 1)
    return (c >= r) if upper else (r >= c)


def _running_sums(tri, g):
    return _dot(tri.astype(BF16), g)


def _glastate_kernel(k_ref, v_ref, gf_ref, gb_ref, sf_ref, sb_ref, *, n):
    k = k_ref[...].astype(F32)
    b = _running_sums(_tri(n, False), gf_ref[...])
    k_fwd = (k * jnp.exp2(b[n - 1:n] - b)).astype(BF16)
    e = _running_sums(_tri(n, True), gb_ref[...])
    k_bwd = (k * jnp.exp2(e[0:1] - e)).astype(BF16)
    for h in range(GLA_HEADS):
        v = v_ref[:, h * GLA_DV:(h + 1) * GLA_DV]
        kc = slice(h * GLA_DK, (h + 1) * GLA_DK)
        sf_ref[h] = _dot_tn(v, k_fwd[:, kc])
        sb_ref[h] = _dot_tn(v, k_bwd[:, kc])


def _glastate(ctx_main, ctx_g, *, batch, n_ctx):
    kw, vw = GLA_HEADS * GLA_DK, GLA_HEADS * GLA_DV
    st = jax.ShapeDtypeStruct((batch, GLA_HEADS, GLA_DV, GLA_DK), F32)
    st_spec = pl.BlockSpec((None, GLA_HEADS, GLA_DV, GLA_DK), lambda b: (b, 0, 0, 0))
    return pl.pallas_call(
        functools.partial(_glastate_kernel, n=n_ctx),
        grid=(batch,),
        in_specs=[pl.BlockSpec((n_ctx, kw), lambda b: (b, COL_GK // kw)),
                  pl.BlockSpec((n_ctx, vw), lambda b: (b, COL_GV // vw)),
                  pl.BlockSpec((n_ctx, kw), lambda b: (b, 0)),
                  pl.BlockSpec((n_ctx, kw), lambda b: (b, 1))],
        out_specs=[st_spec, st_spec],
        out_shape=[st, st],
        compiler_params=pltpu.CompilerParams(vmem_limit_bytes=VMEM_LIMIT),
        name="gla_ctx_state",
    )(ctx_main, ctx_main, ctx_g, ctx_g)


GLA_CHUNK = 64
GLA_TBLOCK = 1024
GLA_WAVE = 4


def _gla_kernel(qf_ref, kf_ref, vf_ref, gf_ref, qb_ref, kb_ref, vb_ref, gb_ref,
                sf0_ref, sb0_ref, of_ref, ob_ref, stf, stb):
    @pl.when(pl.program_id(2) == 0)
    def _():
        stf[...] = sf0_ref[...]
        stb[...] = sb0_ref[...]

    c, n = GLA_CHUNK, GLA_TBLOCK // GLA_CHUNK
    streams = ((gf_ref, qf_ref, kf_ref, vf_ref, of_ref, False),
               (gb_ref, qb_ref, kb_ref, vb_ref, ob_ref, True))
    tri, rows, q, k, v, b, b_mid, b_edge = [], [], [], [], [], [], [], []
    for g_ref, q_ref, k_ref, v_ref, _, upper in streams:
        t = _tri(c, upper)
        edge = 0 if upper else c - 1
        bw = _running_sums(t, jnp.concatenate([g_ref[i * c:(i + 1) * c, :] for i in range(n)], axis=1))
        for i in range(n):
            r = slice(i * c, (i + 1) * c)
            x = bw[:, i * GLA_DK:(i + 1) * GLA_DK]
            tri.append(t), rows.append(r), b.append(x)
            b_mid.append(x[c // 2:c // 2 + 1]), b_edge.append(x[edge:edge + 1])
            q.append(q_ref[r, :]), k.append(k_ref[r, :]), v.append(v_ref[r, :])
    st = [stf[...], stb[...]]
    for w0 in range(0, n, GLA_WAVE):
        u = [(d, d * n + (n - 1 - step if streams[d][5] else step))
             for step in range(w0, w0 + GLA_WAVE) for d in range(len(streams))]
        qt = {j: q[j].astype(F32) * jnp.exp2(b[j] - b_mid[j]) for _, j in u}
        kt = {j: k[j].astype(F32) * jnp.exp2(b_mid[j] - b[j]) for _, j in u}
        s = {j: _dot_nt(qt[j].astype(BF16), kt[j].astype(BF16)) for _, j in u}
        a = {j: jnp.where(tri[j], s[j], 0.0).astype(BF16) for _, j in u}
        k_out = {j: (kt[j] * jnp.exp2(b_edge[j] - b_mid[j])).astype(BF16) for _, j in u}
        upd = {j: _dot_tn(v[j], k_out[j]) for _, j in u}
        intra = {j: _dot(a[j], v[j]) for _, j in u}
        q_in = {j: (qt[j] * jnp.exp2(b_mid[j])).astype(BF16) for _, j in u}
        for d, j in u:
            streams[d][4][rows[j], :] = (intra[j] + _dot_nt(q_in[j], st[d].astype(BF16))).astype(BF16)
            st[d] = st[d] * jnp.exp2(b_edge[j]) + upd[j]
    stf[...] = st[0]
    stb[...] = st[1]


def _gla(main, g, sf0, sb0, *, batch, seq):
    nt = seq // GLA_TBLOCK
    tb = GLA_TBLOCK
    qb_, kb_, vb_ = COL_GQ // GLA_DK, COL_GK // GLA_DK, COL_GV // GLA_DV

    def fwd(col):
        return lambda b, h, t: (b * nt + t, col + h)

    def bwd(col):
        return lambda b, h, t: (b * nt + nt - 1 - t, col + h)

    def seq_specs(m, g_col):
        return [pl.BlockSpec((tb, GLA_DK), m(qb_)), pl.BlockSpec((tb, GLA_DK), m(kb_)),
                pl.BlockSpec((tb, GLA_DV), m(vb_)), pl.BlockSpec((tb, GLA_DK), m(g_col))]

    st_spec = pl.BlockSpec((None, None, GLA_DV, GLA_DK), lambda b, h, t: (b, h, 0, 0))
    out = jax.ShapeDtypeStruct((batch * seq, GLA_HEADS * GLA_DV), BF16)
    return pl.pallas_call(
        _gla_kernel,
        grid=(batch, GLA_HEADS, nt),
        in_specs=seq_specs(fwd, 0) + seq_specs(bwd, GLA_HEADS) + [st_spec, st_spec],
        out_specs=[pl.BlockSpec((tb, GLA_DV), fwd(0)), pl.BlockSpec((tb, GLA_DV), bwd(0))],
        out_shape=[out, out],
        scratch_shapes=[pltpu.VMEM((GLA_DV, GLA_DK), F32), pltpu.VMEM((GLA_DV, GLA_DK), F32)],
        compiler_params=pltpu.CompilerParams(
            dimension_semantics=("arbitrary", "arbitrary", "arbitrary"), vmem_limit_bytes=VMEM_LIMIT),
        name="gla",
    )(main, main, main, g, main, main, main, g, sf0, sb0)


ATT_TQ = 512
ATT_TK = 512


ATT_UNSHIFTED_MAX_LOG2 = 56.0
ATT_ONES_ROWS = 16


def _attn_kernel(bound_ref, q_ref, k_ref, v_ref, kc_ref, vc_ref, o_ref, vt_sc, vct_sc, m_sc, acc_sc, *, seq):
    tq, tk, hd = ATT_TQ, ATT_TK, HEAD_DIM
    nk = seq // tk

    def transposed(v):
        return v.astype(F32).T.astype(BF16)

    @pl.when(pl.program_id(2) == 0)
    def _():
        for j in range(nk):
            vt_sc[j, 0:hd, :] = transposed(v_ref[j * tk:(j + 1) * tk, :])
            vt_sc[j, hd:, :] = jnp.ones((ATT_ONES_ROWS, tk), BF16)
        vct_sc[0:hd, :] = transposed(vc_ref[...])
        vct_sc[hd:, :] = jnp.ones((ATT_ONES_ROWS, vct_sc.shape[1]), BF16)

    q4 = jnp.concatenate([q_ref[:, g * hd:(g + 1) * hd] for g in range(Q_PER_KV)], axis=0)
    acc_sc[...] = jnp.zeros(acc_sc.shape, F32)

    def over_keys(step, unroll):
        def body(j, carry):
            step(k_ref[pl.ds(pl.multiple_of(j * tk, tk), tk), :], vt_sc[j])
            return carry

        lax.fori_loop(0, nk, body, 0, unroll=unroll)
        step(kc_ref[...], vct_sc[...])

    def unshifted_step(kc, vt):
        p = jnp.exp2(_dot_nt(kc, q4)).astype(BF16)
        acc_sc[...] += _dot(vt, p)

    def online_step(kc, vt):
        s = _dot_nt(kc, q4)
        m_prev = m_sc[...]
        m_new = jnp.maximum(m_prev, jnp.max(s, axis=0, keepdims=True))
        p = jnp.exp2(s - m_new).astype(BF16)
        acc_sc[...] = jnp.exp2(m_prev - m_new) * acc_sc[...] + _dot(vt, p)
        m_sc[...] = m_new

    small = bound_ref[0, 0] <= ATT_UNSHIFTED_MAX_LOG2

    @pl.when(small)
    def _():
        over_keys(unshifted_step, True)

    @pl.when(jnp.logical_not(small))
    def _():
        m_sc[...] = jnp.full(m_sc.shape, -jnp.inf, F32)
        over_keys(online_step, 1)

    acc = acc_sc[...]
    o = (acc[:hd] / acc[hd:hd + 1]).T
    for g in range(Q_PER_KV):
        o_ref[:, g * hd:(g + 1) * hd] = o[g * tq:(g + 1) * tq].astype(BF16)


def _attn(score_bound, main, ctx_main, *, batch, seq, n_ctx):
    nq = seq // ATT_TQ
    gw = Q_PER_KV * HEAD_DIM
    rows = Q_PER_KV * ATT_TQ
    vt_rows = HEAD_DIM + ATT_ONES_ROWS
    return pl.pallas_call(
        functools.partial(_attn_kernel, seq=seq),
        grid=(batch, N_KV_HEADS, nq),
        in_specs=[pl.BlockSpec(memory_space=pltpu.SMEM),
                  pl.BlockSpec((ATT_TQ, gw), lambda b, h, i: (b * nq + i, COL_AQ // gw + h)),
                  pl.BlockSpec((seq, HEAD_DIM), lambda b, h, i: (b, COL_AK // HEAD_DIM + h)),
                  pl.BlockSpec((seq, HEAD_DIM), lambda b, h, i: (b, COL_AV // HEAD_DIM + h)),
                  pl.BlockSpec((n_ctx, HEAD_DIM), lambda b, h, i: (b, COL_AK // HEAD_DIM + h)),
                  pl.BlockSpec((n_ctx, HEAD_DIM), lambda b, h, i: (b, COL_AV // HEAD_DIM + h))],
        out_specs=pl.BlockSpec((ATT_TQ, gw), lambda b, h, i: (b * nq + i, h)),
        out_shape=jax.ShapeDtypeStruct((batch * seq, N_Q_HEADS * HEAD_DIM), BF16),
        scratch_shapes=[pltpu.VMEM((seq // ATT_TK, vt_rows, ATT_TK), BF16),
                        pltpu.VMEM((vt_rows, n_ctx), BF16),
                        pltpu.VMEM((1, rows), F32), pltpu.VMEM((vt_rows, rows), F32)],
        compiler_params=pltpu.CompilerParams(
            dimension_semantics=("arbitrary", "arbitrary", "arbitrary"), vmem_limit_bytes=VMEM_LIMIT),
        name="attn",
    )(score_bound, main, main, main, ctx_main, ctx_main)


FF_CHUNK = 1024


def _tail_kernel(x_ref, attn_ref, of_ref, ob_ref, go_ref, ga_ref, gg_ref, gn_ref, gt1_ref,
                 wa_ref, wg_ref, wo_ref, sh_ref, sc_ref, gt2_ref, n2_ref, w1_ref, w2_ref, o_ref, *, d_ff):
    gn = gn_ref[...]
    heads = []
    for h in range(GLA_HEADS):
        s = slice(h * GLA_DV, (h + 1) * GLA_DV)
        o = of_ref[:, s].astype(F32) + ob_ref[:, s].astype(F32)
        heads.append((_head_norm(o, gn) * go_ref[:, s].astype(F32)).astype(BF16))
    gla = jnp.concatenate(heads, axis=-1)
    ya = ga_ref[...].astype(F32) * _dot(attn_ref[...], wa_ref[...])
    yg = gg_ref[...].astype(F32) * _dot(gla, wg_ref[...])
    x1 = x_ref[...] + gt1_ref[...] * _dot((ya + yg).astype(BF16), wo_ref[...])
    ms = jnp.mean(x1 * x1, axis=-1, keepdims=True)
    h2 = x1 * lax.rsqrt(ms + EPS) * n2_ref[...]
    hb = (h2 * (1.0 + sc_ref[...]) + sh_ref[...]).astype(BF16)
    acc = jnp.zeros(x1.shape, F32)
    for c in range(d_ff // FF_CHUNK):
        s = slice(c * FF_CHUNK, (c + 1) * FF_CHUNK)
        u = jnp.maximum(_dot(hb, w1_ref[:, s]), 0.0)
        acc = acc + _dot((u * u).astype(BF16), w2_ref[s, :])
    o_ref[...] = x1 + gt2_ref[...] * acc


def _tail(x2d, attn, o_f, o_b, main, gla_norm, mod4, wa, wg, wo, norm2, w1, w2, *, tm, seq_tiles):
    rows, d = x2d.shape
    d_ff = w1.shape[1]
    tile = pl.BlockSpec((tm, d), lambda i: (i, 0))
    mod = lambda g: pl.BlockSpec((None, None, 1, d), lambda i: (i // seq_tiles, g, 0, 0))
    const = lambda shape: _resident(shape, lambda i: (0, 0))
    return pl.pallas_call(
        functools.partial(_tail_kernel, d_ff=d_ff),
        grid=(rows // tm,),
        in_specs=[tile, tile, tile, tile,
                  pl.BlockSpec((tm, d), lambda i: (i, COL_GO // d)),
                  pl.BlockSpec((tm, d), lambda i: (i, COL_GA // d)),
                  pl.BlockSpec((tm, d), lambda i: (i, COL_GG // d)),
                  const((1, GLA_DV)), mod(2), const((d, d)), const((d, d)), const((d, d)),
                  mod(3), mod(4), mod(5), const((1, d)), const((d, d_ff)), const((d_ff, d))],
        out_specs=tile,
        out_shape=jax.ShapeDtypeStruct((rows, d), F32),
        compiler_params=pltpu.CompilerParams(vmem_limit_bytes=VMEM_LIMIT),
        name="merge_out_mlp",
    )(x2d, attn, o_f, o_b, main, main, main, gla_norm, mod4, wa, wg, wo, mod4, mod4, mod4, norm2, w1, w2)


def _rope_tables(seq):
    t = np.arange(seq)
    half = HEAD_DIM // 2
    freqs = ROPE_THETA ** (-np.arange(0, half, 2, dtype=np.float32) / half)
    ang_r = (t // GRID_W).astype(np.float32)[:, None] * freqs
    ang_c = (t % GRID_W).astype(np.float32)[:, None] * freqs
    cos = np.concatenate([np.cos(ang_r)] * 2 + [np.cos(ang_c)] * 2, axis=-1)
    sin = np.concatenate([-np.sin(ang_r), np.sin(ang_r), -np.sin(ang_c), np.sin(ang_c)], axis=-1)
    return jnp.asarray(cos, F32), jnp.asarray(sin, F32)


def _pad_rows(w, row0):
    return jnp.pad(w, ((row0, LR_PAD - row0 - w.shape[0]), (0, 0)))


def kernel(x, c, ctx, c_ctx, w_ada, b_ada, norm1, w_in, q_norm, k_norm, w_gk_fwd, b_gk_fwd,
           w_gk_bwd, b_gk_bwd, gla_norm, w_br_attn, w_br_gla, w_out, norm2, w_mlp1, w_mlp2):
    batch, seq, d = x.shape
    n_ctx = ctx.shape[1]
    depth = w_ada.shape[0]
    assert depth == 1 and seq % GLA_TBLOCK == 0 and seq % ATT_TK == 0 and batch <= 7
    tm = 512
    seq_tiles = seq // tm
    ctx_tm = n_ctx
    l = 0

    cc = jnp.zeros((8, d), F32).at[:batch].set(c).at[batch].set(c_ctx)
    mod4 = _ada(cc, w_ada[l], b_ada[l][None, :]).reshape(8, 6, 1, d)

    w_main, w_lr = _regroup_w_in(jnp.transpose(w_in[l]))
    cos_t, sin_t = _rope_tables(seq)
    q_scale = (HEAD_DIM ** -0.5) * float(np.log2(np.e))
    q_gain = (q_norm[l] * q_scale)[None, :]
    k_gain = k_norm[l][None, :]
    score_bound = (1.01 * HEAD_DIM * q_scale * jnp.max(jnp.abs(q_norm[l])) * jnp.max(jnp.abs(k_norm[l])))
    score_bound = score_bound.reshape(1, 1).astype(F32)
    n1 = norm1[l][None, :]

    w_gk = jnp.concatenate([_pad_rows(w_gk_fwd[l], 0), _pad_rows(w_gk_bwd[l], GLA_GATE_RANK)], axis=1).astype(BF16)
    b_gk = jnp.concatenate([b_gk_fwd[l], b_gk_bwd[l]])[None, :]

    main, g = _inproj(x.reshape(batch * seq, d), mod4, lambda i: i // seq_tiles, n1, w_main, w_lr, w_gk, b_gk,
                      cos_t, sin_t, q_gain, k_gain, n_cols=N_MAIN, rope=True, tm=tm, seq_tiles=seq_tiles)
    ctx_main, ctx_g = _inproj(ctx.reshape(batch * n_ctx, d), mod4, lambda i: batch, n1, w_main, w_lr, w_gk, b_gk,
                              cos_t, sin_t, q_gain, k_gain, n_cols=N_CTX_MAIN, rope=False, tm=ctx_tm,
                              seq_tiles=1)

    sf0, sb0 = _glastate(ctx_main, ctx_g, batch=batch, n_ctx=n_ctx)
    o_f, o_b = _gla(main, g, sf0, sb0, batch=batch, seq=seq)

    attn = _attn(score_bound, main, ctx_main, batch=batch, seq=seq, n_ctx=n_ctx)

    x2 = _tail(x.reshape(batch * seq, d), attn, o_f, o_b, main, gla_norm[l][None, :], mod4,
               w_br_attn[l].astype(BF16), w_br_gla[l].astype(BF16), w_out[l].astype(BF16),
               norm2[l][None, :], w_mlp1[l].astype(BF16), w_mlp2[l].astype(BF16),
               tm=tm, seq_tiles=seq_tiles)
    return x2.reshape(batch, seq, d)
```

```python
import functools

import numpy as np
import jax
import jax.numpy as jnp
from jax import lax
from jax.experimental import pallas as pl
from jax.experimental.pallas import tpu as pltpu

F32 = jnp.float32
BF16 = jnp.bfloat16

GRID_W = 64
HEAD_DIM = 128
N_Q_HEADS = 8
N_KV_HEADS = 2
Q_PER_KV = N_Q_HEADS // N_KV_HEADS
ROPE_THETA = 10000.0
GLA_HEADS = 4
GLA_DK = 128
GLA_DV = 256
GLA_GATE_RANK = 16
GLA_GATE_NORM = 16.0
EPS = 1e-6
LOG2_E = float(np.log2(np.e))

COL_AK, COL_AV, COL_GK, COL_GV = 0, 256, 512, 1024
COL_AQ, COL_GO, COL_GA, COL_GG, COL_GQ = 2048, 3072, 4096, 5120, 6144
N_MAIN = 6656
N_CTX_MAIN = 2048
LR_PAD = 128
PROJ_CHUNK = 512
PROJ_ROW_SPLIT = 2

VMEM_LIMIT = 56 * 1024 * 1024


def _dot(a, b):
    return jnp.dot(a, b, preferred_element_type=F32)


def _dot_nt(a, b):
    return lax.dot_general(a, b, (((1,), (1,)), ((), ())), preferred_element_type=F32)


def _dot_tn(a, b):
    return lax.dot_general(a, b, (((0,), (0,)), ((), ())), preferred_element_type=F32)


def _sigmoid(x):
    return 0.5 * jnp.tanh(0.5 * x) + 0.5


def _resident(shape, index_map):
    return pl.BlockSpec(shape, index_map, pipeline_mode=pl.Buffered(1))


def _ada_kernel(c_ref, w_ref, b_ref, o_ref):
    c = c_ref[...]
    s = (c * _sigmoid(c)).astype(BF16)
    o_ref[...] = _dot(s, w_ref[...].astype(BF16)) + b_ref[...]


def _ada(cc, w_ada, b_ada):
    d, n = w_ada.shape
    tn = 1536
    return pl.pallas_call(
        _ada_kernel,
        grid=(n // tn,),
        in_specs=[pl.BlockSpec((8, d), lambda j: (0, 0)),
                  pl.BlockSpec((d, tn), lambda j: (0, j)),
                  pl.BlockSpec((1, tn), lambda j: (0, j))],
        out_specs=pl.BlockSpec((8, tn), lambda j: (0, j)),
        out_shape=jax.ShapeDtypeStruct((8, n), F32),
        compiler_params=pltpu.CompilerParams(vmem_limit_bytes=VMEM_LIMIT),
        name="ada",
    )(cc, w_ada, b_ada)


W_IN_LR = (2048, 2080)
W_IN_GROUPS = (((0, 2048), 1.0), ((2080, 3104), 1.0), ((3616, 4640), 1.0), ((4640, 5664), 1.0),
               ((5664, 6688), 1.0), ((3104, 3616), GLA_DK ** -0.5))
W_IN_CHUNKS = tuple((lo + o, scale) for (lo, hi), scale in W_IN_GROUPS for o in range(0, hi - lo, PROJ_CHUNK))
assert len(W_IN_CHUNKS) * PROJ_CHUNK == N_MAIN


def _chunk_table(j, column, divisor=1):
    entry = lambda idx: W_IN_CHUNKS[idx][column] // divisor if divisor != 1 else W_IN_CHUNKS[idx][column]
    out = entry(len(W_IN_CHUNKS) - 1)
    for idx in range(len(W_IN_CHUNKS) - 2, -1, -1):
        out = jnp.where(j == idx, entry(idx), out)
    return out


def _regroup_kernel(src_ref, lr_src_ref, main_ref, lr_ref):
    scale = _chunk_table(pl.program_id(0), 1).astype(F32)
    main_ref[...] = (src_ref[...] * scale).T.astype(BF16)
    rank2 = W_IN_LR[1] - W_IN_LR[0]
    lr_rows = jnp.concatenate([lr_src_ref[...], jnp.zeros((LR_PAD - rank2, lr_src_ref.shape[1]), F32)], axis=0)
    lr_ref[...] = lr_rows.T.astype(BF16)


def _regroup_w_in(w_t):
    n_in, d = w_t.shape
    rank2 = W_IN_LR[1] - W_IN_LR[0]
    return pl.pallas_call(
        _regroup_kernel,
        grid=(len(W_IN_CHUNKS),),
        in_specs=[pl.BlockSpec((pl.Element(PROJ_CHUNK), pl.Element(d)),
                               lambda j: (_chunk_table(j, 0, rank2) * rank2, 0)),
                  pl.BlockSpec((rank2, d), lambda j: (W_IN_LR[0] // rank2, 0))],
        out_specs=[pl.BlockSpec((d, PROJ_CHUNK), lambda j: (0, j)), pl.BlockSpec((d, LR_PAD), lambda j: (0, 0))],
        out_shape=[jax.ShapeDtypeStruct((d, N_MAIN), BF16), jax.ShapeDtypeStruct((d, LR_PAD), BF16)],
        compiler_params=pltpu.CompilerParams(vmem_limit_bytes=VMEM_LIMIT),
        name="regroup_w_in",
    )(w_t, w_t)


def _head_norm(a, gain):
    ms = jnp.mean(a * a, axis=-1, keepdims=True)
    return a * lax.rsqrt(ms + EPS) * gain


def _rope(n, cos, sin_signed):
    lane = lax.broadcasted_iota(jnp.int32, n.shape, 1)
    partner = jnp.where((lane % 64) < 32, pltpu.roll(n, 96, 1), pltpu.roll(n, 32, 1))
    return n * cos + partner * sin_signed


def _log2_decay(lowrank, w, b):
    z = _dot(lowrank.astype(BF16), w) + b
    return (jnp.minimum(z, 0.0) - jnp.log(1.0 + jnp.exp(-jnp.abs(z)))) * (LOG2_E / GLA_GATE_NORM)


def _inproj_kernel(x_ref, sh_ref, sc_ref, n1_ref, w_ref, wlr_ref, wgk_ref, bgk_ref, cos_ref, sin_ref,
                   qg_ref, kg_ref, main_ref, g_ref, *, n_cols, rope):
    half = x_ref.shape[0] // PROJ_ROW_SPLIT
    g_slices = min(8, n_cols // PROJ_CHUNK)
    g_width = g_ref.shape[1] // g_slices
    for part in range(PROJ_ROW_SPLIT):
        r = slice(part * half, (part + 1) * half)
        x = x_ref[r, :]
        ms = jnp.mean(x * x, axis=-1, keepdims=True)
        h = x * lax.rsqrt(ms + EPS) * n1_ref[...]
        hb = (h * (1.0 + sc_ref[...]) + sh_ref[...]).astype(BF16)

        def qk_head(a, gain):
            n = _head_norm(a, gain)
            if rope:
                n = _rope(n, cos_ref[r, :], sin_ref[r, :])
            return n.astype(BF16)

        for c in range(n_cols // PROJ_CHUNK):
            lo = c * PROJ_CHUNK
            acc = _dot(hb, w_ref[:, lo:lo + PROJ_CHUNK])
            for s in range(PROJ_CHUNK // HEAD_DIM):
                col = lo + s * HEAD_DIM
                a = acc[:, s * HEAD_DIM:(s + 1) * HEAD_DIM]
                if COL_AK <= col < COL_AV:
                    main_ref[r, col:col + HEAD_DIM] = qk_head(a, kg_ref[...])
                elif COL_AQ <= col < COL_GO:
                    main_ref[r, col:col + HEAD_DIM] = qk_head(a, qg_ref[...])
                elif COL_GO <= col < COL_GA:
                    main_ref[r, col:col + HEAD_DIM] = (a * _sigmoid(a)).astype(BF16)
                elif COL_GA <= col < COL_GQ:
                    main_ref[r, col:col + HEAD_DIM] = _sigmoid(a).astype(BF16)
                else:
                    main_ref[r, col:col + HEAD_DIM] = a.astype(BF16)
            if c < g_slices:
                if c == 0:
                    lowrank = _dot(hb, wlr_ref[...])
                gs = slice(c * g_width, (c + 1) * g_width)
                g_ref[r, gs] = _log2_decay(lowrank, wgk_ref[:, gs], bgk_ref[:, gs]).astype(BF16)


def _inproj(x2d, mod4, mod_row_of_tile, norm1, w_main, w_lr, w_gk, b_gk, cos_t, sin_t, q_gain, k_gain,
            *, n_cols, rope, tm, seq_tiles):
    rows, d = x2d.shape
    n_g = w_gk.shape[1]
    kern = functools.partial(_inproj_kernel, n_cols=n_cols, rope=rope)
    return pl.pallas_call(
        kern,
        grid=(rows // tm,),
        in_specs=[
            pl.BlockSpec((tm, d), lambda i: (i, 0)),
            pl.BlockSpec((None, None, 1, d), lambda i: (mod_row_of_tile(i), 0, 0, 0)),
            pl.BlockSpec((None, None, 1, d), lambda i: (mod_row_of_tile(i), 1, 0, 0)),
            _resident((1, d), lambda i: (0, 0)),
            _resident((d, n_cols), lambda i: (0, 0)),
            _resident((d, LR_PAD), lambda i: (0, 0)),
            _resident((LR_PAD, n_g), lambda i: (0, 0)),
            _resident((1, n_g), lambda i: (0, 0)),
            pl.BlockSpec((tm, HEAD_DIM), lambda i: (i % seq_tiles, 0)),
            pl.BlockSpec((tm, HEAD_DIM), lambda i: (i % seq_tiles, 0)),
            _resident((1, HEAD_DIM), lambda i: (0, 0)),
            _resident((1, HEAD_DIM), lambda i: (0, 0)),
        ],
        out_specs=[pl.BlockSpec((tm, n_cols), lambda i: (i, 0)),
                   pl.BlockSpec((tm, n_g), lambda i: (i, 0))],
        out_shape=[jax.ShapeDtypeStruct((rows, n_cols), BF16),
                   jax.ShapeDtypeStruct((rows, n_g), BF16)],
        compiler_params=pltpu.CompilerParams(vmem_limit_bytes=VMEM_LIMIT),
        name="inproj_rope" if rope else "inproj_ctx",
    )(x2d, mod4, mod4, norm1, w_main, w_lr, w_gk, b_gk, cos_t, sin_t, q_gain, k_gain)


def _tri(n, upper):
    r = lax.broadcasted_iota(jnp.int32, (n, n), 0)
    c = lax.broadcasted_iota(jnp.int32, (n, n), 1)
    return (c >= r) if upper else (r >= c)


def _running_sums(tri, g):
    return _dot(tri.astype(BF16), g)


def _glastate_kernel(k_ref, v_ref, gf_ref, gb_ref, sf_ref, sb_ref, *, n):
    k = k_ref[...].astype(F32)
    b = _running_sums(_tri(n, False), gf_ref[...])
    k_fwd = (k * jnp.exp2(b[n - 1:n] - b)).astype(BF16)
    e = _running_sums(_tri(n, True), gb_ref[...])
    k_bwd = (k * jnp.exp2(e[0:1] - e)).astype(BF16)
    for h in range(GLA_HEADS):
        v = v_ref[:, h * GLA_DV:(h + 1) * GLA_DV]
        kc = slice(h * GLA_DK, (h + 1) * GLA_DK)
        sf_ref[h] = _dot_tn(v, k_fwd[:, kc])
        sb_ref[h] = _dot_tn(v, k_bwd[:, kc])


def _glastate(ctx_main, ctx_g, *, batch, n_ctx):
    kw, vw = GLA_HEADS * GLA_DK, GLA_HEADS * GLA_DV
    st = jax.ShapeDtypeStruct((batch, GLA_HEADS, GLA_DV, GLA_DK), F32)
    st_spec = pl.BlockSpec((None, GLA_HEADS, GLA_DV, GLA_DK), lambda b: (b, 0, 0, 0))
    return pl.pallas_call(
        functools.partial(_glastate_kernel, n=n_ctx),
        grid=(batch,),
        in_specs=[pl.BlockSpec((n_ctx, kw), lambda b: (b, COL_GK // kw)),
                  pl.BlockSpec((n_ctx, vw), lambda b: (b, COL_GV // vw)),
                  pl.BlockSpec((n_ctx, kw), lambda b: (b, 0)),
                  pl.BlockSpec((n_ctx, kw), lambda b: (b, 1))],
        out_specs=[st_spec, st_spec],
        out_shape=[st, st],
        compiler_params=pltpu.CompilerParams(vmem_limit_bytes=VMEM_LIMIT),
        name="gla_ctx_state",
    )(ctx_main, ctx_main, ctx_g, ctx_g)


GLA_CHUNK = 64
GLA_TBLOCK = 4096
GLA_WAVE = 4


def _gla_kernel(qf_ref, kf_ref, vf_ref, gf_ref, qb_ref, kb_ref, vb_ref, gb_ref,
                sf0_ref, sb0_ref, of_ref, ob_ref, stf, stb):
    @pl.when(pl.program_id(2) == 0)
    def _():
        stf[...] = sf0_ref[...]
        stb[...] = sb0_ref[...]

    c, n = GLA_CHUNK, qf_ref.shape[0] // GLA_CHUNK
    streams = ((gf_ref, qf_ref, kf_ref, vf_ref, of_ref, False),
               (gb_ref, qb_ref, kb_ref, vb_ref, ob_ref, True))
    tri, rows, q, k, v, b, b_mid, b_edge = [], [], [], [], [], [], [], []
    for g_ref, q_ref, k_ref, v_ref, _, upper in streams:
        t = _tri(c, upper)
        edge = 0 if upper else c - 1
        bw = _running_sums(t, jnp.concatenate([g_ref[i * c:(i + 1) * c, :] for i in range(n)], axis=1))
        for i in range(n):
            r = slice(i * c, (i + 1) * c)
            x = bw[:, i * GLA_DK:(i + 1) * GLA_DK]
            tri.append(t), rows.append(r), b.append(x)
            b_mid.append(x[c // 2:c // 2 + 1]), b_edge.append(x[edge:edge + 1])
            q.append(q_ref[r, :]), k.append(k_ref[r, :]), v.append(v_ref[r, :])
    st = [stf[...], stb[...]]
    for w0 in range(0, n, GLA_WAVE):
        u = [(d, d * n + (n - 1 - step if streams[d][5] else step))
             for step in range(w0, w0 + GLA_WAVE) for d in range(len(streams))]
        qt = {j: q[j].astype(F32) * jnp.exp2(b[j] - b_mid[j]) for _, j in u}
        kt = {j: k[j].astype(F32) * jnp.exp2(b_mid[j] - b[j]) for _, j in u}
        s = {j: _dot_nt(qt[j].astype(BF16), kt[j].astype(BF16)) for _, j in u}
        a = {j: jnp.where(tri[j], s[j], 0.0).astype(BF16) for _, j in u}
        k_out = {j: (kt[j] * jnp.exp2(b_edge[j] - b_mid[j])).astype(BF16) for _, j in u}
        upd = {j: _dot_tn(v[j], k_out[j]) for _, j in u}
        intra = {j: _dot(a[j], v[j]) for _, j in u}
        q_in = {j: (qt[j] * jnp.exp2(b_mid[j])).astype(BF16) for _, j in u}
        for d, j in u:
            streams[d][4][rows[j], :] = (intra[j] + _dot_nt(q_in[j], st[d].astype(BF16))).astype(BF16)
            st[d] = st[d] * jnp.exp2(b_edge[j]) + upd[j]
    stf[...] = st[0]
    stb[...] = st[1]


def _gla(main, g, sf0, sb0, *, batch, seq):
    tb = min(GLA_TBLOCK, seq)
    nt = seq // tb
    qb_, kb_, vb_ = COL_GQ // GLA_DK, COL_GK // GLA_DK, COL_GV // GLA_DV

    def fwd(col):
        return lambda b, h, t: (b * nt + t, col + h)

    def bwd(col):
        return lambda b, h, t: (b * nt + nt - 1 - t, col + h)

    def seq_specs(m, g_col):
        return [pl.BlockSpec((tb, GLA_DK), m(qb_)), pl.BlockSpec((tb, GLA_DK), m(kb_)),
                pl.BlockSpec((tb, GLA_DV), m(vb_)), pl.BlockSpec((tb, GLA_DK), m(g_col))]

    st_spec = pl.BlockSpec((None, None, GLA_DV, GLA_DK), lambda b, h, t: (b, h, 0, 0))
    out = jax.ShapeDtypeStruct((batch * seq, GLA_HEADS * GLA_DV), BF16)
    return pl.pallas_call(
        _gla_kernel,
        grid=(batch, GLA_HEADS, nt),
        in_specs=seq_specs(fwd, 0) + seq_specs(bwd, GLA_HEADS) + [st_spec, st_spec],
        out_specs=[pl.BlockSpec((tb, GLA_DV), fwd(0)), pl.BlockSpec((tb, GLA_DV), bwd(0))],
        out_shape=[out, out],
        scratch_shapes=[pltpu.VMEM((GLA_DV, GLA_DK), F32), pltpu.VMEM((GLA_DV, GLA_DK), F32)],
        compiler_params=pltpu.CompilerParams(
            dimension_semantics=("arbitrary", "arbitrary", "arbitrary"), vmem_limit_bytes=VMEM_LIMIT),
        name="gla",
    )(main, main, main, g, main, main, main, g, sf0, sb0)


ATT_TQ = 1024
ATT_TK = 512


ATT_UNSHIFTED_MAX_LOG2 = 56.0
ATT_ONES_ROWS = 16


def _attn_kernel(bound_ref, q_ref, k_ref, v_ref, kc_ref, vc_ref, o_ref, vt_sc, vct_sc, m_sc, acc_sc, *, seq):
    tq, tk, hd = ATT_TQ, ATT_TK, HEAD_DIM
    nk = seq // tk

    def transposed(v):
        return v.astype(F32).T.astype(BF16)

    @pl.when(pl.program_id(2) == 0)
    def _():
        for j in range(nk):
            vt_sc[j, 0:hd, :] = transposed(v_ref[j * tk:(j + 1) * tk, :])
            vt_sc[j, hd:, :] = jnp.ones((ATT_ONES_ROWS, tk), BF16)
        vct_sc[0:hd, :] = transposed(vc_ref[...])
        vct_sc[hd:, :] = jnp.ones((ATT_ONES_ROWS, vct_sc.shape[1]), BF16)

    q4 = jnp.concatenate([q_ref[:, g * hd:(g + 1) * hd] for g in range(Q_PER_KV)], axis=0)
    acc_sc[...] = jnp.zeros(acc_sc.shape, F32)

    def over_keys(step, unroll):
        def body(j, carry):
            step(k_ref[pl.ds(pl.multiple_of(j * tk, tk), tk), :], vt_sc[j])
            return carry

        lax.fori_loop(0, nk, body, 0, unroll=unroll)
        step(kc_ref[...], vct_sc[...])

    def unshifted_step(kc, vt):
        p = jnp.exp2(_dot_nt(kc, q4)).astype(BF16)
        acc_sc[...] += _dot(vt, p)

    def online_step(kc, vt):
        s = _dot_nt(kc, q4)
        m_prev = m_sc[...]
        m_new = jnp.maximum(m_prev, jnp.max(s, axis=0, keepdims=True))
        p = jnp.exp2(s - m_new).astype(BF16)
        acc_sc[...] = jnp.exp2(m_prev - m_new) * acc_sc[...] + _dot(vt, p)
        m_sc[...] = m_new

    small = bound_ref[0, 0] <= ATT_UNSHIFTED_MAX_LOG2

    @pl.when(small)
    def _():
        over_keys(unshifted_step, True)

    @pl.when(jnp.logical_not(small))
    def _():
        m_sc[...] = jnp.full(m_sc.shape, -jnp.inf, F32)
        over_keys(online_step, 1)

    acc = acc_sc[...]
    o = (acc[:hd] / acc[hd:hd + 1]).T
    for g in range(Q_PER_KV):
        o_ref[:, g * hd:(g + 1) * hd] = o[g * tq:(g + 1) * tq].astype(BF16)


def _attn(score_bound, main, ctx_main, *, batch, seq, n_ctx):
    nq = seq // ATT_TQ
    gw = Q_PER_KV * HEAD_DIM
    rows = Q_PER_KV * ATT_TQ
    vt_rows = HEAD_DIM + ATT_ONES_ROWS
    return pl.pallas_call(
        functools.partial(_attn_kernel, seq=seq),
        grid=(batch, N_KV_HEADS, nq),
        in_specs=[pl.BlockSpec(memory_space=pltpu.SMEM),
                  pl.BlockSpec((ATT_TQ, gw), lambda b, h, i: (b * nq + i, COL_AQ // gw + h)),
                  pl.BlockSpec((seq, HEAD_DIM), lambda b, h, i: (b, COL_AK // HEAD_DIM + h)),
                  pl.BlockSpec((seq, HEAD_DIM), lambda b, h, i: (b, COL_AV // HEAD_DIM + h)),
                  pl.BlockSpec((n_ctx, HEAD_DIM), lambda b, h, i: (b, COL_AK // HEAD_DIM + h)),
                  pl.BlockSpec((n_ctx, HEAD_DIM), lambda b, h, i: (b, COL_AV // HEAD_DIM + h))],
        out_specs=pl.BlockSpec((ATT_TQ, gw), lambda b, h, i: (b * nq + i, h)),
        out_shape=jax.ShapeDtypeStruct((batch * seq, N_Q_HEADS * HEAD_DIM), BF16),
        scratch_shapes=[pltpu.VMEM((seq // ATT_TK, vt_rows, ATT_TK), BF16),
                        pltpu.VMEM((vt_rows, n_ctx), BF16),
                        pltpu.VMEM((1, rows), F32), pltpu.VMEM((vt_rows, rows), F32)],
        compiler_params=pltpu.CompilerParams(
            dimension_semantics=("arbitrary", "arbitrary", "arbitrary"), vmem_limit_bytes=VMEM_LIMIT),
        name="attn",
    )(score_bound, main, main, main, ctx_main, ctx_main)


FF_CHUNK = 1024


def _tail_kernel(x_ref, attn_ref, of_ref, ob_ref, go_ref, ga_ref, gg_ref, gn_ref, gt1_ref,
                 wa_ref, wg_ref, wo_ref, sh_ref, sc_ref, gt2_ref, n2_ref, w1_ref, w2_ref, o_ref, *, d_ff):
    gn = gn_ref[...]
    heads = []
    for h in range(GLA_HEADS):
        s = slice(h * GLA_DV, (h + 1) * GLA_DV)
        o = of_ref[:, s].astype(F32) + ob_ref[:, s].astype(F32)
        heads.append((_head_norm(o, gn) * go_ref[:, s].astype(F32)).astype(BF16))
    gla = jnp.concatenate(heads, axis=-1)
    ya = ga_ref[...].astype(F32) * _dot(attn_ref[...], wa_ref[...])
    yg = gg_ref[...].astype(F32) * _dot(gla, wg_ref[...])
    x1 = x_ref[...] + gt1_ref[...] * _dot((ya + yg).astype(BF16), wo_ref[...])
    ms = jnp.mean(x1 * x1, axis=-1, keepdims=True)
    h2 = x1 * lax.rsqrt(ms + EPS) * n2_ref[...]
    hb = (h2 * (1.0 + sc_ref[...]) + sh_ref[...]).astype(BF16)
    acc = jnp.zeros(x1.shape, F32)
    for c in range(d_ff // FF_CHUNK):
        s = slice(c * FF_CHUNK, (c + 1) * FF_CHUNK)
        u = jnp.maximum(_dot(hb, w1_ref[:, s]), 0.0)
        acc = acc + _dot((u * u).astype(BF16), w2_ref[s, :])
    o_ref[...] = x1 + gt2_ref[...] * acc


def _tail(x2d, attn, o_f, o_b, main, gla_norm, mod4, wa, wg, wo, norm2, w1, w2, *, tm, seq_tiles):
    rows, d = x2d.shape
    d_ff = w1.shape[1]
    tile = pl.BlockSpec((tm, d), lambda i: (i, 0))
    mod = lambda g: pl.BlockSpec((None, None, 1, d), lambda i: (i // seq_tiles, g, 0, 0))
    const = lambda shape: _resident(shape, lambda i: (0, 0))
    return pl.pallas_call(
        functools.partial(_tail_kernel, d_ff=d_ff),
        grid=(rows // tm,),
        in_specs=[tile, tile, tile, tile,
                  pl.BlockSpec((tm, d), lambda i: (i, COL_GO // d)),
                  pl.BlockSpec((tm, d), lambda i: (i, COL_GA // d)),
                  pl.BlockSpec((tm, d), lambda i: (i, COL_GG // d)),
                  const((1, GLA_DV)), mod(2), const((d, d)), const((d, d)), const((d, d)),
                  mod(3), mod(4), mod(5), const((1, d)), const((d, d_ff)), const((d_ff, d))],
        out_specs=tile,
        out_shape=jax.ShapeDtypeStruct((rows, d), F32),
        compiler_params=pltpu.CompilerParams(vmem_limit_bytes=VMEM_LIMIT),
        name="merge_out_mlp",
    )(x2d, attn, o_f, o_b, main, main, main, gla_norm, mod4, wa, wg, wo, mod4, mod4, mod4, norm2, w1, w2)


def _rope_tables(seq):
    t = np.arange(seq)
    half = HEAD_DIM // 2
    freqs = ROPE_THETA ** (-np.arange(0, half, 2, dtype=np.float32) / half)
    ang_r = (t // GRID_W).astype(np.float32)[:, None] * freqs
    ang_c = (t % GRID_W).astype(np.float32)[:, None] * freqs
    cos = np.concatenate([np.cos(ang_r)] * 2 + [np.cos(ang_c)] * 2, axis=-1)
    sin = np.concatenate([-np.sin(ang_r), np.sin(ang_r), -np.sin(ang_c), np.sin(ang_c)], axis=-1)
    return jnp.asarray(cos, F32), jnp.asarray(sin, F32)


def _pad_rows(w, row0):
    return jnp.pad(w, ((row0, LR_PAD - row0 - w.shape[0]), (0, 0)))


def kernel(x, c, ctx, c_ctx, w_ada, b_ada, norm1, w_in, q_norm, k_norm, w_gk_fwd, b_gk_fwd,
           w_gk_bwd, b_gk_bwd, gla_norm, w_br_attn, w_br_gla, w_out, norm2, w_mlp1, w_mlp2):
    batch, seq, d = x.shape
    n_ctx = ctx.shape[1]
    depth = w_ada.shape[0]
    assert depth == 1 and seq % min(GLA_TBLOCK, seq) == 0 and seq % ATT_TK == 0 and batch <= 7
    tm = 512
    seq_tiles = seq // tm
    ctx_tm = n_ctx
    l = 0

    cc = jnp.zeros((8, d), F32).at[:batch].set(c).at[batch].set(c_ctx)
    mod4 = _ada(cc, w_ada[l], b_ada[l][None, :]).reshape(8, 6, 1, d)

    w_main, w_lr = _regroup_w_in(jnp.transpose(w_in[l]))
    cos_t, sin_t = _rope_tables(seq)
    q_scale = (HEAD_DIM ** -0.5) * float(np.log2(np.e))
    q_gain = (q_norm[l] * q_scale)[None, :]
    k_gain = k_norm[l][None, :]
    score_bound = (1.01 * HEAD_DIM * q_scale * jnp.max(jnp.abs(q_norm[l])) * jnp.max(jnp.abs(k_norm[l])))
    score_bound = score_bound.reshape(1, 1).astype(F32)
    n1 = norm1[l][None, :]

    w_gk = jnp.concatenate([_pad_rows(w_gk_fwd[l], 0), _pad_rows(w_gk_bwd[l], GLA_GATE_RANK)], axis=1).astype(BF16)
    b_gk = jnp.concatenate([b_gk_fwd[l], b_gk_bwd[l]])[None, :]

    main, g = _inproj(x.reshape(batch * seq, d), mod4, lambda i: i // seq_tiles, n1, w_main, w_lr, w_gk, b_gk,
                      cos_t, sin_t, q_gain, k_gain, n_cols=N_MAIN, rope=True, tm=tm, seq_tiles=seq_tiles)
    ctx_main, ctx_g = _inproj(ctx.reshape(batch * n_ctx, d), mod4, lambda i: batch, n1, w_main, w_lr, w_gk, b_gk,
                              cos_t, sin_t, q_gain, k_gain, n_cols=N_CTX_MAIN, rope=False, tm=ctx_tm,
                              seq_tiles=1)

    sf0, sb0 = _glastate(ctx_main, ctx_g, batch=batch, n_ctx=n_ctx)
    o_f, o_b = _gla(main, g, sf0, sb0, batch=batch, seq=seq)

    attn = _attn(score_bound, main, ctx_main, batch=batch, seq=seq, n_ctx=n_ctx)

    x2 = _tail(x.reshape(batch * seq, d), attn, o_f, o_b, main, gla_norm[l][None, :], mod4,
               w_br_attn[l].astype(BF16), w_br_gla[l].astype(BF16), w_out[l].astype(BF16),
               norm2[l][None, :], w_mlp1[l].astype(BF16), w_mlp2[l].astype(BF16),
               tm=tm, seq_tiles=seq_tiles)
    return x2.reshape(batch, seq, d)
```

```python
import functools

import numpy as np
import jax
import jax.numpy as jnp
from jax import lax
from jax.experimental import pallas as pl
from jax.experimental.pallas import tpu as pltpu

F32 = jnp.float32
BF16 = jnp.bfloat16

GRID_W = 64
HEAD_DIM = 128
N_Q_HEADS = 8
N_KV_HEADS = 2
Q_PER_KV = N_Q_HEADS // N_KV_HEADS
ROPE_THETA = 10000.0
GLA_HEADS = 4
GLA_DK = 128
GLA_DV = 256
GLA_GATE_RANK = 16
GLA_GATE_NORM = 16.0
EPS = 1e-6
LOG2_E = float(np.log2(np.e))

COL_AK, COL_AV, COL_GK, COL_GV = 0, 256, 512, 1024
COL_AQ, COL_GO, COL_GA, COL_GG, COL_GQ = 2048, 3072, 4096, 5120, 6144
N_MAIN = 6656
N_CTX_MAIN = 2048
LR_PAD = 128
PROJ_CHUNK = 512
PROJ_ROW_SPLIT = 2

VMEM_LIMIT = 56 * 1024 * 1024


def _dot(a, b):
    return jnp.dot(a, b, preferred_element_type=F32)


def _dot_nt(a, b):
    return lax.dot_general(a, b, (((1,), (1,)), ((), ())), preferred_element_type=F32)


def _dot_tn(a, b):
    return lax.dot_general(a, b, (((0,), (0,)), ((), ())), preferred_element_type=F32)


def _sigmoid(x):
    return 0.5 * jnp.tanh(0.5 * x) + 0.5


def _resident(shape, index_map):
    return pl.BlockSpec(shape, index_map, pipeline_mode=pl.Buffered(1))


def _ada_kernel(c_ref, w_ref, b_ref, o_ref):
    c = c_ref[...]
    s = (c * _sigmoid(c)).astype(BF16)
    o_ref[...] = _dot(s, w_ref[...].astype(BF16)) + b_ref[...]


def _ada(cc, w_ada, b_ada):
    d, n = w_ada.shape
    tn = 1536
    return pl.pallas_call(
        _ada_kernel,
        grid=(n // tn,),
        in_specs=[pl.BlockSpec((8, d), lambda j: (0, 0)),
                  pl.BlockSpec((d, tn), lambda j: (0, j)),
                  pl.BlockSpec((1, tn), lambda j: (0, j))],
        out_specs=pl.BlockSpec((8, tn), lambda j: (0, j)),
        out_shape=jax.ShapeDtypeStruct((8, n), F32),
        compiler_params=pltpu.CompilerParams(vmem_limit_bytes=VMEM_LIMIT),
        name="ada",
    )(cc, w_ada, b_ada)


W_IN_LR = (2048, 2080)
W_IN_GROUPS = (((0, 2048), 1.0), ((2080, 3104), 1.0), ((3616, 4640), 1.0), ((4640, 5664), 1.0),
               ((5664, 6688), 1.0), ((3104, 3616), GLA_DK ** -0.5))
W_IN_CHUNKS = tuple((lo + o, scale) for (lo, hi), scale in W_IN_GROUPS for o in range(0, hi - lo, PROJ_CHUNK))
assert len(W_IN_CHUNKS) * PROJ_CHUNK == N_MAIN


def _chunk_table(j, column, divisor=1):
    entry = lambda idx: W_IN_CHUNKS[idx][column] // divisor if divisor != 1 else W_IN_CHUNKS[idx][column]
    out = entry(len(W_IN_CHUNKS) - 1)
    for idx in range(len(W_IN_CHUNKS) - 2, -1, -1):
        out = jnp.where(j == idx, entry(idx), out)
    return out


def _regroup_kernel(src_ref, lr_src_ref, main_ref, lr_ref):
    scale = _chunk_table(pl.program_id(0), 1).astype(F32)
    main_ref[...] = (src_ref[...] * scale).T.astype(BF16)
    rank2 = W_IN_LR[1] - W_IN_LR[0]
    lr_rows = jnp.concatenate([lr_src_ref[...], jnp.zeros((LR_PAD - rank2, lr_src_ref.shape[1]), F32)], axis=0)
    lr_ref[...] = lr_rows.T.astype(BF16)


def _regroup_w_in(w_t):
    n_in, d = w_t.shape
    rank2 = W_IN_LR[1] - W_IN_LR[0]
    return pl.pallas_call(
        _regroup_kernel,
        grid=(len(W_IN_CHUNKS),),
        in_specs=[pl.BlockSpec((pl.Element(PROJ_CHUNK), pl.Element(d)),
                               lambda j: (_chunk_table(j, 0, rank2) * rank2, 0)),
                  pl.BlockSpec((rank2, d), lambda j: (W_IN_LR[0] // rank2, 0))],
        out_specs=[pl.BlockSpec((d, PROJ_CHUNK), lambda j: (0, j)), pl.BlockSpec((d, LR_PAD), lambda j: (0, 0))],
        out_shape=[jax.ShapeDtypeStruct((d, N_MAIN), BF16), jax.ShapeDtypeStruct((d, LR_PAD), BF16)],
        compiler_params=pltpu.CompilerParams(vmem_limit_bytes=VMEM_LIMIT),
        name="regroup_w_in",
    )(w_t, w_t)


def _head_norm(a, gain):
    ms = jnp.mean(a * a, axis=-1, keepdims=True)
    return a * lax.rsqrt(ms + EPS) * gain


def _rope(n, cos, sin_signed):
    lane = lax.broadcasted_iota(jnp.int32, n.shape, 1)
    partner = jnp.where((lane % 64) < 32, pltpu.roll(n, 96, 1), pltpu.roll(n, 32, 1))
    return n * cos + partner * sin_signed


def _log2_decay(lowrank, w, b):
    z = _dot(lowrank.astype(BF16), w) + b
    return (jnp.minimum(z, 0.0) - jnp.log(1.0 + jnp.exp(-jnp.abs(z)))) * (LOG2_E / GLA_GATE_NORM)


def _inproj_kernel(x_ref, sh_ref, sc_ref, n1_ref, w_ref, wlr_ref, wgk_ref, bgk_ref, cos_ref, sin_ref,
                   qg_ref, kg_ref, main_ref, g_ref, *, n_cols, rope):
    half = x_ref.shape[0] // PROJ_ROW_SPLIT
    g_slices = min(8, n_cols // PROJ_CHUNK)
    g_width = g_ref.shape[1] // g_slices
    for part in range(PROJ_ROW_SPLIT):
        r = slice(part * half, (part + 1) * half)
        x = x_ref[r, :]
        ms = jnp.mean(x * x, axis=-1, keepdims=True)
        h = x * lax.rsqrt(ms + EPS) * n1_ref[...]
        hb = (h * (1.0 + sc_ref[...]) + sh_ref[...]).astype(BF16)

        def qk_head(a, gain):
            n = _head_norm(a, gain)
            if rope:
                n = _rope(n, cos_ref[r, :], sin_ref[r, :])
            return n.astype(BF16)

        for c in range(n_cols // PROJ_CHUNK):
            lo = c * PROJ_CHUNK
            acc = _dot(hb, w_ref[:, lo:lo + PROJ_CHUNK])
            for s in range(PROJ_CHUNK // HEAD_DIM):
                col = lo + s * HEAD_DIM
                a = acc[:, s * HEAD_DIM:(s + 1) * HEAD_DIM]
                if COL_AK <= col < COL_AV:
                    main_ref[r, col:col + HEAD_DIM] = qk_head(a, kg_ref[...])
                elif COL_AQ <= col < COL_GO:
                    main_ref[r, col:col + HEAD_DIM] = qk_head(a, qg_ref[...])
                elif COL_GO <= col < COL_GA:
                    main_ref[r, col:col + HEAD_DIM] = (a * _sigmoid(a)).astype(BF16)
                elif COL_GA <= col < COL_GQ:
                    main_ref[r, col:col + HEAD_DIM] = _sigmoid(a).astype(BF16)
                else:
                    main_ref[r, col:col + HEAD_DIM] = a.astype(BF16)
            if c < g_slices:
                if c == 0:
                    lowrank = _dot(hb, wlr_ref[...])
                gs = slice(c * g_width, (c + 1) * g_width)
                g_ref[r, gs] = _log2_decay(lowrank, wgk_ref[:, gs], bgk_ref[:, gs]).astype(BF16)


def _inproj(x2d, mod4, mod_row_of_tile, norm1, w_main, w_lr, w_gk, b_gk, cos_t, sin_t, q_gain, k_gain,
            *, n_cols, rope, tm, seq_tiles):
    rows, d = x2d.shape
    n_g = w_gk.shape[1]
    kern = functools.partial(_inproj_kernel, n_cols=n_cols, rope=rope)
    return pl.pallas_call(
        kern,
        grid=(rows // tm,),
        in_specs=[
            pl.BlockSpec((tm, d), lambda i: (i, 0)),
            pl.BlockSpec((None, None, 1, d), lambda i: (mod_row_of_tile(i), 0, 0, 0)),
            pl.BlockSpec((None, None, 1, d), lambda i: (mod_row_of_tile(i), 1, 0, 0)),
            _resident((1, d), lambda i: (0, 0)),
            _resident((d, n_cols), lambda i: (0, 0)),
            _resident((d, LR_PAD), lambda i: (0, 0)),
            _resident((LR_PAD, n_g), lambda i: (0, 0)),
            _resident((1, n_g), lambda i: (0, 0)),
            pl.BlockSpec((tm, HEAD_DIM), lambda i: (i % seq_tiles, 0)),
            pl.BlockSpec((tm, HEAD_DIM), lambda i: (i % seq_tiles, 0)),
            _resident((1, HEAD_DIM), lambda i: (0, 0)),
            _resident((1, HEAD_DIM), lambda i: (0, 0)),
        ],
        out_specs=[pl.BlockSpec((tm, n_cols), lambda i: (i, 0)),
                   pl.BlockSpec((tm, n_g), lambda i: (i, 0))],
        out_shape=[jax.ShapeDtypeStruct((rows, n_cols), BF16),
                   jax.ShapeDtypeStruct((rows, n_g), BF16)],
        compiler_params=pltpu.CompilerParams(vmem_limit_bytes=VMEM_LIMIT),
        name="inproj_rope" if rope else "inproj_ctx",
    )(x2d, mod4, mod4, norm1, w_main, w_lr, w_gk, b_gk, cos_t, sin_t, q_gain, k_gain)


def _tri(n, upper):
    r = lax.broadcasted_iota(jnp.int32, (n, n), 0)
    c = lax.broadcasted_iota(jnp.int32, (n, n), 1)
    return (c >= r) if upper else (r >= c)


def _running_sums(tri, g):
    return _dot(tri.astype(BF16), g)


def _glastate_kernel(k_ref, v_ref, gf_ref, gb_ref, sf_ref, sb_ref, *, n):
    k = k_ref[...].astype(F32)
    b = _running_sums(_tri(n, False), gf_ref[...])
    k_fwd = (k * jnp.exp2(b[n - 1:n] - b)).astype(BF16)
    e = _running_sums(_tri(n, True), gb_ref[...])
    k_bwd = (k * jnp.exp2(e[0:1] - e)).astype(BF16)
    for h in range(GLA_HEADS):
        v = v_ref[:, h * GLA_DV:(h + 1) * GLA_DV]
        kc = slice(h * GLA_DK, (h + 1) * GLA_DK)
        sf_ref[h] = _dot_tn(v, k_fwd[:, kc])
        sb_ref[h] = _dot_tn(v, k_bwd[:, kc])


def _glastate(ctx_main, ctx_g, *, batch, n_ctx):
    kw, vw = GLA_HEADS * GLA_DK, GLA_HEADS * GLA_DV
    st = jax.ShapeDtypeStruct((batch, GLA_HEADS, GLA_DV, GLA_DK), F32)
    st_spec = pl.BlockSpec((None, GLA_HEADS, GLA_DV, GLA_DK), lambda b: (b, 0, 0, 0))
    return pl.pallas_call(
        functools.partial(_glastate_kernel, n=n_ctx),
        grid=(batch,),
        in_specs=[pl.BlockSpec((n_ctx, kw), lambda b: (b, COL_GK // kw)),
                  pl.BlockSpec((n_ctx, vw), lambda b: (b, COL_GV // vw)),
                  pl.BlockSpec((n_ctx, kw), lambda b: (b, 0)),
                  pl.BlockSpec((n_ctx, kw), lambda b: (b, 1))],
        out_specs=[st_spec, st_spec],
        out_shape=[st, st],
        compiler_params=pltpu.CompilerParams(vmem_limit_bytes=VMEM_LIMIT),
        name="gla_ctx_state",
    )(ctx_main, ctx_main, ctx_g, ctx_g)


GLA_CHUNK = 64
GLA_TBLOCK = 4096
GLA_WAVE = 4


def _gla_kernel(qf_ref, kf_ref, vf_ref, gf_ref, qb_ref, kb_ref, vb_ref, gb_ref,
                sf0_ref, sb0_ref, of_ref, ob_ref, stf, stb):
    @pl.when(pl.program_id(2) == 0)
    def _():
        stf[...] = sf0_ref[...]
        stb[...] = sb0_ref[...]

    c, n = GLA_CHUNK, qf_ref.shape[0] // GLA_CHUNK
    streams = ((gf_ref, qf_ref, kf_ref, vf_ref, of_ref, False),
               (gb_ref, qb_ref, kb_ref, vb_ref, ob_ref, True))
    tri, rows, q, k, v, b, b_mid, b_edge = [], [], [], [], [], [], [], []
    for g_ref, q_ref, k_ref, v_ref, _, upper in streams:
        t = _tri(c, upper)
        edge = 0 if upper else c - 1
        bw = _running_sums(t, jnp.concatenate([g_ref[i * c:(i + 1) * c, :] for i in range(n)], axis=1))
        for i in range(n):
            r = slice(i * c, (i + 1) * c)
            x = bw[:, i * GLA_DK:(i + 1) * GLA_DK]
            tri.append(t), rows.append(r), b.append(x)
            b_mid.append(x[c // 2:c // 2 + 1]), b_edge.append(x[edge:edge + 1])
            q.append(q_ref[r, :]), k.append(k_ref[r, :]), v.append(v_ref[r, :])
    st = [stf[...], stb[...]]
    for w0 in range(0, n, GLA_WAVE):
        u = [(d, d * n + (n - 1 - step if streams[d][5] else step))
             for step in range(w0, w0 + GLA_WAVE) for d in range(len(streams))]
        qt = {j: q[j].astype(F32) * jnp.exp2(b[j] - b_mid[j]) for _, j in u}
        kt = {j: k[j].astype(F32) * jnp.exp2(b_mid[j] - b[j]) for _, j in u}
        s = {j: _dot_nt(qt[j].astype(BF16), kt[j].astype(BF16)) for _, j in u}
        a = {j: jnp.where(tri[j], s[j], 0.0).astype(BF16) for _, j in u}
        k_out = {j: (kt[j] * jnp.exp2(b_edge[j] - b_mid[j])).astype(BF16) for _, j in u}
        upd = {j: _dot_tn(v[j], k_out[j]) for _, j in u}
        intra = {j: _dot(a[j], v[j]) for _, j in u}
        q_in = {j: (qt[j] * jnp.exp2(b_mid[j])).astype(BF16) for _, j in u}
        for d, j in u:
            streams[d][4][rows[j], :] = (intra[j] + _dot_nt(q_in[j], st[d].astype(BF16))).astype(BF16)
            st[d] = st[d] * jnp.exp2(b_edge[j]) + upd[j]
    stf[...] = st[0]
    stb[...] = st[1]


def _gla(main, g, sf0, sb0, *, batch, seq):
    tb = min(GLA_TBLOCK, seq)
    nt = seq // tb
    qb_, kb_, vb_ = COL_GQ // GLA_DK, COL_GK // GLA_DK, COL_GV // GLA_DV

    def fwd(col):
        return lambda b, h, t: (b * nt + t, col + h)

    def bwd(col):
        return lambda b, h, t: (b * nt + nt - 1 - t, col + h)

    def seq_specs(m, g_col):
        return [pl.BlockSpec((tb, GLA_DK), m(qb_)), pl.BlockSpec((tb, GLA_DK), m(kb_)),
                pl.BlockSpec((tb, GLA_DV), m(vb_)), pl.BlockSpec((tb, GLA_DK), m(g_col))]

    st_spec = pl.BlockSpec((None, None, GLA_DV, GLA_DK), lambda b, h, t: (b, h, 0, 0))
    out = jax.ShapeDtypeStruct((batch * seq, GLA_HEADS * GLA_DV), BF16)
    return pl.pallas_call(
        _gla_kernel,
        grid=(batch, GLA_HEADS, nt),
        in_specs=seq_specs(fwd, 0) + seq_specs(bwd, GLA_HEADS) + [st_spec, st_spec],
        out_specs=[pl.BlockSpec((tb, GLA_DV), fwd(0)), pl.BlockSpec((tb, GLA_DV), bwd(0))],
        out_shape=[out, out],
        scratch_shapes=[pltpu.VMEM((GLA_DV, GLA_DK), F32), pltpu.VMEM((GLA_DV, GLA_DK), F32)],
        compiler_params=pltpu.CompilerParams(
            dimension_semantics=("arbitrary", "arbitrary", "arbitrary"), vmem_limit_bytes=VMEM_LIMIT),
        name="gla",
    )(main, main, main, g, main, main, main, g, sf0, sb0)


ATT_TQ = 1024
ATT_TK = 512


ATT_UNSHIFTED_MAX_LOG2 = 56.0


def _attn_kernel(bound_ref, q_ref, k_ref, v_ref, kc_ref, vc_ref, o_ref, vt_sc, vct_sc, m_sc, l_sc, acc_sc, *, seq):
    tq, tk, hd = ATT_TQ, ATT_TK, HEAD_DIM
    nk = seq // tk

    def transposed(v):
        return v.astype(F32).T.astype(BF16)

    @pl.when(pl.program_id(2) == 0)
    def _():
        for j in range(nk):
            vt_sc[j] = transposed(v_ref[j * tk:(j + 1) * tk, :])
        vct_sc[...] = transposed(vc_ref[...])

    q4 = jnp.concatenate([q_ref[:, g * hd:(g + 1) * hd] for g in range(Q_PER_KV)], axis=0)
    acc_sc[...] = jnp.zeros(acc_sc.shape, F32)
    l_sc[...] = jnp.zeros(l_sc.shape, F32)

    def over_keys(step, unroll):
        def body(j, carry):
            step(k_ref[pl.ds(pl.multiple_of(j * tk, tk), tk), :], vt_sc[j])
            return carry

        lax.fori_loop(0, nk, body, 0, unroll=unroll)
        step(kc_ref[...], vct_sc[...])

    def key_sums(e):
        return jnp.sum(e.reshape(e.shape[0] // 8, 8, e.shape[1]), axis=0)

    def unshifted_step(kc, vt):
        e = jnp.exp2(_dot_nt(kc, q4))
        acc_sc[...] += _dot(vt, e.astype(BF16))
        l_sc[...] += key_sums(e)

    def online_step(kc, vt):
        s = _dot_nt(kc, q4)
        m_prev = m_sc[...]
        m_new = jnp.maximum(m_prev, jnp.max(s, axis=0, keepdims=True))
        e = jnp.exp2(s - m_new)
        rescale = jnp.exp2(m_prev - m_new)
        acc_sc[...] = rescale * acc_sc[...] + _dot(vt, e.astype(BF16))
        l_sc[...] = rescale * l_sc[...] + key_sums(e)
        m_sc[...] = m_new

    small = bound_ref[0, 0] <= ATT_UNSHIFTED_MAX_LOG2

    @pl.when(small)
    def _():
        over_keys(unshifted_step, True)

    @pl.when(jnp.logical_not(small))
    def _():
        m_sc[...] = jnp.full(m_sc.shape, -jnp.inf, F32)
        over_keys(online_step, 1)

    o = (acc_sc[...] / jnp.sum(l_sc[...], axis=0, keepdims=True)).T
    for g in range(Q_PER_KV):
        o_ref[:, g * hd:(g + 1) * hd] = o[g * tq:(g + 1) * tq].astype(BF16)


def _attn(score_bound, main, ctx_main, *, batch, seq, n_ctx):
    nq = seq // ATT_TQ
    gw = Q_PER_KV * HEAD_DIM
    rows = Q_PER_KV * ATT_TQ
    vt_rows = HEAD_DIM
    return pl.pallas_call(
        functools.partial(_attn_kernel, seq=seq),
        grid=(batch, N_KV_HEADS, nq),
        in_specs=[pl.BlockSpec(memory_space=pltpu.SMEM),
                  pl.BlockSpec((ATT_TQ, gw), lambda b, h, i: (b * nq + i, COL_AQ // gw + h)),
                  pl.BlockSpec((seq, HEAD_DIM), lambda b, h, i: (b, COL_AK // HEAD_DIM + h)),
                  pl.BlockSpec((seq, HEAD_DIM), lambda b, h, i: (b, COL_AV // HEAD_DIM + h)),
                  pl.BlockSpec((n_ctx, HEAD_DIM), lambda b, h, i: (b, COL_AK // HEAD_DIM + h)),
                  pl.BlockSpec((n_ctx, HEAD_DIM), lambda b, h, i: (b, COL_AV // HEAD_DIM + h))],
        out_specs=pl.BlockSpec((ATT_TQ, gw), lambda b, h, i: (b * nq + i, h)),
        out_shape=jax.ShapeDtypeStruct((batch * seq, N_Q_HEADS * HEAD_DIM), BF16),
        scratch_shapes=[pltpu.VMEM((seq // ATT_TK, vt_rows, ATT_TK), BF16),
                        pltpu.VMEM((vt_rows, n_ctx), BF16),
                        pltpu.VMEM((1, rows), F32), pltpu.VMEM((8, rows), F32),
                        pltpu.VMEM((vt_rows, rows), F32)],
        compiler_params=pltpu.CompilerParams(
            dimension_semantics=("arbitrary", "arbitrary", "arbitrary"), vmem_limit_bytes=VMEM_LIMIT),
        name="attn",
    )(score_bound, main, main, main, ctx_main, ctx_main)


FF_CHUNK = 1024


def _tail_kernel(x_ref, attn_ref, of_ref, ob_ref, go_ref, ga_ref, gg_ref, gn_ref, gt1_ref,
                 wa_ref, wg_ref, wo_ref, sh_ref, sc_ref, gt2_ref, n2_ref, w1_ref, w2_ref, o_ref, *, d_ff):
    gn = gn_ref[...]
    heads = []
    for h in range(GLA_HEADS):
        s = slice(h * GLA_DV, (h + 1) * GLA_DV)
        o = of_ref[:, s].astype(F32) + ob_ref[:, s].astype(F32)
        heads.append((_head_norm(o, gn) * go_ref[:, s].astype(F32)).astype(BF16))
    gla = jnp.concatenate(heads, axis=-1)
    ya = ga_ref[...].astype(F32) * _dot(attn_ref[...], wa_ref[...])
    yg = gg_ref[...].astype(F32) * _dot(gla, wg_ref[...])
    x1 = x_ref[...] + gt1_ref[...] * _dot((ya + yg).astype(BF16), wo_ref[...])
    ms = jnp.mean(x1 * x1, axis=-1, keepdims=True)
    h2 = x1 * lax.rsqrt(ms + EPS) * n2_ref[...]
    hb = (h2 * (1.0 + sc_ref[...]) + sh_ref[...]).astype(BF16)
    acc = jnp.zeros(x1.shape, F32)
    for c in range(d_ff // FF_CHUNK):
        s = slice(c * FF_CHUNK, (c + 1) * FF_CHUNK)
        u = jnp.maximum(_dot(hb, w1_ref[:, s]), 0.0)
        acc = acc + _dot((u * u).astype(BF16), w2_ref[s, :])
    o_ref[...] = x1 + gt2_ref[...] * acc


def _tail(x2d, attn, o_f, o_b, main, gla_norm, mod4, wa, wg, wo, norm2, w1, w2, *, tm, seq_tiles):
    rows, d = x2d.shape
    d_ff = w1.shape[1]
    tile = pl.BlockSpec((tm, d), lambda i: (i, 0))
    mod = lambda g: pl.BlockSpec((None, None, 1, d), lambda i: (i // seq_tiles, g, 0, 0))
    const = lambda shape: _resident(shape, lambda i: (0, 0))
    return pl.pallas_call(
        functools.partial(_tail_kernel, d_ff=d_ff),
        grid=(rows // tm,),
        in_specs=[tile, tile, tile, tile,
                  pl.BlockSpec((tm, d), lambda i: (i, COL_GO // d)),
                  pl.BlockSpec((tm, d), lambda i: (i, COL_GA // d)),
                  pl.BlockSpec((tm, d), lambda i: (i, COL_GG // d)),
                  const((1, GLA_DV)), mod(2), const((d, d)), const((d, d)), const((d, d)),
                  mod(3), mod(4), mod(5), const((1, d)), const((d, d_ff)), const((d_ff, d))],
        out_specs=tile,
        out_shape=jax.ShapeDtypeStruct((rows, d), F32),
        compiler_params=pltpu.CompilerParams(vmem_limit_bytes=VMEM_LIMIT),
        name="merge_out_mlp",
    )(x2d, attn, o_f, o_b, main, main, main, gla_norm, mod4, wa, wg, wo, mod4, mod4, mod4, norm2, w1, w2)


def _rope_tables(seq):
    t = np.arange(seq)
    half = HEAD_DIM // 2
    freqs = ROPE_THETA ** (-np.arange(0, half, 2, dtype=np.float32) / half)
    ang_r = (t // GRID_W).astype(np.float32)[:, None] * freqs
    ang_c = (t % GRID_W).astype(np.float32)[:, None] * freqs
    cos = np.concatenate([np.cos(ang_r)] * 2 + [np.cos(ang_c)] * 2, axis=-1)
    sin = np.concatenate([-np.sin(ang_r), np.sin(ang_r), -np.sin(ang_c), np.sin(ang_c)], axis=-1)
    return jnp.asarray(cos, F32), jnp.asarray(sin, F32)


def _pad_rows(w, row0):
    return jnp.pad(w, ((row0, LR_PAD - row0 - w.shape[0]), (0, 0)))


def kernel(x, c, ctx, c_ctx, w_ada, b_ada, norm1, w_in, q_norm, k_norm, w_gk_fwd, b_gk_fwd,
           w_gk_bwd, b_gk_bwd, gla_norm, w_br_attn, w_br_gla, w_out, norm2, w_mlp1, w_mlp2):
    batch, seq, d = x.shape
    n_ctx = ctx.shape[1]
    depth = w_ada.shape[0]
    assert depth == 1 and seq % min(GLA_TBLOCK, seq) == 0 and seq % ATT_TK == 0 and batch <= 7
    tm = 512
    seq_tiles = seq // tm
    ctx_tm = n_ctx
    l = 0

    cc = jnp.zeros((8, d), F32).at[:batch].set(c).at[batch].set(c_ctx)
    mod4 = _ada(cc, w_ada[l], b_ada[l][None, :]).reshape(8, 6, 1, d)

    w_main, w_lr = _regroup_w_in(jnp.transpose(w_in[l]))
    cos_t, sin_t = _rope_tables(seq)
    q_scale = (HEAD_DIM ** -0.5) * float(np.log2(np.e))
    q_gain = (q_norm[l] * q_scale)[None, :]
    k_gain = k_norm[l][None, :]
    score_bound = (1.01 * HEAD_DIM * q_scale * jnp.max(jnp.abs(q_norm[l])) * jnp.max(jnp.abs(k_norm[l])))
    score_bound = score_bound.reshape(1, 1).astype(F32)
    n1 = norm1[l][None, :]

    w_gk = jnp.concatenate([_pad_rows(w_gk_fwd[l], 0), _pad_rows(w_gk_bwd[l], GLA_GATE_RANK)], axis=1).astype(BF16)
    b_gk = jnp.concatenate([b_gk_fwd[l], b_gk_bwd[l]])[None, :]

    main, g = _inproj(x.reshape(batch * seq, d), mod4, lambda i: i // seq_tiles, n1, w_main, w_lr, w_gk, b_gk,
                      cos_t, sin_t, q_gain, k_gain, n_cols=N_MAIN, rope=True, tm=tm, seq_tiles=seq_tiles)
    ctx_main, ctx_g = _inproj(ctx.reshape(batch * n_ctx, d), mod4, lambda i: batch, n1, w_main, w_lr, w_gk, b_gk,
                              cos_t, sin_t, q_gain, k_gain, n_cols=N_CTX_MAIN, rope=False, tm=ctx_tm,
                              seq_tiles=1)

    sf0, sb0 = _glastate(ctx_main, ctx_g, batch=batch, n_ctx=n_ctx)
    o_f, o_b = _gla(main, g, sf0, sb0, batch=batch, seq=seq)

    attn = _attn(score_bound, main, ctx_main, batch=batch, seq=seq, n_ctx=n_ctx)

    x2 = _tail(x.reshape(batch * seq, d), attn, o_f, o_b, main, gla_norm[l][None, :], mod4,
               w_br_attn[l].astype(BF16), w_br_gla[l].astype(BF16), w_out[l].astype(BF16),
               norm2[l][None, :], w_mlp1[l].astype(BF16), w_mlp2[l].astype(BF16),
               tm=tm, seq_tiles=seq_tiles)
    return x2.reshape(batch, seq, d)
```

```python
import functools

import numpy as np
import jax
import jax.numpy as jnp
from jax import lax
from jax.experimental import pallas as pl
from jax.experimental.pallas import tpu as pltpu

F32 = jnp.float32
BF16 = jnp.bfloat16

GRID_W = 64
HEAD_DIM = 128
N_Q_HEADS = 8
N_KV_HEADS = 2
Q_PER_KV = N_Q_HEADS // N_KV_HEADS
ROPE_THETA = 10000.0
GLA_HEADS = 4
GLA_DK = 128
GLA_DV = 256
GLA_GATE_RANK = 16
GLA_GATE_NORM = 16.0
EPS = 1e-6
LOG2_E = float(np.log2(np.e))

COL_AK, COL_AV, COL_GK, COL_GV = 0, 256, 512, 1024
COL_AQ, COL_GO, COL_GA, COL_GG, COL_GQ = 2048, 3072, 4096, 5120, 6144
N_MAIN = 6656
N_CTX_MAIN = 2048
SUBLANES = 8
LANES = 128
VMEM_BYTES = 64 * 1024 * 1024
VMEM_LIMIT = VMEM_BYTES - 8 * 1024 * 1024

LR_PAD = LANES
TOKEN_TILE = 512
PROJ_CHUNK = 512
PROJ_ROW_SPLIT = 2


def _dot(a, b):
    return jnp.dot(a, b, preferred_element_type=F32)


def _dot_nt(a, b):
    return lax.dot_general(a, b, (((1,), (1,)), ((), ())), preferred_element_type=F32)


def _dot_tn(a, b):
    return lax.dot_general(a, b, (((0,), (0,)), ((), ())), preferred_element_type=F32)


def _sigmoid(x):
    return 0.5 * jnp.tanh(0.5 * x) + 0.5


def _resident(shape, index_map):
    return pl.BlockSpec(shape, index_map, pipeline_mode=pl.Buffered(1))


def _ada_kernel(c_ref, w_ref, b_ref, o_ref):
    c = c_ref[...]
    s = (c * _sigmoid(c)).astype(BF16)
    o_ref[...] = _dot(s, w_ref[...].astype(BF16)) + b_ref[...]


ADA_ROWS = SUBLANES
ADA_COL_TILE = 1536


def _ada(cc, w_ada, b_ada):
    d, n = w_ada.shape
    tn = ADA_COL_TILE
    return pl.pallas_call(
        _ada_kernel,
        grid=(n // tn,),
        in_specs=[pl.BlockSpec((ADA_ROWS, d), lambda j: (0, 0)),
                  pl.BlockSpec((d, tn), lambda j: (0, j)),
                  pl.BlockSpec((1, tn), lambda j: (0, j))],
        out_specs=pl.BlockSpec((ADA_ROWS, tn), lambda j: (0, j)),
        out_shape=jax.ShapeDtypeStruct((ADA_ROWS, n), F32),
        compiler_params=pltpu.CompilerParams(vmem_limit_bytes=VMEM_LIMIT),
        name="ada",
    )(cc, w_ada, b_ada)


W_IN_LR = (2048, 2080)
W_IN_GROUPS = (((0, 2048), 1.0), ((2080, 3104), 1.0), ((3616, 4640), 1.0), ((4640, 5664), 1.0),
               ((5664, 6688), 1.0), ((3104, 3616), GLA_DK ** -0.5))
W_IN_CHUNKS = tuple((lo + o, scale) for (lo, hi), scale in W_IN_GROUPS for o in range(0, hi - lo, PROJ_CHUNK))
assert len(W_IN_CHUNKS) * PROJ_CHUNK == N_MAIN


def _chunk_table(j, column, divisor=1):
    entry = lambda idx: W_IN_CHUNKS[idx][column] // divisor if divisor != 1 else W_IN_CHUNKS[idx][column]
    out = entry(len(W_IN_CHUNKS) - 1)
    for idx in range(len(W_IN_CHUNKS) - 2, -1, -1):
        out = jnp.where(j == idx, entry(idx), out)
    return out


def _regroup_kernel(src_ref, lr_src_ref, main_ref, lr_ref):
    scale = _chunk_table(pl.program_id(0), 1).astype(F32)
    main_ref[...] = (src_ref[...] * scale).T.astype(BF16)
    rank2 = W_IN_LR[1] - W_IN_LR[0]
    lr_rows = jnp.concatenate([lr_src_ref[...], jnp.zeros((LR_PAD - rank2, lr_src_ref.shape[1]), F32)], axis=0)
    lr_ref[...] = lr_rows.T.astype(BF16)


def _regroup_w_in(w_t):
    n_in, d = w_t.shape
    rank2 = W_IN_LR[1] - W_IN_LR[0]
    return pl.pallas_call(
        _regroup_kernel,
        grid=(len(W_IN_CHUNKS),),
        in_specs=[pl.BlockSpec((pl.Element(PROJ_CHUNK), pl.Element(d)),
                               lambda j: (_chunk_table(j, 0, rank2) * rank2, 0)),
                  pl.BlockSpec((rank2, d), lambda j: (W_IN_LR[0] // rank2, 0))],
        out_specs=[pl.BlockSpec((d, PROJ_CHUNK), lambda j: (0, j)), pl.BlockSpec((d, LR_PAD), lambda j: (0, 0))],
        out_shape=[jax.ShapeDtypeStruct((d, N_MAIN), BF16), jax.ShapeDtypeStruct((d, LR_PAD), BF16)],
        compiler_params=pltpu.CompilerParams(vmem_limit_bytes=VMEM_LIMIT),
        name="regroup_w_in",
    )(w_t, w_t)


def _head_norm(a, gain):
    ms = jnp.mean(a * a, axis=-1, keepdims=True)
    return a * lax.rsqrt(ms + EPS) * gain


def _rope(n, cos, sin_signed):
    lane = lax.broadcasted_iota(jnp.int32, n.shape, 1)
    partner = jnp.where((lane % 64) < 32, pltpu.roll(n, 96, 1), pltpu.roll(n, 32, 1))
    return n * cos + partner * sin_signed


def _log2_decay(lowrank, w, b):
    z = _dot(lowrank.astype(BF16), w) + b
    return (jnp.minimum(z, 0.0) - jnp.log(1.0 + jnp.exp(-jnp.abs(z)))) * (LOG2_E / GLA_GATE_NORM)


def _inproj_kernel(x_ref, sh_ref, sc_ref, n1_ref, w_ref, wlr_ref, wgk_ref, bgk_ref, cos_ref, sin_ref,
                   qg_ref, kg_ref, main_ref, g_ref, *, n_cols, rope):
    half = x_ref.shape[0] // PROJ_ROW_SPLIT
    g_slices = min(8, n_cols // PROJ_CHUNK)
    g_width = g_ref.shape[1] // g_slices
    for part in range(PROJ_ROW_SPLIT):
        r = slice(part * half, (part + 1) * half)
        x = x_ref[r, :]
        ms = jnp.mean(x * x, axis=-1, keepdims=True)
        h = x * lax.rsqrt(ms + EPS) * n1_ref[...]
        hb = (h * (1.0 + sc_ref[...]) + sh_ref[...]).astype(BF16)

        def qk_head(a, gain):
            n = _head_norm(a, gain)
            if rope:
                n = _rope(n, cos_ref[r, :], sin_ref[r, :])
            return n.astype(BF16)

        for c in range(n_cols // PROJ_CHUNK):
            lo = c * PROJ_CHUNK
            acc = _dot(hb, w_ref[:, lo:lo + PROJ_CHUNK])
            for s in range(PROJ_CHUNK // HEAD_DIM):
                col = lo + s * HEAD_DIM
                a = acc[:, s * HEAD_DIM:(s + 1) * HEAD_DIM]
                if COL_AK <= col < COL_AV:
                    main_ref[r, col:col + HEAD_DIM] = qk_head(a, kg_ref[...])
                elif COL_AQ <= col < COL_GO:
                    main_ref[r, col:col + HEAD_DIM] = qk_head(a, qg_ref[...])
                elif COL_GO <= col < COL_GA:
                    main_ref[r, col:col + HEAD_DIM] = (a * _sigmoid(a)).astype(BF16)
                elif COL_GA <= col < COL_GQ:
                    main_ref[r, col:col + HEAD_DIM] = _sigmoid(a).astype(BF16)
                else:
                    main_ref[r, col:col + HEAD_DIM] = a.astype(BF16)
            if c < g_slices:
                if c == 0:
                    lowrank = _dot(hb, wlr_ref[...])
                gs = slice(c * g_width, (c + 1) * g_width)
                g_ref[r, gs] = _log2_decay(lowrank, wgk_ref[:, gs], bgk_ref[:, gs]).astype(BF16)


def _inproj(x2d, mod4, mod_row_of_tile, norm1, w_main, w_lr, w_gk, b_gk, cos_t, sin_t, q_gain, k_gain,
            *, n_cols, rope, tm, seq_tiles):
    rows, d = x2d.shape
    n_g = w_gk.shape[1]
    kern = functools.partial(_inproj_kernel, n_cols=n_cols, rope=rope)
    return pl.pallas_call(
        kern,
        grid=(rows // tm,),
        in_specs=[
            pl.BlockSpec((tm, d), lambda i: (i, 0)),
            pl.BlockSpec((None, None, 1, d), lambda i: (mod_row_of_tile(i), 0, 0, 0)),
            pl.BlockSpec((None, None, 1, d), lambda i: (mod_row_of_tile(i), 1, 0, 0)),
            _resident((1, d), lambda i: (0, 0)),
            _resident((d, n_cols), lambda i: (0, 0)),
            _resident((d, LR_PAD), lambda i: (0, 0)),
            _resident((LR_PAD, n_g), lambda i: (0, 0)),
            _resident((1, n_g), lambda i: (0, 0)),
            pl.BlockSpec((tm, HEAD_DIM), lambda i: (i % seq_tiles, 0)),
            pl.BlockSpec((tm, HEAD_DIM), lambda i: (i % seq_tiles, 0)),
            _resident((1, HEAD_DIM), lambda i: (0, 0)),
            _resident((1, HEAD_DIM), lambda i: (0, 0)),
        ],
        out_specs=[pl.BlockSpec((tm, n_cols), lambda i: (i, 0)),
                   pl.BlockSpec((tm, n_g), lambda i: (i, 0))],
        out_shape=[jax.ShapeDtypeStruct((rows, n_cols), BF16),
                   jax.ShapeDtypeStruct((rows, n_g), BF16)],
        compiler_params=pltpu.CompilerParams(vmem_limit_bytes=VMEM_LIMIT),
        name="inproj_rope" if rope else "inproj_ctx",
    )(x2d, mod4, mod4, norm1, w_main, w_lr, w_gk, b_gk, cos_t, sin_t, q_gain, k_gain)


def _tri(n, upper):
    r = lax.broadcasted_iota(jnp.int32, (n, n), 0)
    c = lax.broadcasted_iota(jnp.int32, (n, n), 1)
    return (c >= r) if upper else (r >= c)


def _running_sums(tri, g):
    return _dot(tri.astype(BF16), g)


def _glastate_kernel(k_ref, v_ref, gf_ref, gb_ref, sf_ref, sb_ref, *, n):
    k = k_ref[...].astype(F32)
    b = _running_sums(_tri(n, False), gf_ref[...])
    k_fwd = (k * jnp.exp2(b[n - 1:n] - b)).astype(BF16)
    e = _running_sums(_tri(n, True), gb_ref[...])
    k_bwd = (k * jnp.exp2(e[0:1] - e)).astype(BF16)
    for h in range(GLA_HEADS):
        v = v_ref[:, h * GLA_DV:(h + 1) * GLA_DV]
        kc = slice(h * GLA_DK, (h + 1) * GLA_DK)
        sf_ref[h] = _dot_tn(v, k_fwd[:, kc])
        sb_ref[h] = _dot_tn(v, k_bwd[:, kc])


def _glastate(ctx_main, ctx_g, *, batch, n_ctx):
    kw, vw = GLA_HEADS * GLA_DK, GLA_HEADS * GLA_DV
    st = jax.ShapeDtypeStruct((batch, GLA_HEADS, GLA_DV, GLA_DK), F32)
    st_spec = pl.BlockSpec((None, GLA_HEADS, GLA_DV, GLA_DK), lambda b: (b, 0, 0, 0))
    return pl.pallas_call(
        functools.partial(_glastate_kernel, n=n_ctx),
        grid=(batch,),
        in_specs=[pl.BlockSpec((n_ctx, kw), lambda b: (b, COL_GK // kw)),
                  pl.BlockSpec((n_ctx, vw), lambda b: (b, COL_GV // vw)),
                  pl.BlockSpec((n_ctx, kw), lambda b: (b, 0)),
                  pl.BlockSpec((n_ctx, kw), lambda b: (b, 1))],
        out_specs=[st_spec, st_spec],
        out_shape=[st, st],
        compiler_params=pltpu.CompilerParams(vmem_limit_bytes=VMEM_LIMIT),
        name="gla_ctx_state",
    )(ctx_main, ctx_main, ctx_g, ctx_g)


GLA_CHUNK = 64
GLA_TBLOCK = 4096
GLA_WAVE = 4


def _gla_kernel(qf_ref, kf_ref, vf_ref, gf_ref, qb_ref, kb_ref, vb_ref, gb_ref,
                sf0_ref, sb0_ref, of_ref, ob_ref, stf, stb):
    @pl.when(pl.program_id(2) == 0)
    def _():
        stf[...] = sf0_ref[...]
        stb[...] = sb0_ref[...]

    c, n = GLA_CHUNK, qf_ref.shape[0] // GLA_CHUNK
    streams = ((gf_ref, qf_ref, kf_ref, vf_ref, of_ref, False),
               (gb_ref, qb_ref, kb_ref, vb_ref, ob_ref, True))
    tri, rows, q, k, v, b, b_mid, b_edge = [], [], [], [], [], [], [], []
    for g_ref, q_ref, k_ref, v_ref, _, upper in streams:
        t = _tri(c, upper)
        edge = 0 if upper else c - 1
        bw = _running_sums(t, jnp.concatenate([g_ref[i * c:(i + 1) * c, :] for i in range(n)], axis=1))
        for i in range(n):
            r = slice(i * c, (i + 1) * c)
            x = bw[:, i * GLA_DK:(i + 1) * GLA_DK]
            tri.append(t), rows.append(r), b.append(x)
            b_mid.append(x[c // 2:c // 2 + 1]), b_edge.append(x[edge:edge + 1])
            q.append(q_ref[r, :]), k.append(k_ref[r, :]), v.append(v_ref[r, :])
    st = [stf[...], stb[...]]
    for w0 in range(0, n, GLA_WAVE):
        u = [(d, d * n + (n - 1 - step if streams[d][5] else step))
             for step in range(w0, w0 + GLA_WAVE) for d in range(len(streams))]
        qt = {j: q[j].astype(F32) * jnp.exp2(b[j] - b_mid[j]) for _, j in u}
        kt = {j: k[j].astype(F32) * jnp.exp2(b_mid[j] - b[j]) for _, j in u}
        s = {j: _dot_nt(qt[j].astype(BF16), kt[j].astype(BF16)) for _, j in u}
        a = {j: jnp.where(tri[j], s[j], 0.0).astype(BF16) for _, j in u}
        k_out = {j: (kt[j] * jnp.exp2(b_edge[j] - b_mid[j])).astype(BF16) for _, j in u}
        upd = {j: _dot_tn(v[j], k_out[j]) for _, j in u}
        intra = {j: _dot(a[j], v[j]) for _, j in u}
        q_in = {j: (qt[j] * jnp.exp2(b_mid[j])).astype(BF16) for _, j in u}
        for d, j in u:
            streams[d][4][rows[j], :] = (intra[j] + _dot_nt(q_in[j], st[d].astype(BF16))).astype(BF16)
            st[d] = st[d] * jnp.exp2(b_edge[j]) + upd[j]
    stf[...] = st[0]
    stb[...] = st[1]


def _gla_kernel_one_block(q_ref, k_ref, v_ref, gf_ref, gb_ref, sf0_ref, sb0_ref, of_ref, ob_ref, stf, stb):
    _gla_kernel(q_ref, k_ref, v_ref, gf_ref, q_ref, k_ref, v_ref, gb_ref, sf0_ref, sb0_ref, of_ref, ob_ref, stf, stb)


def _gla(main, g, sf0, sb0, *, batch, seq):
    tb = min(GLA_TBLOCK, seq)
    nt = seq // tb
    qb_, kb_, vb_ = COL_GQ // GLA_DK, COL_GK // GLA_DK, COL_GV // GLA_DV

    def fwd(col):
        return lambda b, h, t: (b * nt + t, col + h)

    def bwd(col):
        return lambda b, h, t: (b * nt + nt - 1 - t, col + h)

    def seq_specs(m, g_col):
        return [pl.BlockSpec((tb, GLA_DK), m(qb_)), pl.BlockSpec((tb, GLA_DK), m(kb_)),
                pl.BlockSpec((tb, GLA_DV), m(vb_)), pl.BlockSpec((tb, GLA_DK), m(g_col))]

    st_spec = pl.BlockSpec((None, None, GLA_DV, GLA_DK), lambda b, h, t: (b, h, 0, 0))
    out = jax.ShapeDtypeStruct((batch * seq, GLA_HEADS * GLA_DV), BF16)
    if nt == 1:
        kern = _gla_kernel_one_block
        in_specs = seq_specs(fwd, 0) + [pl.BlockSpec((tb, GLA_DK), bwd(GLA_HEADS)), st_spec, st_spec]
        operands = (main, main, main, g, g, sf0, sb0)
    else:
        kern = _gla_kernel
        in_specs = seq_specs(fwd, 0) + seq_specs(bwd, GLA_HEADS) + [st_spec, st_spec]
        operands = (main, main, main, g, main, main, main, g, sf0, sb0)
    return pl.pallas_call(
        kern,
        grid=(batch, GLA_HEADS, nt),
        in_specs=in_specs,
        out_specs=[pl.BlockSpec((tb, GLA_DV), fwd(0)), pl.BlockSpec((tb, GLA_DV), bwd(0))],
        out_shape=[out, out],
        scratch_shapes=[pltpu.VMEM((GLA_DV, GLA_DK), F32), pltpu.VMEM((GLA_DV, GLA_DK), F32)],
        compiler_params=pltpu.CompilerParams(
            dimension_semantics=("arbitrary", "arbitrary", "arbitrary"), vmem_limit_bytes=VMEM_LIMIT),
        name="gla",
    )(*operands)


ATT_TQ = 1024
ATT_TK = 512


ATT_UNSHIFTED_MAX_LOG2 = 56.0


def _attn_kernel(bound_ref, q_ref, k_ref, v_ref, kc_ref, vc_ref, o_ref, vt_sc, vct_sc, m_sc, l_sc, acc_sc, *, seq):
    tq, tk, hd = ATT_TQ, ATT_TK, HEAD_DIM
    nk = seq // tk

    def transposed(v):
        return v.astype(F32).T.astype(BF16)

    @pl.when(pl.program_id(2) == 0)
    def _():
        for j in range(nk):
            vt_sc[j] = transposed(v_ref[j * tk:(j + 1) * tk, :])
        vct_sc[...] = transposed(vc_ref[...])

    q4 = jnp.concatenate([q_ref[:, g * hd:(g + 1) * hd] for g in range(Q_PER_KV)], axis=0)
    acc_sc[...] = jnp.zeros(acc_sc.shape, F32)
    l_sc[...] = jnp.zeros(l_sc.shape, F32)

    def over_keys(step, unroll):
        def body(j, carry):
            step(k_ref[pl.ds(pl.multiple_of(j * tk, tk), tk), :], vt_sc[j])
            return carry

        lax.fori_loop(0, nk, body, 0, unroll=unroll)
        step(kc_ref[...], vct_sc[...])

    def key_sums(e):
        return jnp.sum(e.reshape(e.shape[0] // SUBLANES, SUBLANES, e.shape[1]), axis=0)

    def unshifted_step(kc, vt):
        e = jnp.exp2(_dot_nt(kc, q4))
        acc_sc[...] += _dot(vt, e.astype(BF16))
        l_sc[...] += key_sums(e)

    def online_step(kc, vt):
        s = _dot_nt(kc, q4)
        m_prev = m_sc[...]
        m_new = jnp.maximum(m_prev, jnp.max(s, axis=0, keepdims=True))
        e = jnp.exp2(s - m_new)
        rescale = jnp.exp2(m_prev - m_new)
        acc_sc[...] = rescale * acc_sc[...] + _dot(vt, e.astype(BF16))
        l_sc[...] = rescale * l_sc[...] + key_sums(e)
        m_sc[...] = m_new

    small = bound_ref[0, 0] <= ATT_UNSHIFTED_MAX_LOG2

    @pl.when(small)
    def _():
        over_keys(unshifted_step, True)

    @pl.when(jnp.logical_not(small))
    def _():
        m_sc[...] = jnp.full(m_sc.shape, -jnp.inf, F32)
        over_keys(online_step, 1)

    o = (acc_sc[...] / jnp.sum(l_sc[...], axis=0, keepdims=True)).T
    for g in range(Q_PER_KV):
        o_ref[:, g * hd:(g + 1) * hd] = o[g * tq:(g + 1) * tq].astype(BF16)


def _attn(score_bound, main, ctx_main, *, batch, seq, n_ctx):
    nq = seq // ATT_TQ
    gw = Q_PER_KV * HEAD_DIM
    rows = Q_PER_KV * ATT_TQ
    vt_rows = HEAD_DIM
    return pl.pallas_call(
        functools.partial(_attn_kernel, seq=seq),
        grid=(batch, N_KV_HEADS, nq),
        in_specs=[pl.BlockSpec(memory_space=pltpu.SMEM),
                  pl.BlockSpec((ATT_TQ, gw), lambda b, h, i: (b * nq + i, COL_AQ // gw + h)),
                  pl.BlockSpec((seq, HEAD_DIM), lambda b, h, i: (b, COL_AK // HEAD_DIM + h)),
                  pl.BlockSpec((seq, HEAD_DIM), lambda b, h, i: (b, COL_AV // HEAD_DIM + h)),
                  pl.BlockSpec((n_ctx, HEAD_DIM), lambda b, h, i: (b, COL_AK // HEAD_DIM + h)),
                  pl.BlockSpec((n_ctx, HEAD_DIM), lambda b, h, i: (b, COL_AV // HEAD_DIM + h))],
        out_specs=pl.BlockSpec((ATT_TQ, gw), lambda b, h, i: (b * nq + i, h)),
        out_shape=jax.ShapeDtypeStruct((batch * seq, N_Q_HEADS * HEAD_DIM), BF16),
        scratch_shapes=[pltpu.VMEM((seq // ATT_TK, vt_rows, ATT_TK), BF16),
                        pltpu.VMEM((vt_rows, n_ctx), BF16),
                        pltpu.VMEM((1, rows), F32), pltpu.VMEM((SUBLANES, rows), F32),
                        pltpu.VMEM((vt_rows, rows), F32)],
        compiler_params=pltpu.CompilerParams(
            dimension_semantics=("arbitrary", "arbitrary", "arbitrary"), vmem_limit_bytes=VMEM_LIMIT),
        name="attn",
    )(score_bound, main, main, main, ctx_main, ctx_main)


FF_CHUNK = 1024


def _tail_kernel(x_ref, attn_ref, of_ref, ob_ref, go_ref, ga_ref, gg_ref, gn_ref, gt1_ref,
                 wa_ref, wg_ref, wo_ref, sh_ref, sc_ref, gt2_ref, n2_ref, w1_ref, w2_ref, o_ref, *, d_ff):
    gn = gn_ref[...]
    heads = []
    for h in range(GLA_HEADS):
        s = slice(h * GLA_DV, (h + 1) * GLA_DV)
        o = of_ref[:, s].astype(F32) + ob_ref[:, s].astype(F32)
        heads.append((_head_norm(o, gn) * go_ref[:, s].astype(F32)).astype(BF16))
    gla = jnp.concatenate(heads, axis=-1)
    ya = ga_ref[...].astype(F32) * _dot(attn_ref[...], wa_ref[...])
    yg = gg_ref[...].astype(F32) * _dot(gla, wg_ref[...])
    x1 = x_ref[...] + gt1_ref[...] * _dot((ya + yg).astype(BF16), wo_ref[...])
    ms = jnp.mean(x1 * x1, axis=-1, keepdims=True)
    h2 = x1 * lax.rsqrt(ms + EPS) * n2_ref[...]
    hb = (h2 * (1.0 + sc_ref[...]) + sh_ref[...]).astype(BF16)
    acc = jnp.zeros(x1.shape, F32)
    for c in range(d_ff // FF_CHUNK):
        s = slice(c * FF_CHUNK, (c + 1) * FF_CHUNK)
        u = jnp.maximum(_dot(hb, w1_ref[:, s]), 0.0)
        acc = acc + _dot((u * u).astype(BF16), w2_ref[s, :])
    o_ref[...] = x1 + gt2_ref[...] * acc


def _tail(x2d, attn, o_f, o_b, main, gla_norm, mod4, wa, wg, wo, norm2, w1, w2, *, tm, seq_tiles):
    rows, d = x2d.shape
    d_ff = w1.shape[1]
    tile = pl.BlockSpec((tm, d), lambda i: (i, 0))
    mod = lambda g: pl.BlockSpec((None, None, 1, d), lambda i: (i // seq_tiles, g, 0, 0))
    const = lambda shape: _resident(shape, lambda i: (0, 0))
    return pl.pallas_call(
        functools.partial(_tail_kernel, d_ff=d_ff),
        grid=(rows // tm,),
        in_specs=[tile, tile, tile, tile,
                  pl.BlockSpec((tm, d), lambda i: (i, COL_GO // d)),
                  pl.BlockSpec((tm, d), lambda i: (i, COL_GA // d)),
                  pl.BlockSpec((tm, d), lambda i: (i, COL_GG // d)),
                  const((1, GLA_DV)), mod(2), const((d, d)), const((d, d)), const((d, d)),
                  mod(3), mod(4), mod(5), const((1, d)), const((d, d_ff)), const((d_ff, d))],
        out_specs=tile,
        out_shape=jax.ShapeDtypeStruct((rows, d), F32),
        compiler_params=pltpu.CompilerParams(vmem_limit_bytes=VMEM_LIMIT),
        name="merge_out_mlp",
    )(x2d, attn, o_f, o_b, main, main, main, gla_norm, mod4, wa, wg, wo, mod4, mod4, mod4, norm2, w1, w2)


def _rope_tables(seq):
    t = np.arange(seq)
    half = HEAD_DIM // 2
    freqs = ROPE_THETA ** (-np.arange(0, half, 2, dtype=np.float32) / half)
    ang_r = (t // GRID_W).astype(np.float32)[:, None] * freqs
    ang_c = (t % GRID_W).astype(np.float32)[:, None] * freqs
    cos = np.concatenate([np.cos(ang_r)] * 2 + [np.cos(ang_c)] * 2, axis=-1)
    sin = np.concatenate([-np.sin(ang_r), np.sin(ang_r), -np.sin(ang_c), np.sin(ang_c)], axis=-1)
    return jnp.asarray(cos, F32), jnp.asarray(sin, F32)


def _pad_rows(w, row0):
    return jnp.pad(w, ((row0, LR_PAD - row0 - w.shape[0]), (0, 0)))


def kernel(x, c, ctx, c_ctx, w_ada, b_ada, norm1, w_in, q_norm, k_norm, w_gk_fwd, b_gk_fwd,
           w_gk_bwd, b_gk_bwd, gla_norm, w_br_attn, w_br_gla, w_out, norm2, w_mlp1, w_mlp2):
    batch, seq, d = x.shape
    n_ctx = ctx.shape[1]
    depth = w_ada.shape[0]
    assert depth == 1 and batch < ADA_ROWS
    assert seq % min(GLA_TBLOCK, seq) == 0 and seq % ATT_TK == 0 and seq % ATT_TQ == 0 and seq % TOKEN_TILE == 0
    tm = TOKEN_TILE
    seq_tiles = seq // tm
    ctx_tm = n_ctx
    l = 0

    cc = jnp.zeros((ADA_ROWS, d), F32).at[:batch].set(c).at[batch].set(c_ctx)
    mod4 = _ada(cc, w_ada[l], b_ada[l][None, :]).reshape(ADA_ROWS, 6, 1, d)

    w_main, w_lr = _regroup_w_in(jnp.transpose(w_in[l]))
    cos_t, sin_t = _rope_tables(seq)
    q_scale = (HEAD_DIM ** -0.5) * float(np.log2(np.e))
    q_gain = (q_norm[l] * q_scale)[None, :]
    k_gain = k_norm[l][None, :]
    score_bound = (1.01 * HEAD_DIM * q_scale * jnp.max(jnp.abs(q_norm[l])) * jnp.max(jnp.abs(k_norm[l])))
    score_bound = score_bound.reshape(1, 1).astype(F32)
    n1 = norm1[l][None, :]

    w_gk = jnp.concatenate([_pad_rows(w_gk_fwd[l], 0), _pad_rows(w_gk_bwd[l], GLA_GATE_RANK)], axis=1).astype(BF16)
    b_gk = jnp.concatenate([b_gk_fwd[l], b_gk_bwd[l]])[None, :]

    main, g = _inproj(x.reshape(batch * seq, d), mod4, lambda i: i // seq_tiles, n1, w_main, w_lr, w_gk, b_gk,
                      cos_t, sin_t, q_gain, k_gain, n_cols=N_MAIN, rope=True, tm=tm, seq_tiles=seq_tiles)
    ctx_main, ctx_g = _inproj(ctx.reshape(batch * n_ctx, d), mod4, lambda i: batch, n1, w_main, w_lr, w_gk, b_gk,
                              cos_t, sin_t, q_gain, k_gain, n_cols=N_CTX_MAIN, rope=False, tm=ctx_tm,
                              seq_tiles=1)

    sf0, sb0 = _glastate(ctx_main, ctx_g, batch=batch, n_ctx=n_ctx)
    o_f, o_b = _gla(main, g, sf0, sb0, batch=batch, seq=seq)

    attn = _attn(score_bound, main, ctx_main, batch=batch, seq=seq, n_ctx=n_ctx)

    x2 = _tail(x.reshape(batch * seq, d), attn, o_f, o_b, main, gla_norm[l][None, :], mod4,
               w_br_attn[l].astype(BF16), w_br_gla[l].astype(BF16), w_out[l].astype(BF16),
               norm2[l][None, :], w_mlp1[l].astype(BF16), w_mlp2[l].astype(BF16),
               tm=tm, seq_tiles=seq_tiles)
    return x2.reshape(batch, seq, d)
```

```python
import functools

import numpy as np
import jax
import jax.numpy as jnp
from jax import lax
from jax.experimental import pallas as pl
from jax.experimental.pallas import tpu as pltpu

F32 = jnp.float32
BF16 = jnp.bfloat16

GRID_W = 64
HEAD_DIM = 128
N_Q_HEADS = 8
N_KV_HEADS = 2
Q_PER_KV = N_Q_HEADS // N_KV_HEADS
ROPE_THETA = 10000.0
GLA_HEADS = 4
GLA_DK = 128
GLA_DV = 256
GLA_GATE_RANK = 16
GLA_GATE_NORM = 16.0
EPS = 1e-6
LOG2_E = float(np.log2(np.e))

COL_AK, COL_AV, COL_GK, COL_GV = 0, 256, 512, 1024
COL_AQ, COL_GO, COL_GA, COL_GG, COL_GQ = 2048, 3072, 4096, 5120, 6144
N_MAIN = 6656
N_CTX_MAIN = 2048
SUBLANES = 8
LANES = 128
VMEM_BYTES = 64 * 1024 * 1024
VMEM_LIMIT = VMEM_BYTES - 8 * 1024 * 1024

LR_PAD = LANES
TOKEN_TILE = 512
PROJ_CHUNK = 512
PROJ_ROW_SPLIT = 2


def _dot(a, b):
    return jnp.dot(a, b, preferred_element_type=F32)


def _dot_nt(a, b):
    return lax.dot_general(a, b, (((1,), (1,)), ((), ())), preferred_element_type=F32)


def _dot_tn(a, b):
    return lax.dot_general(a, b, (((0,), (0,)), ((), ())), preferred_element_type=F32)


def _sigmoid(x):
    return 0.5 * jnp.tanh(0.5 * x) + 0.5


def _resident(shape, index_map):
    return pl.BlockSpec(shape, index_map, pipeline_mode=pl.Buffered(1))


def _ada_kernel(c_ref, w_ref, b_ref, o_ref):
    c = c_ref[...]
    s = (c * _sigmoid(c)).astype(BF16)
    o_ref[...] = _dot(s, w_ref[...].astype(BF16)) + b_ref[...]


ADA_ROWS = SUBLANES
ADA_COL_TILE = 512


def _ada(cc, w_ada, b_ada):
    d, n = w_ada.shape
    tn = ADA_COL_TILE
    return pl.pallas_call(
        _ada_kernel,
        grid=(n // tn,),
        in_specs=[pl.BlockSpec((ADA_ROWS, d), lambda j: (0, 0)),
                  pl.BlockSpec((d, tn), lambda j: (0, j)),
                  pl.BlockSpec((1, tn), lambda j: (0, j))],
        out_specs=pl.BlockSpec((ADA_ROWS, tn), lambda j: (0, j)),
        out_shape=jax.ShapeDtypeStruct((ADA_ROWS, n), F32),
        compiler_params=pltpu.CompilerParams(vmem_limit_bytes=VMEM_LIMIT),
        name="ada",
    )(cc, w_ada, b_ada)


W_IN_LR = (2048, 2080)
W_IN_GROUPS = (((0, 2048), 1.0), ((2080, 3104), 1.0), ((3616, 4640), 1.0), ((4640, 5664), 1.0),
               ((5664, 6688), 1.0), ((3104, 3616), GLA_DK ** -0.5))
W_IN_CHUNKS = tuple((lo + o, scale) for (lo, hi), scale in W_IN_GROUPS for o in range(0, hi - lo, PROJ_CHUNK))
assert len(W_IN_CHUNKS) * PROJ_CHUNK == N_MAIN


def _chunk_table(j, column, divisor=1):
    entry = lambda idx: W_IN_CHUNKS[idx][column] // divisor if divisor != 1 else W_IN_CHUNKS[idx][column]
    out = entry(len(W_IN_CHUNKS) - 1)
    for idx in range(len(W_IN_CHUNKS) - 2, -1, -1):
        out = jnp.where(j == idx, entry(idx), out)
    return out


def _regroup_kernel(src_ref, lr_src_ref, main_ref, lr_ref):
    scale = _chunk_table(pl.program_id(0), 1).astype(F32)
    main_ref[...] = (src_ref[...] * scale).T.astype(BF16)
    rank2 = W_IN_LR[1] - W_IN_LR[0]
    lr_rows = jnp.concatenate([lr_src_ref[...], jnp.zeros((LR_PAD - rank2, lr_src_ref.shape[1]), F32)], axis=0)
    lr_ref[...] = lr_rows.T.astype(BF16)


def _regroup_w_in(w_t):
    n_in, d = w_t.shape
    rank2 = W_IN_LR[1] - W_IN_LR[0]
    return pl.pallas_call(
        _regroup_kernel,
        grid=(len(W_IN_CHUNKS),),
        in_specs=[pl.BlockSpec((pl.Element(PROJ_CHUNK), pl.Element(d)),
                               lambda j: (_chunk_table(j, 0, rank2) * rank2, 0)),
                  pl.BlockSpec((rank2, d), lambda j: (W_IN_LR[0] // rank2, 0))],
        out_specs=[pl.BlockSpec((d, PROJ_CHUNK), lambda j: (0, j)), pl.BlockSpec((d, LR_PAD), lambda j: (0, 0))],
        out_shape=[jax.ShapeDtypeStruct((d, N_MAIN), BF16), jax.ShapeDtypeStruct((d, LR_PAD), BF16)],
        compiler_params=pltpu.CompilerParams(vmem_limit_bytes=VMEM_LIMIT),
        name="regroup_w_in",
    )(w_t, w_t)


def _head_norm(a, gain):
    ms = jnp.mean(a * a, axis=-1, keepdims=True)
    return a * lax.rsqrt(ms + EPS) * gain


def _rope(n, cos, sin_signed):
    lane = lax.broadcasted_iota(jnp.int32, n.shape, 1)
    partner = jnp.where((lane % 64) < 32, pltpu.roll(n, 96, 1), pltpu.roll(n, 32, 1))
    return n * cos + partner * sin_signed


def _log2_decay(lowrank, w, b):
    z = _dot(lowrank.astype(BF16), w) + b
    return (jnp.minimum(z, 0.0) - jnp.log(1.0 + jnp.exp(-jnp.abs(z)))) * (LOG2_E / GLA_GATE_NORM)


def _inproj_kernel(x_ref, sh_ref, sc_ref, n1_ref, w_ref, wlr_ref, wgk_ref, bgk_ref, cos_ref, sin_ref,
                   qg_ref, kg_ref, main_ref, g_ref, *, n_cols, rope):
    half = x_ref.shape[0] // PROJ_ROW_SPLIT
    g_slices = min(8, n_cols // PROJ_CHUNK)
    g_width = g_ref.shape[1] // g_slices
    for part in range(PROJ_ROW_SPLIT):
        r = slice(part * half, (part + 1) * half)
        x = x_ref[r, :]
        ms = jnp.mean(x * x, axis=-1, keepdims=True)
        h = x * lax.rsqrt(ms + EPS) * n1_ref[...]
        hb = (h * (1.0 + sc_ref[...]) + sh_ref[...]).astype(BF16)

        def qk_head(a, gain):
            n = _head_norm(a, gain)
            if rope:
                n = _rope(n, cos_ref[r, :], sin_ref[r, :])
            return n.astype(BF16)

        for c in range(n_cols // PROJ_CHUNK):
            lo = c * PROJ_CHUNK
            acc = _dot(hb, w_ref[:, lo:lo + PROJ_CHUNK])
            for s in range(PROJ_CHUNK // HEAD_DIM):
                col = lo + s * HEAD_DIM
                a = acc[:, s * HEAD_DIM:(s + 1) * HEAD_DIM]
                if COL_AK <= col < COL_AV:
                    main_ref[r, col:col + HEAD_DIM] = qk_head(a, kg_ref[...])
                elif COL_AQ <= col < COL_GO:
                    main_ref[r, col:col + HEAD_DIM] = qk_head(a, qg_ref[...])
                elif COL_GO <= col < COL_GA:
                    main_ref[r, col:col + HEAD_DIM] = (a * _sigmoid(a)).astype(BF16)
                elif COL_GA <= col < COL_GQ:
                    main_ref[r, col:col + HEAD_DIM] = _sigmoid(a).astype(BF16)
                else:
                    main_ref[r, col:col + HEAD_DIM] = a.astype(BF16)
            if c < g_slices:
                if c == 0:
                    lowrank = _dot(hb, wlr_ref[...])
                gs = slice(c * g_width, (c + 1) * g_width)
                g_ref[r, gs] = _log2_decay(lowrank, wgk_ref[:, gs], bgk_ref[:, gs]).astype(BF16)


def _inproj(x2d, mod4, mod_row_of_tile, norm1, w_main, w_lr, w_gk, b_gk, cos_t, sin_t, q_gain, k_gain,
            *, n_cols, rope, tm, seq_tiles):
    rows, d = x2d.shape
    n_g = w_gk.shape[1]
    kern = functools.partial(_inproj_kernel, n_cols=n_cols, rope=rope)
    return pl.pallas_call(
        kern,
        grid=(rows // tm,),
        in_specs=[
            pl.BlockSpec((tm, d), lambda i: (i, 0)),
            pl.BlockSpec((None, None, 1, d), lambda i: (mod_row_of_tile(i), 0, 0, 0)),
            pl.BlockSpec((None, None, 1, d), lambda i: (mod_row_of_tile(i), 1, 0, 0)),
            _resident((1, d), lambda i: (0, 0)),
            _resident((d, n_cols), lambda i: (0, 0)),
            _resident((d, LR_PAD), lambda i: (0, 0)),
            _resident((LR_PAD, n_g), lambda i: (0, 0)),
            _resident((1, n_g), lambda i: (0, 0)),
            pl.BlockSpec((tm, HEAD_DIM), lambda i: (i % seq_tiles, 0)),
            pl.BlockSpec((tm, HEAD_DIM), lambda i: (i % seq_tiles, 0)),
            _resident((1, HEAD_DIM), lambda i: (0, 0)),
            _resident((1, HEAD_DIM), lambda i: (0, 0)),
        ],
        out_specs=[pl.BlockSpec((tm, n_cols), lambda i: (i, 0)),
                   pl.BlockSpec((tm, n_g), lambda i: (i, 0))],
        out_shape=[jax.ShapeDtypeStruct((rows, n_cols), BF16),
                   jax.ShapeDtypeStruct((rows, n_g), BF16)],
        compiler_params=pltpu.CompilerParams(vmem_limit_bytes=VMEM_LIMIT),
        name="inproj_rope" if rope else "inproj_ctx",
    )(x2d, mod4, mod4, norm1, w_main, w_lr, w_gk, b_gk, cos_t, sin_t, q_gain, k_gain)


def _tri(n, upper):
    r = lax.broadcasted_iota(jnp.int32, (n, n), 0)
    c = lax.broadcasted_iota(jnp.int32, (n, n), 1)
    return (c >= r) if upper else (r >= c)


def _running_sums(tri, g):
    return _dot(tri.astype(BF16), g)


def _glastate_kernel(k_ref, v_ref, gf_ref, gb_ref, sf_ref, sb_ref, *, n):
    k = k_ref[...].astype(F32)
    b = _running_sums(_tri(n, False), gf_ref[...])
    k_fwd = (k * jnp.exp2(b[n - 1:n] - b)).astype(BF16)
    e = _running_sums(_tri(n, True), gb_ref[...])
    k_bwd = (k * jnp.exp2(e[0:1] - e)).astype(BF16)
    for h in range(GLA_HEADS):
        v = v_ref[:, h * GLA_DV:(h + 1) * GLA_DV]
        kc = slice(h * GLA_DK, (h + 1) * GLA_DK)
        sf_ref[h] = _dot_tn(v, k_fwd[:, kc])
        sb_ref[h] = _dot_tn(v, k_bwd[:, kc])


def _glastate(ctx_main, ctx_g, *, batch, n_ctx):
    kw, vw = GLA_HEADS * GLA_DK, GLA_HEADS * GLA_DV
    st = jax.ShapeDtypeStruct((batch, GLA_HEADS, GLA_DV, GLA_DK), F32)
    st_spec = pl.BlockSpec((None, GLA_HEADS, GLA_DV, GLA_DK), lambda b: (b, 0, 0, 0))
    return pl.pallas_call(
        functools.partial(_glastate_kernel, n=n_ctx),
        grid=(batch,),
        in_specs=[pl.BlockSpec((n_ctx, kw), lambda b: (b, COL_GK // kw)),
                  pl.BlockSpec((n_ctx, vw), lambda b: (b, COL_GV // vw)),
                  pl.BlockSpec((n_ctx, kw), lambda b: (b, 0)),
                  pl.BlockSpec((n_ctx, kw), lambda b: (b, 1))],
        out_specs=[st_spec, st_spec],
        out_shape=[st, st],
        compiler_params=pltpu.CompilerParams(vmem_limit_bytes=VMEM_LIMIT),
        name="gla_ctx_state",
    )(ctx_main, ctx_main, ctx_g, ctx_g)


GLA_CHUNK = 64
GLA_TBLOCK = 4096
GLA_WAVE = 4


def _gla_kernel(qf_ref, kf_ref, vf_ref, gf_ref, qb_ref, kb_ref, vb_ref, gb_ref,
                sf0_ref, sb0_ref, of_ref, ob_ref, stf, stb):
    @pl.when(pl.program_id(2) == 0)
    def _():
        stf[...] = sf0_ref[...]
        stb[...] = sb0_ref[...]

    c, n = GLA_CHUNK, qf_ref.shape[0] // GLA_CHUNK
    streams = ((gf_ref, qf_ref, kf_ref, vf_ref, of_ref, False),
               (gb_ref, qb_ref, kb_ref, vb_ref, ob_ref, True))
    tri, rows, q, k, v, b, b_mid, b_edge = [], [], [], [], [], [], [], []
    for g_ref, q_ref, k_ref, v_ref, _, upper in streams:
        t = _tri(c, upper)
        edge = 0 if upper else c - 1
        bw = _running_sums(t, jnp.concatenate([g_ref[i * c:(i + 1) * c, :] for i in range(n)], axis=1))
        for i in range(n):
            r = slice(i * c, (i + 1) * c)
            x = bw[:, i * GLA_DK:(i + 1) * GLA_DK]
            tri.append(t), rows.append(r), b.append(x)
            b_mid.append(x[c // 2:c // 2 + 1]), b_edge.append(x[edge:edge + 1])
            q.append(q_ref[r, :]), k.append(k_ref[r, :]), v.append(v_ref[r, :])
    st = [stf[...], stb[...]]
    for w0 in range(0, n, GLA_WAVE):
        u = [(d, d * n + (n - 1 - step if streams[d][5] else step))
             for step in range(w0, w0 + GLA_WAVE) for d in range(len(streams))]
        qt = {j: q[j].astype(F32) * jnp.exp2(b[j] - b_mid[j]) for _, j in u}
        kt = {j: k[j].astype(F32) * jnp.exp2(b_mid[j] - b[j]) for _, j in u}
        s = {j: _dot_nt(qt[j].astype(BF16), kt[j].astype(BF16)) for _, j in u}
        a = {j: jnp.where(tri[j], s[j], 0.0).astype(BF16) for _, j in u}
        k_out = {j: (kt[j] * jnp.exp2(b_edge[j] - b_mid[j])).astype(BF16) for _, j in u}
        upd = {j: _dot_tn(v[j], k_out[j]) for _, j in u}
        intra = {j: _dot(a[j], v[j]) for _, j in u}
        q_in = {j: (qt[j] * jnp.exp2(b_mid[j])).astype(BF16) for _, j in u}
        for d, j in u:
            streams[d][4][rows[j], :] = (intra[j] + _dot_nt(q_in[j], st[d].astype(BF16))).astype(BF16)
            st[d] = st[d] * jnp.exp2(b_edge[j]) + upd[j]
    stf[...] = st[0]
    stb[...] = st[1]


def _gla_kernel_one_block(q_ref, k_ref, v_ref, gf_ref, gb_ref, sf0_ref, sb0_ref, of_ref, ob_ref, stf, stb):
    _gla_kernel(q_ref, k_ref, v_ref, gf_ref, q_ref, k_ref, v_ref, gb_ref, sf0_ref, sb0_ref, of_ref, ob_ref, stf, stb)


def _gla(main, g, sf0, sb0, *, batch, seq):
    tb = min(GLA_TBLOCK, seq)
    nt = seq // tb
    qb_, kb_, vb_ = COL_GQ // GLA_DK, COL_GK // GLA_DK, COL_GV // GLA_DV

    def fwd(col):
        return lambda b, h, t: (b * nt + t, col + h)

    def bwd(col):
        return lambda b, h, t: (b * nt + nt - 1 - t, col + h)

    def seq_specs(m, g_col):
        return [pl.BlockSpec((tb, GLA_DK), m(qb_)), pl.BlockSpec((tb, GLA_DK), m(kb_)),
                pl.BlockSpec((tb, GLA_DV), m(vb_)), pl.BlockSpec((tb, GLA_DK), m(g_col))]

    st_spec = pl.BlockSpec((None, None, GLA_DV, GLA_DK), lambda b, h, t: (b, h, 0, 0))
    out = jax.ShapeDtypeStruct((batch * seq, GLA_HEADS * GLA_DV), BF16)
    if nt == 1:
        kern = _gla_kernel_one_block
        in_specs = seq_specs(fwd, 0) + [pl.BlockSpec((tb, GLA_DK), bwd(GLA_HEADS)), st_spec, st_spec]
        operands = (main, main, main, g, g, sf0, sb0)
    else:
        kern = _gla_kernel
        in_specs = seq_specs(fwd, 0) + seq_specs(bwd, GLA_HEADS) + [st_spec, st_spec]
        operands = (main, main, main, g, main, main, main, g, sf0, sb0)
    return pl.pallas_call(
        kern,
        grid=(batch, GLA_HEADS, nt),
        in_specs=in_specs,
        out_specs=[pl.BlockSpec((tb, GLA_DV), fwd(0)), pl.BlockSpec((tb, GLA_DV), bwd(0))],
        out_shape=[out, out],
        scratch_shapes=[pltpu.VMEM((GLA_DV, GLA_DK), F32), pltpu.VMEM((GLA_DV, GLA_DK), F32)],
        compiler_params=pltpu.CompilerParams(
            dimension_semantics=("arbitrary", "arbitrary", "arbitrary"), vmem_limit_bytes=VMEM_LIMIT),
        name="gla",
    )(*operands)


ATT_TQ = 1024
ATT_TK = 512


ATT_UNSHIFTED_MAX_LOG2 = 56.0


def _attn_kernel(bound_ref, q_ref, k_ref, v_ref, kc_ref, vc_ref, o_ref, vt_sc, vct_sc, m_sc, l_sc, acc_sc, *, seq):
    tq, tk, hd = ATT_TQ, ATT_TK, HEAD_DIM
    nk = seq // tk

    def transposed(v):
        return v.astype(F32).T.astype(BF16)

    @pl.when(pl.program_id(2) == 0)
    def _():
        for j in range(nk):
            vt_sc[j] = transposed(v_ref[j * tk:(j + 1) * tk, :])
        vct_sc[...] = transposed(vc_ref[...])

    q4 = jnp.concatenate([q_ref[:, g * hd:(g + 1) * hd] for g in range(Q_PER_KV)], axis=0)
    acc_sc[...] = jnp.zeros(acc_sc.shape, F32)
    l_sc[...] = jnp.zeros(l_sc.shape, F32)

    def over_keys(step, unroll):
        def body(j, carry):
            step(k_ref[pl.ds(pl.multiple_of(j * tk, tk), tk), :], vt_sc[j])
            return carry

        lax.fori_loop(0, nk, body, 0, unroll=unroll)
        step(kc_ref[...], vct_sc[...])

    def key_sums(e):
        return jnp.sum(e.reshape(e.shape[0] // SUBLANES, SUBLANES, e.shape[1]), axis=0)

    def unshifted_step(kc, vt):
        e = jnp.exp2(_dot_nt(kc, q4))
        acc_sc[...] += _dot(vt, e.astype(BF16))
        l_sc[...] += key_sums(e)

    def online_step(kc, vt):
        s = _dot_nt(kc, q4)
        m_prev = m_sc[...]
        m_new = jnp.maximum(m_prev, jnp.max(s, axis=0, keepdims=True))
        e = jnp.exp2(s - m_new)
        rescale = jnp.exp2(m_prev - m_new)
        acc_sc[...] = rescale * acc_sc[...] + _dot(vt, e.astype(BF16))
        l_sc[...] = rescale * l_sc[...] + key_sums(e)
        m_sc[...] = m_new

    small = bound_ref[0, 0] <= ATT_UNSHIFTED_MAX_LOG2

    @pl.when(small)
    def _():
        over_keys(unshifted_step, True)

    @pl.when(jnp.logical_not(small))
    def _():
        m_sc[...] = jnp.full(m_sc.shape, -jnp.inf, F32)
        over_keys(online_step, 1)

    o = (acc_sc[...] / jnp.sum(l_sc[...], axis=0, keepdims=True)).T
    for g in range(Q_PER_KV):
        o_ref[:, g * hd:(g + 1) * hd] = o[g * tq:(g + 1) * tq].astype(BF16)


def _attn(score_bound, main, ctx_main, *, batch, seq, n_ctx):
    nq = seq // ATT_TQ
    gw = Q_PER_KV * HEAD_DIM
    rows = Q_PER_KV * ATT_TQ
    vt_rows = HEAD_DIM
    return pl.pallas_call(
        functools.partial(_attn_kernel, seq=seq),
        grid=(batch, N_KV_HEADS, nq),
        in_specs=[pl.BlockSpec(memory_space=pltpu.SMEM),
                  pl.BlockSpec((ATT_TQ, gw), lambda b, h, i: (b * nq + i, COL_AQ // gw + h)),
                  pl.BlockSpec((seq, HEAD_DIM), lambda b, h, i: (b, COL_AK // HEAD_DIM + h)),
                  pl.BlockSpec((seq, HEAD_DIM), lambda b, h, i: (b, COL_AV // HEAD_DIM + h)),
                  pl.BlockSpec((n_ctx, HEAD_DIM), lambda b, h, i: (b, COL_AK // HEAD_DIM + h)),
                  pl.BlockSpec((n_ctx, HEAD_DIM), lambda b, h, i: (b, COL_AV // HEAD_DIM + h))],
        out_specs=pl.BlockSpec((ATT_TQ, gw), lambda b, h, i: (b * nq + i, h)),
        out_shape=jax.ShapeDtypeStruct((batch * seq, N_Q_HEADS * HEAD_DIM), BF16),
        scratch_shapes=[pltpu.VMEM((seq // ATT_TK, vt_rows, ATT_TK), BF16),
                        pltpu.VMEM((vt_rows, n_ctx), BF16),
                        pltpu.VMEM((1, rows), F32), pltpu.VMEM((SUBLANES, rows), F32),
                        pltpu.VMEM((vt_rows, rows), F32)],
        compiler_params=pltpu.CompilerParams(
            dimension_semantics=("arbitrary", "arbitrary", "arbitrary"), vmem_limit_bytes=VMEM_LIMIT),
        name="attn",
    )(score_bound, main, main, main, ctx_main, ctx_main)


FF_CHUNK = 1024


def _tail_kernel(x_ref, attn_ref, of_ref, ob_ref, go_ref, ga_ref, gg_ref, gn_ref, gt1_ref,
                 wa_ref, wg_ref, wo_ref, sh_ref, sc_ref, gt2_ref, n2_ref, w1_ref, w2_ref, o_ref, *, d_ff):
    gn = gn_ref[...]
    heads = []
    for h in range(GLA_HEADS):
        s = slice(h * GLA_DV, (h + 1) * GLA_DV)
        o = of_ref[:, s].astype(F32) + ob_ref[:, s].astype(F32)
        heads.append((_head_norm(o, gn) * go_ref[:, s].astype(F32)).astype(BF16))
    gla = jnp.concatenate(heads, axis=-1)
    ya = ga_ref[...].astype(F32) * _dot(attn_ref[...], wa_ref[...])
    yg = gg_ref[...].astype(F32) * _dot(gla, wg_ref[...])
    x1 = x_ref[...] + gt1_ref[...] * _dot((ya + yg).astype(BF16), wo_ref[...])
    ms = jnp.mean(x1 * x1, axis=-1, keepdims=True)
    h2 = x1 * lax.rsqrt(ms + EPS) * n2_ref[...]
    hb = (h2 * (1.0 + sc_ref[...]) + sh_ref[...]).astype(BF16)
    acc = jnp.zeros(x1.shape, F32)
    for c in range(d_ff // FF_CHUNK):
        s = slice(c * FF_CHUNK, (c + 1) * FF_CHUNK)
        u = jnp.maximum(_dot(hb, w1_ref[:, s]), 0.0)
        acc = acc + _dot((u * u).astype(BF16), w2_ref[s, :])
    o_ref[...] = x1 + gt2_ref[...] * acc


def _tail(x2d, attn, o_f, o_b, main, gla_norm, mod4, wa, wg, wo, norm2, w1, w2, *, tm, seq_tiles):
    rows, d = x2d.shape
    d_ff = w1.shape[1]
    tile = pl.BlockSpec((tm, d), lambda i: (i, 0))
    mod = lambda g: pl.BlockSpec((None, None, 1, d), lambda i: (i // seq_tiles, g, 0, 0))
    const = lambda shape: _resident(shape, lambda i: (0, 0))
    return pl.pallas_call(
        functools.partial(_tail_kernel, d_ff=d_ff),
        grid=(rows // tm,),
        in_specs=[tile, tile, tile, tile,
                  pl.BlockSpec((tm, d), lambda i: (i, COL_GO // d)),
                  pl.BlockSpec((tm, d), lambda i: (i, COL_GA // d)),
                  pl.BlockSpec((tm, d), lambda i: (i, COL_GG // d)),
                  const((1, GLA_DV)), mod(2), const((d, d)), const((d, d)), const((d, d)),
                  mod(3), mod(4), mod(5), const((1, d)), const((d, d_ff)), const((d_ff, d))],
        out_specs=tile,
        out_shape=jax.ShapeDtypeStruct((rows, d), F32),
        compiler_params=pltpu.CompilerParams(vmem_limit_bytes=VMEM_LIMIT),
        name="merge_out_mlp",
    )(x2d, attn, o_f, o_b, main, main, main, gla_norm, mod4, wa, wg, wo, mod4, mod4, mod4, norm2, w1, w2)


def _rope_tables(seq):
    t = np.arange(seq)
    half = HEAD_DIM // 2
    freqs = ROPE_THETA ** (-np.arange(0, half, 2, dtype=np.float32) / half)
    ang_r = (t // GRID_W).astype(np.float32)[:, None] * freqs
    ang_c = (t % GRID_W).astype(np.float32)[:, None] * freqs
    cos = np.concatenate([np.cos(ang_r)] * 2 + [np.cos(ang_c)] * 2, axis=-1)
    sin = np.concatenate([-np.sin(ang_r), np.sin(ang_r), -np.sin(ang_c), np.sin(ang_c)], axis=-1)
    return jnp.asarray(cos, F32), jnp.asarray(sin, F32)


def _pad_rows(w, row0):
    return jnp.pad(w, ((row0, LR_PAD - row0 - w.shape[0]), (0, 0)))


def kernel(x, c, ctx, c_ctx, w_ada, b_ada, norm1, w_in, q_norm, k_norm, w_gk_fwd, b_gk_fwd,
           w_gk_bwd, b_gk_bwd, gla_norm, w_br_attn, w_br_gla, w_out, norm2, w_mlp1, w_mlp2):
    batch, seq, d = x.shape
    n_ctx = ctx.shape[1]
    depth = w_ada.shape[0]
    assert depth == 1 and batch < ADA_ROWS
    assert seq % min(GLA_TBLOCK, seq) == 0 and seq % ATT_TK == 0 and seq % ATT_TQ == 0 and seq % TOKEN_TILE == 0
    tm = TOKEN_TILE
    seq_tiles = seq // tm
    ctx_tm = TOKEN_TILE if (batch * n_ctx) % TOKEN_TILE == 0 else n_ctx
    l = 0

    cc = jnp.zeros((ADA_ROWS, d), F32).at[:batch].set(c).at[batch].set(c_ctx)
    mod4 = _ada(cc, w_ada[l], b_ada[l][None, :]).reshape(ADA_ROWS, 6, 1, d)

    w_main, w_lr = _regroup_w_in(jnp.transpose(w_in[l]))
    cos_t, sin_t = _rope_tables(seq)
    q_scale = (HEAD_DIM ** -0.5) * float(np.log2(np.e))
    q_gain = (q_norm[l] * q_scale)[None, :]
    k_gain = k_norm[l][None, :]
    score_bound = (1.01 * HEAD_DIM * q_scale * jnp.max(jnp.abs(q_norm[l])) * jnp.max(jnp.abs(k_norm[l])))
    score_bound = score_bound.reshape(1, 1).astype(F32)
    n1 = norm1[l][None, :]

    w_gk = jnp.concatenate([_pad_rows(w_gk_fwd[l], 0), _pad_rows(w_gk_bwd[l], GLA_GATE_RANK)], axis=1).astype(BF16)
    b_gk = jnp.concatenate([b_gk_fwd[l], b_gk_bwd[l]])[None, :]

    main, g = _inproj(x.reshape(batch * seq, d), mod4, lambda i: i // seq_tiles, n1, w_main, w_lr, w_gk, b_gk,
                      cos_t, sin_t, q_gain, k_gain, n_cols=N_MAIN, rope=True, tm=tm, seq_tiles=seq_tiles)
    ctx_main, ctx_g = _inproj(ctx.reshape(batch * n_ctx, d), mod4, lambda i: batch, n1, w_main, w_lr, w_gk, b_gk,
                              cos_t, sin_t, q_gain, k_gain, n_cols=N_CTX_MAIN, rope=False, tm=ctx_tm,
                              seq_tiles=1)

    sf0, sb0 = _glastate(ctx_main, ctx_g, batch=batch, n_ctx=n_ctx)
    o_f, o_b = _gla(main, g, sf0, sb0, batch=batch, seq=seq)

    attn = _attn(score_bound, main, ctx_main, batch=batch, seq=seq, n_ctx=n_ctx)

    x2 = _tail(x.reshape(batch * seq, d), attn, o_f, o_b, main, gla_norm[l][None, :], mod4,
               w_br_attn[l].astype(BF16), w_br_gla[l].astype(BF16), w_out[l].astype(BF16),
               norm2[l][None, :], w_mlp1[l].astype(BF16), w_mlp2[l].astype(BF16),
               tm=tm, seq_tiles=seq_tiles)
    return x2.reshape(batch, seq, d)
```

```python
import functools

import numpy as np
import jax
import jax.numpy as jnp
from jax import lax
from jax.experimental import pallas as pl
from jax.experimental.pallas import tpu as pltpu

F32 = jnp.float32
BF16 = jnp.bfloat16

GRID_W = 64
HEAD_DIM = 128
N_Q_HEADS = 8
N_KV_HEADS = 2
Q_PER_KV = N_Q_HEADS // N_KV_HEADS
ROPE_THETA = 10000.0
GLA_HEADS = 4
GLA_DK = 128
GLA_DV = 256
GLA_GATE_RANK = 16
GLA_GATE_NORM = 16.0
EPS = 1e-6
LOG2_E = float(np.log2(np.e))

COL_AK, COL_AV, COL_GK, COL_GV = 0, 256, 512, 1024
COL_AQ, COL_GO, COL_GA, COL_GG, COL_GQ = 2048, 3072, 4096, 5120, 6144
N_MAIN = 6656
N_CTX_MAIN = 2048
SUBLANES = 8
LANES = 128
VMEM_BYTES = 64 * 1024 * 1024
VMEM_LIMIT = VMEM_BYTES - 8 * 1024 * 1024

LR_PAD = LANES
TOKEN_TILE = 512
PROJ_CHUNK = 512
PROJ_ROW_SPLIT = 2


def _dot(a, b):
    return jnp.dot(a, b, preferred_element_type=F32)


def _dot_nt(a, b):
    return lax.dot_general(a, b, (((1,), (1,)), ((), ())), preferred_element_type=F32)


def _dot_tn(a, b):
    return lax.dot_general(a, b, (((0,), (0,)), ((), ())), preferred_element_type=F32)


def _sigmoid(x):
    return 0.5 * jnp.tanh(0.5 * x) + 0.5


def _resident(shape, index_map):
    return pl.BlockSpec(shape, index_map, pipeline_mode=pl.Buffered(1))


def _ada_kernel(c_ref, w_ref, b_ref, o_ref):
    c = c_ref[...]
    s = (c * _sigmoid(c)).astype(BF16)
    o_ref[...] = _dot(s, w_ref[...].astype(BF16)) + b_ref[...]


ADA_ROWS = SUBLANES
ADA_COL_TILE = 1536


def _ada(cc, w_ada, b_ada):
    d, n = w_ada.shape
    tn = ADA_COL_TILE
    return pl.pallas_call(
        _ada_kernel,
        grid=(n // tn,),
        in_specs=[pl.BlockSpec((ADA_ROWS, d), lambda j: (0, 0)),
                  pl.BlockSpec((d, tn), lambda j: (0, j)),
                  pl.BlockSpec((1, tn), lambda j: (0, j))],
        out_specs=pl.BlockSpec((ADA_ROWS, tn), lambda j: (0, j)),
        out_shape=jax.ShapeDtypeStruct((ADA_ROWS, n), F32),
        compiler_params=pltpu.CompilerParams(vmem_limit_bytes=VMEM_LIMIT),
        name="ada",
    )(cc, w_ada, b_ada)


W_IN_LR = (2048, 2080)
W_IN_GROUPS = (((0, 2048), 1.0), ((2080, 3104), 1.0), ((3616, 4640), 1.0), ((4640, 5664), 1.0),
               ((5664, 6688), 1.0), ((3104, 3616), GLA_DK ** -0.5))
W_IN_CHUNKS = tuple((lo + o, scale) for (lo, hi), scale in W_IN_GROUPS for o in range(0, hi - lo, PROJ_CHUNK))
assert len(W_IN_CHUNKS) * PROJ_CHUNK == N_MAIN


def _chunk_table(j, column, divisor=1):
    entry = lambda idx: W_IN_CHUNKS[idx][column] // divisor if divisor != 1 else W_IN_CHUNKS[idx][column]
    out = entry(len(W_IN_CHUNKS) - 1)
    for idx in range(len(W_IN_CHUNKS) - 2, -1, -1):
        out = jnp.where(j == idx, entry(idx), out)
    return out


def _regroup_kernel(src_ref, lr_src_ref, main_ref, lr_ref):
    scale = _chunk_table(pl.program_id(0), 1).astype(F32)
    main_ref[...] = (src_ref[...] * scale).T.astype(BF16)
    rank2 = W_IN_LR[1] - W_IN_LR[0]
    lr_rows = jnp.concatenate([lr_src_ref[...], jnp.zeros((LR_PAD - rank2, lr_src_ref.shape[1]), F32)], axis=0)
    lr_ref[...] = lr_rows.T.astype(BF16)


def _regroup_w_in(w_t):
    n_in, d = w_t.shape
    rank2 = W_IN_LR[1] - W_IN_LR[0]
    return pl.pallas_call(
        _regroup_kernel,
        grid=(len(W_IN_CHUNKS),),
        in_specs=[pl.BlockSpec((pl.Element(PROJ_CHUNK), pl.Element(d)),
                               lambda j: (_chunk_table(j, 0, rank2) * rank2, 0)),
                  pl.BlockSpec((rank2, d), lambda j: (W_IN_LR[0] // rank2, 0))],
        out_specs=[pl.BlockSpec((d, PROJ_CHUNK), lambda j: (0, j)), pl.BlockSpec((d, LR_PAD), lambda j: (0, 0))],
        out_shape=[jax.ShapeDtypeStruct((d, N_MAIN), BF16), jax.ShapeDtypeStruct((d, LR_PAD), BF16)],
        compiler_params=pltpu.CompilerParams(vmem_limit_bytes=VMEM_LIMIT),
        name="regroup_w_in",
    )(w_t, w_t)


def _head_norm(a, gain):
    ms = jnp.mean(a * a, axis=-1, keepdims=True)
    return a * lax.rsqrt(ms + EPS) * gain


def _rope(n, cos, sin_signed):
    lane = lax.broadcasted_iota(jnp.int32, n.shape, 1)
    partner = jnp.where((lane % 64) < 32, pltpu.roll(n, 96, 1), pltpu.roll(n, 32, 1))
    return n * cos + partner * sin_signed


def _log2_decay(lowrank, w, b):
    z = _dot(lowrank.astype(BF16), w) + b
    return (jnp.minimum(z, 0.0) - jnp.log(1.0 + jnp.exp(-jnp.abs(z)))) * (LOG2_E / GLA_GATE_NORM)


def _inproj_kernel(x_ref, sh_ref, sc_ref, n1_ref, w_ref, wlr_ref, wgk_ref, bgk_ref, cos_ref, sin_ref,
                   qg_ref, kg_ref, main_ref, g_ref, *, n_cols, rope):
    half = x_ref.shape[0] // PROJ_ROW_SPLIT
    g_slices = min(8, n_cols // PROJ_CHUNK)
    g_width = g_ref.shape[1] // g_slices
    for part in range(PROJ_ROW_SPLIT):
        r = slice(part * half, (part + 1) * half)
        x = x_ref[r, :]
        ms = jnp.mean(x * x, axis=-1, keepdims=True)
        h = x * lax.rsqrt(ms + EPS) * n1_ref[...]
        hb = (h * (1.0 + sc_ref[...]) + sh_ref[...]).astype(BF16)

        def qk_head(a, gain):
            n = _head_norm(a, gain)
            if rope:
                n = _rope(n, cos_ref[r, :], sin_ref[r, :])
            return n.astype(BF16)

        for c in range(n_cols // PROJ_CHUNK):
            lo = c * PROJ_CHUNK
            acc = _dot(hb, w_ref[:, lo:lo + PROJ_CHUNK])
            for s in range(PROJ_CHUNK // HEAD_DIM):
                col = lo + s * HEAD_DIM
                a = acc[:, s * HEAD_DIM:(s + 1) * HEAD_DIM]
                if COL_AK <= col < COL_AV:
                    main_ref[r, col:col + HEAD_DIM] = qk_head(a, kg_ref[...])
                elif COL_AQ <= col < COL_GO:
                    main_ref[r, col:col + HEAD_DIM] = qk_head(a, qg_ref[...])
                elif COL_GO <= col < COL_GA:
                    main_ref[r, col:col + HEAD_DIM] = (a * _sigmoid(a)).astype(BF16)
                elif COL_GA <= col < COL_GQ:
                    main_ref[r, col:col + HEAD_DIM] = _sigmoid(a).astype(BF16)
                else:
                    main_ref[r, col:col + HEAD_DIM] = a.astype(BF16)
            if c < g_slices:
                if c == 0:
                    lowrank = _dot(hb, wlr_ref[...])
                gs = slice(c * g_width, (c + 1) * g_width)
                g_ref[r, gs] = _log2_decay(lowrank, wgk_ref[:, gs], bgk_ref[:, gs]).astype(BF16)


def _inproj(x2d, mod4, mod_row_of_tile, norm1, w_main, w_lr, w_gk, b_gk, cos_t, sin_t, q_gain, k_gain,
            *, n_cols, rope, tm, seq_tiles):
    rows, d = x2d.shape
    n_g = w_gk.shape[1]
    kern = functools.partial(_inproj_kernel, n_cols=n_cols, rope=rope)
    return pl.pallas_call(
        kern,
        grid=(rows // tm,),
        in_specs=[
            pl.BlockSpec((tm, d), lambda i: (i, 0)),
            pl.BlockSpec((None, None, 1, d), lambda i: (mod_row_of_tile(i), 0, 0, 0)),
            pl.BlockSpec((None, None, 1, d), lambda i: (mod_row_of_tile(i), 1, 0, 0)),
            _resident((1, d), lambda i: (0, 0)),
            _resident((d, n_cols), lambda i: (0, 0)),
            _resident((d, LR_PAD), lambda i: (0, 0)),
            _resident((LR_PAD, n_g), lambda i: (0, 0)),
            _resident((1, n_g), lambda i: (0, 0)),
            pl.BlockSpec((tm, HEAD_DIM), lambda i: (i % seq_tiles, 0)),
            pl.BlockSpec((tm, HEAD_DIM), lambda i: (i % seq_tiles, 0)),
            _resident((1, HEAD_DIM), lambda i: (0, 0)),
            _resident((1, HEAD_DIM), lambda i: (0, 0)),
        ],
        out_specs=[pl.BlockSpec((tm, n_cols), lambda i: (i, 0)),
                   pl.BlockSpec((tm, n_g), lambda i: (i, 0))],
        out_shape=[jax.ShapeDtypeStruct((rows, n_cols), BF16),
                   jax.ShapeDtypeStruct((rows, n_g), BF16)],
        compiler_params=pltpu.CompilerParams(vmem_limit_bytes=VMEM_LIMIT),
        name="inproj_rope" if rope else "inproj_ctx",
    )(x2d, mod4, mod4, norm1, w_main, w_lr, w_gk, b_gk, cos_t, sin_t, q_gain, k_gain)


def _tri(n, upper):
    r = lax.broadcasted_iota(jnp.int32, (n, n), 0)
    c = lax.broadcasted_iota(jnp.int32, (n, n), 1)
    return (c >= r) if upper else (r >= c)


def _running_sums(tri, g):
    return _dot(tri.astype(BF16), g)


def _glastate_kernel(k_ref, v_ref, gf_ref, gb_ref, sf_ref, sb_ref, *, n):
    k = k_ref[...].astype(F32)
    b = _running_sums(_tri(n, False), gf_ref[...])
    k_fwd = (k * jnp.exp2(b[n - 1:n] - b)).astype(BF16)
    e = _running_sums(_tri(n, True), gb_ref[...])
    k_bwd = (k * jnp.exp2(e[0:1] - e)).astype(BF16)
    for h in range(GLA_HEADS):
        v = v_ref[:, h * GLA_DV:(h + 1) * GLA_DV]
        kc = slice(h * GLA_DK, (h + 1) * GLA_DK)
        sf_ref[h] = _dot_tn(v, k_fwd[:, kc])
        sb_ref[h] = _dot_tn(v, k_bwd[:, kc])


def _glastate(ctx_main, ctx_g, *, batch, n_ctx):
    kw, vw = GLA_HEADS * GLA_DK, GLA_HEADS * GLA_DV
    st = jax.ShapeDtypeStruct((batch, GLA_HEADS, GLA_DV, GLA_DK), F32)
    st_spec = pl.BlockSpec((None, GLA_HEADS, GLA_DV, GLA_DK), lambda b: (b, 0, 0, 0))
    return pl.pallas_call(
        functools.partial(_glastate_kernel, n=n_ctx),
        grid=(batch,),
        in_specs=[pl.BlockSpec((n_ctx, kw), lambda b: (b, COL_GK // kw)),
                  pl.BlockSpec((n_ctx, vw), lambda b: (b, COL_GV // vw)),
                  pl.BlockSpec((n_ctx, kw), lambda b: (b, 0)),
                  pl.BlockSpec((n_ctx, kw), lambda b: (b, 1))],
        out_specs=[st_spec, st_spec],
        out_shape=[st, st],
        compiler_params=pltpu.CompilerParams(vmem_limit_bytes=VMEM_LIMIT),
        name="gla_ctx_state",
    )(ctx_main, ctx_main, ctx_g, ctx_g)


GLA_CHUNK = 64
GLA_TBLOCK = 4096
GLA_WAVE = 4
GLA_FUSED_EXPERIMENT = True


def _gla_kernel(qf_ref, kf_ref, vf_ref, gf_ref, qb_ref, kb_ref, vb_ref, gb_ref,
                sf0_ref, sb0_ref, of_ref, ob_ref, stf, stb):
    @pl.when(pl.program_id(2) == 0)
    def _():
        stf[...] = sf0_ref[...]
        stb[...] = sb0_ref[...]

    c, n = GLA_CHUNK, qf_ref.shape[0] // GLA_CHUNK
    streams = ((gf_ref, qf_ref, kf_ref, vf_ref, of_ref, False),
               (gb_ref, qb_ref, kb_ref, vb_ref, ob_ref, True))
    tri, rows, q, k, v, b, b_mid, b_edge = [], [], [], [], [], [], [], []
    for g_ref, q_ref, k_ref, v_ref, _, upper in streams:
        t = _tri(c, upper)
        edge = 0 if upper else c - 1
        bw = _running_sums(t, jnp.concatenate([g_ref[i * c:(i + 1) * c, :] for i in range(n)], axis=1))
        for i in range(n):
            r = slice(i * c, (i + 1) * c)
            x = bw[:, i * GLA_DK:(i + 1) * GLA_DK]
            tri.append(t), rows.append(r), b.append(x)
            b_mid.append(x[c // 2:c // 2 + 1]), b_edge.append(x[edge:edge + 1])
            q.append(q_ref[r, :]), k.append(k_ref[r, :]), v.append(v_ref[r, :])
    st = [stf[...], stb[...]]
    for w0 in range(0, n, GLA_WAVE):
        u = [(d, d * n + (n - 1 - step if streams[d][5] else step))
             for step in range(w0, w0 + GLA_WAVE) for d in range(len(streams))]
        qt = {j: q[j].astype(F32) * jnp.exp2(b[j] - b_mid[j]) for _, j in u}
        kt = {j: k[j].astype(F32) * jnp.exp2(b_mid[j] - b[j]) for _, j in u}
        s = {j: _dot_nt(qt[j].astype(BF16), kt[j].astype(BF16)) for _, j in u}
        a = {j: jnp.where(tri[j], s[j], 0.0).astype(BF16) for _, j in u}
        k_out = {j: (kt[j] * jnp.exp2(b_edge[j] - b_mid[j])).astype(BF16) for _, j in u}
        upd = {j: _dot_tn(v[j], k_out[j]) for _, j in u}
        intra = {j: _dot(a[j], v[j]) for _, j in u}
        q_in = {j: (qt[j] * jnp.exp2(b_mid[j])).astype(BF16) for _, j in u}
        for d, j in u:
            streams[d][4][rows[j], :] = (intra[j] + _dot_nt(q_in[j], st[d].astype(BF16))).astype(BF16)
            st[d] = st[d] * jnp.exp2(b_edge[j]) + upd[j]
    stf[...] = st[0]
    stb[...] = st[1]


def _gla_kernel_fused(q_ref, k_ref, v_ref, gf_ref, gb_ref, sf0_ref, sb0_ref, of_ref, ob_ref, stf, stb):
    @pl.when(pl.program_id(2) == 0)
    def _():
        stf[...] = sf0_ref[...]
        stb[...] = sb0_ref[...]

    c, n = GLA_CHUNK, q_ref.shape[0] // GLA_CHUNK
    streams = ((gf_ref, of_ref, False), (gb_ref, ob_ref, True))
    r_i = lax.broadcasted_iota(jnp.int32, (c, 2 * c), 0)
    c_i = lax.broadcasted_iota(jnp.int32, (c, 2 * c), 1)
    masks, b, b_mid, b_edge, dec_cols = {}, [], [], [], []
    for g_ref, _, upper in streams:
        t = _tri(c, upper)
        edge = 0 if upper else c - 1
        for pos in range(2):
            local = c_i - pos * c
            inside = (local >= 0) & (local < c)
            masks[(upper, pos)] = inside & ((local >= r_i) if upper else (r_i >= local))
        bw = _running_sums(t, jnp.concatenate([g_ref[i * c:(i + 1) * c, :] for i in range(n)], axis=1))
        edges = []
        for i in range(n):
            x = bw[:, i * GLA_DK:(i + 1) * GLA_DK]
            b.append(x), b_mid.append(x[c // 2:c // 2 + 1]), b_edge.append(x[edge:edge + 1])
            edges.append(x[edge:edge + 1])
        dec_cols.append(jnp.exp2(jnp.concatenate(edges, axis=0)).T)
    st = [stf[...], stb[...]]
    zeros = jnp.zeros((c, GLA_DK), BF16)
    for w0 in range(0, n, GLA_WAVE):
        u = [(d, (n - 1 - step if streams[d][2] else step))
             for step in range(w0, w0 + GLA_WAVE) for d in range(len(streams))]
        key = lambda d, i: d * n + i
        rows = lambda i: slice(i * c, (i + 1) * c)
        qt = {(d, i): q_ref[rows(i), :].astype(F32) * jnp.exp2(b[key(d, i)] - b_mid[key(d, i)]) for d, i in u}
        kt = {(d, i): k_ref[rows(i), :].astype(F32) * jnp.exp2(b_mid[key(d, i)] - b[key(d, i)]) for d, i in u}
        ktp = {e: jnp.concatenate([kt[e].astype(BF16), zeros] if e[1] % 2 == 0 else [zeros, kt[e].astype(BF16)], axis=0)
               for e in u}
        s = {e: _dot_nt(qt[e].astype(BF16), ktp[e]) for e in u}
        a = {e: jnp.where(masks[(streams[e[0]][2], e[1] % 2)], s[e], 0.0).astype(BF16) for e in u}
        k_out = {(d, i): (kt[(d, i)] * jnp.exp2(b_edge[key(d, i)] - b_mid[key(d, i)])).astype(BF16) for d, i in u}
        upd = {(d, i): _dot_tn(k_out[(d, i)], v_ref[rows(i), :]) for d, i in u}
        q_in = {(d, i): (qt[(d, i)] * jnp.exp2(b_mid[key(d, i)])).astype(BF16) for d, i in u}
        for d, i in u:
            pair = slice((i // 2) * 2 * c, (i // 2 + 1) * 2 * c)
            lhs = jnp.concatenate([q_in[(d, i)], a[(d, i)]], axis=1)
            rhs = jnp.concatenate([st[d].astype(BF16), v_ref[pair, :]], axis=0)
            streams[d][1][rows(i), :] = _dot(lhs, rhs).astype(BF16)
            st[d] = st[d] * dec_cols[d][:, i:i + 1] + upd[(d, i)]
    stf[...] = st[0]
    stb[...] = st[1]


def _gla_kernel_one_block(q_ref, k_ref, v_ref, gf_ref, gb_ref, sf0_ref, sb0_ref, of_ref, ob_ref, stf, stb):
    _gla_kernel(q_ref, k_ref, v_ref, gf_ref, q_ref, k_ref, v_ref, gb_ref, sf0_ref, sb0_ref, of_ref, ob_ref, stf, stb)


def _gla(main, g, sf0, sb0, *, batch, seq):
    tb = min(GLA_TBLOCK, seq)
    nt = seq // tb
    qb_, kb_, vb_ = COL_GQ // GLA_DK, COL_GK // GLA_DK, COL_GV // GLA_DV

    def fwd(col):
        return lambda b, h, t: (b * nt + t, col + h)

    def bwd(col):
        return lambda b, h, t: (b * nt + nt - 1 - t, col + h)

    def seq_specs(m, g_col):
        return [pl.BlockSpec((tb, GLA_DK), m(qb_)), pl.BlockSpec((tb, GLA_DK), m(kb_)),
                pl.BlockSpec((tb, GLA_DV), m(vb_)), pl.BlockSpec((tb, GLA_DK), m(g_col))]

    st_spec = pl.BlockSpec((None, None, GLA_DV, GLA_DK), lambda b, h, t: (b, h, 0, 0))
    out = jax.ShapeDtypeStruct((batch * seq, GLA_HEADS * GLA_DV), BF16)
    st_shape = (GLA_DV, GLA_DK)
    if nt == 1 and GLA_FUSED_EXPERIMENT:
        kern = _gla_kernel_fused
        st_shape = (GLA_DK, GLA_DV)
        st_spec = pl.BlockSpec((None, None, GLA_DK, GLA_DV), lambda b, h, t: (b, h, 0, 0))
        in_specs = seq_specs(fwd, 0) + [pl.BlockSpec((tb, GLA_DK), bwd(GLA_HEADS)), st_spec, st_spec]
        operands = (main, main, main, g, g, jnp.swapaxes(sf0, 2, 3), jnp.swapaxes(sb0, 2, 3))
    elif nt == 1:
        kern = _gla_kernel_one_block
        in_specs = seq_specs(fwd, 0) + [pl.BlockSpec((tb, GLA_DK), bwd(GLA_HEADS)), st_spec, st_spec]
        operands = (main, main, main, g, g, sf0, sb0)
    else:
        kern = _gla_kernel
        in_specs = seq_specs(fwd, 0) + seq_specs(bwd, GLA_HEADS) + [st_spec, st_spec]
        operands = (main, main, main, g, main, main, main, g, sf0, sb0)
    return pl.pallas_call(
        kern,
        grid=(batch, GLA_HEADS, nt),
        in_specs=in_specs,
        out_specs=[pl.BlockSpec((tb, GLA_DV), fwd(0)), pl.BlockSpec((tb, GLA_DV), bwd(0))],
        out_shape=[out, out],
        scratch_shapes=[pltpu.VMEM(st_shape, F32), pltpu.VMEM(st_shape, F32)],
        compiler_params=pltpu.CompilerParams(
            dimension_semantics=("arbitrary", "arbitrary", "arbitrary"), vmem_limit_bytes=VMEM_LIMIT),
        name="gla",
    )(*operands)


ATT_TQ = 1024
ATT_TK = 512


ATT_UNSHIFTED_MAX_LOG2 = 56.0


def _attn_kernel(bound_ref, q_ref, k_ref, v_ref, kc_ref, vc_ref, o_ref, vt_sc, vct_sc, m_sc, l_sc, acc_sc, *, seq):
    tq, tk, hd = ATT_TQ, ATT_TK, HEAD_DIM
    nk = seq // tk

    def transposed(v):
        return v.astype(F32).T.astype(BF16)

    @pl.when(pl.program_id(2) == 0)
    def _():
        for j in range(nk):
            vt_sc[j] = transposed(v_ref[j * tk:(j + 1) * tk, :])
        vct_sc[...] = transposed(vc_ref[...])

    q4 = jnp.concatenate([q_ref[:, g * hd:(g + 1) * hd] for g in range(Q_PER_KV)], axis=0)

    def key_sums(e):
        return jnp.sum(e.reshape(e.shape[0] // SUBLANES, SUBLANES, e.shape[1]), axis=0)

    def finish(acc, l):
        o = (acc / jnp.sum(l, axis=0, keepdims=True)).T
        for g in range(Q_PER_KV):
            o_ref[:, g * hd:(g + 1) * hd] = o[g * tq:(g + 1) * tq].astype(BF16)

    small = bound_ref[0, 0] <= ATT_UNSHIFTED_MAX_LOG2

    @pl.when(small)
    def _():
        chunks = [(k_ref[j * tk:(j + 1) * tk, :], vt_sc[j]) for j in range(nk)] + [(kc_ref[...], vct_sc[...])]
        acc = l = None
        for kc, vt in chunks:
            e = jnp.exp2(_dot_nt(kc, q4))
            pv, ks = _dot(vt, e.astype(BF16)), key_sums(e)
            acc, l = (pv, ks) if acc is None else (acc + pv, l + ks)
        finish(acc, l)

    @pl.when(jnp.logical_not(small))
    def _():
        m_sc[...] = jnp.full(m_sc.shape, -jnp.inf, F32)
        acc_sc[...] = jnp.zeros(acc_sc.shape, F32)
        l_sc[...] = jnp.zeros(l_sc.shape, F32)

        def online_step(kc, vt):
            s = _dot_nt(kc, q4)
            m_prev = m_sc[...]
            m_new = jnp.maximum(m_prev, jnp.max(s, axis=0, keepdims=True))
            e = jnp.exp2(s - m_new)
            rescale = jnp.exp2(m_prev - m_new)
            acc_sc[...] = rescale * acc_sc[...] + _dot(vt, e.astype(BF16))
            l_sc[...] = rescale * l_sc[...] + key_sums(e)
            m_sc[...] = m_new

        def body(j, carry):
            online_step(k_ref[pl.ds(pl.multiple_of(j * tk, tk), tk), :], vt_sc[j])
            return carry

        lax.fori_loop(0, nk, body, 0)
        online_step(kc_ref[...], vct_sc[...])
        finish(acc_sc[...], l_sc[...])


def _attn(score_bound, main, ctx_main, *, batch, seq, n_ctx):
    nq = seq // ATT_TQ
    gw = Q_PER_KV * HEAD_DIM
    rows = Q_PER_KV * ATT_TQ
    vt_rows = HEAD_DIM
    return pl.pallas_call(
        functools.partial(_attn_kernel, seq=seq),
        grid=(batch, N_KV_HEADS, nq),
        in_specs=[pl.BlockSpec(memory_space=pltpu.SMEM),
                  pl.BlockSpec((ATT_TQ, gw), lambda b, h, i: (b * nq + i, COL_AQ // gw + h)),
                  pl.BlockSpec((seq, HEAD_DIM), lambda b, h, i: (b, COL_AK // HEAD_DIM + h)),
                  pl.BlockSpec((seq, HEAD_DIM), lambda b, h, i: (b, COL_AV // HEAD_DIM + h)),
                  pl.BlockSpec((n_ctx, HEAD_DIM), lambda b, h, i: (b, COL_AK // HEAD_DIM + h)),
                  pl.BlockSpec((n_ctx, HEAD_DIM), lambda b, h, i: (b, COL_AV // HEAD_DIM + h))],
        out_specs=pl.BlockSpec((ATT_TQ, gw), lambda b, h, i: (b * nq + i, h)),
        out_shape=jax.ShapeDtypeStruct((batch * seq, N_Q_HEADS * HEAD_DIM), BF16),
        scratch_shapes=[pltpu.VMEM((seq // ATT_TK, vt_rows, ATT_TK), BF16),
                        pltpu.VMEM((vt_rows, n_ctx), BF16),
                        pltpu.VMEM((1, rows), F32), pltpu.VMEM((SUBLANES, rows), F32),
                        pltpu.VMEM((vt_rows, rows), F32)],
        compiler_params=pltpu.CompilerParams(
            dimension_semantics=("arbitrary", "arbitrary", "arbitrary"), vmem_limit_bytes=VMEM_LIMIT),
        name="attn",
    )(score_bound, main, main, main, ctx_main, ctx_main)


FF_CHUNK = 1024


def _tail_kernel(x_ref, attn_ref, of_ref, ob_ref, go_ref, ga_ref, gg_ref, gn_ref, gt1_ref,
                 wa_ref, wg_ref, wo_ref, sh_ref, sc_ref, gt2_ref, n2_ref, w1_ref, w2_ref, o_ref, *, d_ff):
    gn = gn_ref[...]
    heads = []
    for h in range(GLA_HEADS):
        s = slice(h * GLA_DV, (h + 1) * GLA_DV)
        o = of_ref[:, s].astype(F32) + ob_ref[:, s].astype(F32)
        heads.append((_head_norm(o, gn) * go_ref[:, s].astype(F32)).astype(BF16))
    gla = jnp.concatenate(heads, axis=-1)
    ya = ga_ref[...].astype(F32) * _dot(attn_ref[...], wa_ref[...])
    yg = gg_ref[...].astype(F32) * _dot(gla, wg_ref[...])
    x1 = x_ref[...] + gt1_ref[...] * _dot((ya + yg).astype(BF16), wo_ref[...])
    ms = jnp.mean(x1 * x1, axis=-1, keepdims=True)
    h2 = x1 * lax.rsqrt(ms + EPS) * n2_ref[...]
    hb = (h2 * (1.0 + sc_ref[...]) + sh_ref[...]).astype(BF16)
    acc = jnp.zeros(x1.shape, F32)
    for c in range(d_ff // FF_CHUNK):
        s = slice(c * FF_CHUNK, (c + 1) * FF_CHUNK)
        u = jnp.maximum(_dot(hb, w1_ref[:, s]), 0.0)
        acc = acc + _dot((u * u).astype(BF16), w2_ref[s, :])
    o_ref[...] = x1 + gt2_ref[...] * acc


def _tail(x2d, attn, o_f, o_b, main, gla_norm, mod4, wa, wg, wo, norm2, w1, w2, *, tm, seq_tiles):
    rows, d = x2d.shape
    d_ff = w1.shape[1]
    tile = pl.BlockSpec((tm, d), lambda i: (i, 0))
    mod = lambda g: pl.BlockSpec((None, None, 1, d), lambda i: (i // seq_tiles, g, 0, 0))
    const = lambda shape: _resident(shape, lambda i: (0, 0))
    return pl.pallas_call(
        functools.partial(_tail_kernel, d_ff=d_ff),
        grid=(rows // tm,),
        in_specs=[tile, tile, tile, tile,
                  pl.BlockSpec((tm, d), lambda i: (i, COL_GO // d)),
                  pl.BlockSpec((tm, d), lambda i: (i, COL_GA // d)),
                  pl.BlockSpec((tm, d), lambda i: (i, COL_GG // d)),
                  const((1, GLA_DV)), mod(2), const((d, d)), const((d, d)), const((d, d)),
                  mod(3), mod(4), mod(5), const((1, d)), const((d, d_ff)), const((d_ff, d))],
        out_specs=tile,
        out_shape=jax.ShapeDtypeStruct((rows, d), F32),
        compiler_params=pltpu.CompilerParams(vmem_limit_bytes=VMEM_LIMIT),
        name="merge_out_mlp",
    )(x2d, attn, o_f, o_b, main, main, main, gla_norm, mod4, wa, wg, wo, mod4, mod4, mod4, norm2, w1, w2)


def _rope_tables(seq):
    t = np.arange(seq)
    half = HEAD_DIM // 2
    freqs = ROPE_THETA ** (-np.arange(0, half, 2, dtype=np.float32) / half)
    ang_r = (t // GRID_W).astype(np.float32)[:, None] * freqs
    ang_c = (t % GRID_W).astype(np.float32)[:, None] * freqs
    cos = np.concatenate([np.cos(ang_r)] * 2 + [np.cos(ang_c)] * 2, axis=-1)
    sin = np.concatenate([-np.sin(ang_r), np.sin(ang_r), -np.sin(ang_c), np.sin(ang_c)], axis=-1)
    return jnp.asarray(cos, F32), jnp.asarray(sin, F32)


def _pad_rows(w, row0):
    return jnp.pad(w, ((row0, LR_PAD - row0 - w.shape[0]), (0, 0)))


def kernel(x, c, ctx, c_ctx, w_ada, b_ada, norm1, w_in, q_norm, k_norm, w_gk_fwd, b_gk_fwd,
           w_gk_bwd, b_gk_bwd, gla_norm, w_br_attn, w_br_gla, w_out, norm2, w_mlp1, w_mlp2):
    batch, seq, d = x.shape
    n_ctx = ctx.shape[1]
    depth = w_ada.shape[0]
    assert depth == 1 and batch < ADA_ROWS
    assert seq % min(GLA_TBLOCK, seq) == 0 and seq % ATT_TK == 0 and seq % ATT_TQ == 0 and seq % TOKEN_TILE == 0
    tm = TOKEN_TILE
    seq_tiles = seq // tm
    ctx_tm = n_ctx
    l = 0

    cc = jnp.zeros((ADA_ROWS, d), F32).at[:batch].set(c).at[batch].set(c_ctx)
    mod4 = _ada(cc, w_ada[l], b_ada[l][None, :]).reshape(ADA_ROWS, 6, 1, d)

    w_main, w_lr = _regroup_w_in(jnp.transpose(w_in[l]))
    cos_t, sin_t = _rope_tables(seq)
    q_scale = (HEAD_DIM ** -0.5) * float(np.log2(np.e))
    q_gain = (q_norm[l] * q_scale)[None, :]
    k_gain = k_norm[l][None, :]
    score_bound = (1.01 * HEAD_DIM * q_scale * jnp.max(jnp.abs(q_norm[l])) * jnp.max(jnp.abs(k_norm[l])))
    score_bound = score_bound.reshape(1, 1).astype(F32)
    n1 = norm1[l][None, :]

    w_gk = jnp.concatenate([_pad_rows(w_gk_fwd[l], 0), _pad_rows(w_gk_bwd[l], GLA_GATE_RANK)], axis=1).astype(BF16)
    b_gk = jnp.concatenate([b_gk_fwd[l], b_gk_bwd[l]])[None, :]

    main, g = _inproj(x.reshape(batch * seq, d), mod4, lambda i: i // seq_tiles, n1, w_main, w_lr, w_gk, b_gk,
                      cos_t, sin_t, q_gain, k_gain, n_cols=N_MAIN, rope=True, tm=tm, seq_tiles=seq_tiles)
    ctx_main, ctx_g = _inproj(ctx.reshape(batch * n_ctx, d), mod4, lambda i: batch, n1, w_main, w_lr, w_gk, b_gk,
                              cos_t, sin_t, q_gain, k_gain, n_cols=N_CTX_MAIN, rope=False, tm=ctx_tm,
                              seq_tiles=1)

    sf0, sb0 = _glastate(ctx_main, ctx_g, batch=batch, n_ctx=n_ctx)
    o_f, o_b = _gla(main, g, sf0, sb0, batch=batch, seq=seq)

    attn = _attn(score_bound, main, ctx_main, batch=batch, seq=seq, n_ctx=n_ctx)

    x2 = _tail(x.reshape(batch * seq, d), attn, o_f, o_b, main, gla_norm[l][None, :], mod4,
               w_br_attn[l].astype(BF16), w_br_gla[l].astype(BF16), w_out[l].astype(BF16),
               norm2[l][None, :], w_mlp1[l].astype(BF16), w_mlp2[l].astype(BF16),
               tm=tm, seq_tiles=seq_tiles)
    return x2.reshape(batch, seq, d)
```

```python
import functools

import numpy as np
import jax
import jax.numpy as jnp
from jax import lax
from jax.experimental import pallas as pl
from jax.experimental.pallas import tpu as pltpu

F32 = jnp.float32
BF16 = jnp.bfloat16

GRID_W = 64
HEAD_DIM = 128
N_Q_HEADS = 8
N_KV_HEADS = 2
Q_PER_KV = N_Q_HEADS // N_KV_HEADS
ROPE_THETA = 10000.0
GLA_HEADS = 4
GLA_DK = 128
GLA_DV = 256
GLA_GATE_RANK = 16
GLA_GATE_NORM = 16.0
EPS = 1e-6
LOG2_E = float(np.log2(np.e))

COL_AK, COL_AV, COL_GK, COL_GV = 0, 256, 512, 1024
COL_AQ, COL_GO, COL_GA, COL_GG, COL_GQ = 2048, 3072, 4096, 5120, 6144
N_MAIN = 6656
N_CTX_MAIN = 2048
SUBLANES = 8
LANES = 128
VMEM_BYTES = 64 * 1024 * 1024
VMEM_LIMIT = VMEM_BYTES - 8 * 1024 * 1024

LR_PAD = LANES
TOKEN_TILE = 512
PROJ_CHUNK = 512
PROJ_ROW_SPLIT = 2


def _dot(a, b):
    return jnp.dot(a, b, preferred_element_type=F32)


def _dot_nt(a, b):
    return lax.dot_general(a, b, (((1,), (1,)), ((), ())), preferred_element_type=F32)


def _dot_tn(a, b):
    return lax.dot_general(a, b, (((0,), (0,)), ((), ())), preferred_element_type=F32)


def _sigmoid(x):
    return 0.5 * jnp.tanh(0.5 * x) + 0.5


def _resident(shape, index_map):
    return pl.BlockSpec(shape, index_map, pipeline_mode=pl.Buffered(1))


def _ada_kernel(c_ref, w_ref, b_ref, o_ref):
    c = c_ref[...]
    s = (c * _sigmoid(c)).astype(BF16)
    o_ref[...] = _dot(s, w_ref[...].astype(BF16)) + b_ref[...]


ADA_ROWS = SUBLANES
ADA_COL_TILE = 1536


def _ada(cc, w_ada, b_ada):
    d, n = w_ada.shape
    tn = ADA_COL_TILE
    return pl.pallas_call(
        _ada_kernel,
        grid=(n // tn,),
        in_specs=[pl.BlockSpec((ADA_ROWS, d), lambda j: (0, 0)),
                  pl.BlockSpec((d, tn), lambda j: (0, j)),
                  pl.BlockSpec((1, tn), lambda j: (0, j))],
        out_specs=pl.BlockSpec((ADA_ROWS, tn), lambda j: (0, j)),
        out_shape=jax.ShapeDtypeStruct((ADA_ROWS, n), F32),
        compiler_params=pltpu.CompilerParams(vmem_limit_bytes=VMEM_LIMIT),
        name="ada",
    )(cc, w_ada, b_ada)


W_IN_LR = (2048, 2080)
W_IN_GROUPS = (((0, 2048), 1.0), ((2080, 3104), 1.0), ((3616, 4640), 1.0), ((4640, 5664), 1.0),
               ((5664, 6688), 1.0), ((3104, 3616), GLA_DK ** -0.5))
W_IN_CHUNKS = tuple((lo + o, scale) for (lo, hi), scale in W_IN_GROUPS for o in range(0, hi - lo, PROJ_CHUNK))
assert len(W_IN_CHUNKS) * PROJ_CHUNK == N_MAIN


def _chunk_table(j, column, divisor=1):
    entry = lambda idx: W_IN_CHUNKS[idx][column] // divisor if divisor != 1 else W_IN_CHUNKS[idx][column]
    out = entry(len(W_IN_CHUNKS) - 1)
    for idx in range(len(W_IN_CHUNKS) - 2, -1, -1):
        out = jnp.where(j == idx, entry(idx), out)
    return out


def _regroup_kernel(src_ref, lr_src_ref, main_ref, lr_ref):
    scale = _chunk_table(pl.program_id(0), 1).astype(F32)
    main_ref[...] = (src_ref[...] * scale).T.astype(BF16)
    rank2 = W_IN_LR[1] - W_IN_LR[0]
    lr_rows = jnp.concatenate([lr_src_ref[...], jnp.zeros((LR_PAD - rank2, lr_src_ref.shape[1]), F32)], axis=0)
    lr_ref[...] = lr_rows.T.astype(BF16)


def _regroup_w_in(w_t):
    n_in, d = w_t.shape
    rank2 = W_IN_LR[1] - W_IN_LR[0]
    n_chunks = len(W_IN_CHUNKS)
    return pl.pallas_call(
        _regroup_kernel,
        grid=(len(W_IN_CHUNKS),),
        in_specs=[pl.BlockSpec((pl.Element(PROJ_CHUNK), pl.Element(d)),
                               lambda j: (_chunk_table(j, 0, rank2) * rank2, 0)),
                  pl.BlockSpec((rank2, d), lambda j: (W_IN_LR[0] // rank2, 0))],
        out_specs=[pl.BlockSpec((None, d, PROJ_CHUNK), lambda j: (j, 0, 0)),
                   pl.BlockSpec((d, LR_PAD), lambda j: (0, 0))],
        out_shape=[jax.ShapeDtypeStruct((n_chunks, d, PROJ_CHUNK), BF16),
                   jax.ShapeDtypeStruct((d, LR_PAD), BF16)],
        compiler_params=pltpu.CompilerParams(vmem_limit_bytes=VMEM_LIMIT),
        name="regroup_w_in",
    )(w_t, w_t)


def _head_norm(a, gain):
    ms = jnp.mean(a * a, axis=-1, keepdims=True)
    return a * lax.rsqrt(ms + EPS) * gain


def _rope(n, cos, sin_signed):
    lane = lax.broadcasted_iota(jnp.int32, n.shape, 1)
    partner = jnp.where((lane % 64) < 32, pltpu.roll(n, 96, 1), pltpu.roll(n, 32, 1))
    return n * cos + partner * sin_signed


def _log2_decay(lowrank, w, b):
    z = _dot(lowrank.astype(BF16), w) + b
    return (jnp.minimum(z, 0.0) - jnp.log(1.0 + jnp.exp(-jnp.abs(z)))) * (LOG2_E / GLA_GATE_NORM)


def _inproj_kernel(x_ref, sh_ref, sc_ref, n1_ref, w_ref, wlr_ref, wgk_ref, bgk_ref, cos_ref, sin_ref,
                   qg_ref, kg_ref, main_ref, g_ref, *, n_cols, rope):
    half = x_ref.shape[0] // PROJ_ROW_SPLIT
    g_slices = min(8, n_cols // PROJ_CHUNK)
    g_width = g_ref.shape[1] // g_slices
    for part in range(PROJ_ROW_SPLIT):
        r = slice(part * half, (part + 1) * half)
        x = x_ref[r, :]
        ms = jnp.mean(x * x, axis=-1, keepdims=True)
        h = x * lax.rsqrt(ms + EPS) * n1_ref[...]
        hb = (h * (1.0 + sc_ref[...]) + sh_ref[...]).astype(BF16)

        def qk_head(a, gain):
            n = _head_norm(a, gain)
            if rope:
                n = _rope(n, cos_ref[r, :], sin_ref[r, :])
            return n.astype(BF16)

        for c in range(n_cols // PROJ_CHUNK):
            lo = c * PROJ_CHUNK
            acc = _dot(hb, w_ref[c])
            for s in range(PROJ_CHUNK // HEAD_DIM):
                col = lo + s * HEAD_DIM
                a = acc[:, s * HEAD_DIM:(s + 1) * HEAD_DIM]
                if COL_AK <= col < COL_AV:
                    main_ref[r, col:col + HEAD_DIM] = qk_head(a, kg_ref[...])
                elif COL_AQ <= col < COL_GO:
                    main_ref[r, col:col + HEAD_DIM] = qk_head(a, qg_ref[...])
                elif COL_GO <= col < COL_GA:
                    main_ref[r, col:col + HEAD_DIM] = (a * _sigmoid(a)).astype(BF16)
                elif COL_GA <= col < COL_GQ:
                    main_ref[r, col:col + HEAD_DIM] = _sigmoid(a).astype(BF16)
                else:
                    main_ref[r, col:col + HEAD_DIM] = a.astype(BF16)
            if c < g_slices:
                if c == 0:
                    lowrank = _dot(hb, wlr_ref[...])
                gs = slice(c * g_width, (c + 1) * g_width)
                g_ref[r, gs] = _log2_decay(lowrank, wgk_ref[:, gs], bgk_ref[:, gs]).astype(BF16)


def _inproj(x2d, mod4, mod_row_of_tile, norm1, w_main, w_lr, w_gk, b_gk, cos_t, sin_t, q_gain, k_gain,
            *, n_cols, rope, tm, seq_tiles):
    rows, d = x2d.shape
    n_g = w_gk.shape[1]
    kern = functools.partial(_inproj_kernel, n_cols=n_cols, rope=rope)
    return pl.pallas_call(
        kern,
        grid=(rows // tm,),
        in_specs=[
            pl.BlockSpec((tm, d), lambda i: (i, 0)),
            pl.BlockSpec((None, None, 1, d), lambda i: (mod_row_of_tile(i), 0, 0, 0)),
            pl.BlockSpec((None, None, 1, d), lambda i: (mod_row_of_tile(i), 1, 0, 0)),
            _resident((1, d), lambda i: (0, 0)),
            _resident((n_cols // PROJ_CHUNK, d, PROJ_CHUNK), lambda i: (0, 0, 0)),
            _resident((d, LR_PAD), lambda i: (0, 0)),
            _resident((LR_PAD, n_g), lambda i: (0, 0)),
            _resident((1, n_g), lambda i: (0, 0)),
            pl.BlockSpec((tm, HEAD_DIM), lambda i: (i % seq_tiles, 0)),
            pl.BlockSpec((tm, HEAD_DIM), lambda i: (i % seq_tiles, 0)),
            _resident((1, HEAD_DIM), lambda i: (0, 0)),
            _resident((1, HEAD_DIM), lambda i: (0, 0)),
        ],
        out_specs=[pl.BlockSpec((tm, n_cols), lambda i: (i, 0)),
                   pl.BlockSpec((tm, n_g), lambda i: (i, 0))],
        out_shape=[jax.ShapeDtypeStruct((rows, n_cols), BF16),
                   jax.ShapeDtypeStruct((rows, n_g), BF16)],
        compiler_params=pltpu.CompilerParams(vmem_limit_bytes=VMEM_LIMIT),
        name="inproj_rope" if rope else "inproj_ctx",
    )(x2d, mod4, mod4, norm1, w_main, w_lr, w_gk, b_gk, cos_t, sin_t, q_gain, k_gain)


def _tri(n, upper):
    r = lax.broadcasted_iota(jnp.int32, (n, n), 0)
    c = lax.broadcasted_iota(jnp.int32, (n, n), 1)
    return (c >= r) if upper else (r >= c)


def _running_sums(tri, g):
    return _dot(tri.astype(BF16), g)


def _glastate_kernel(k_ref, v_ref, gf_ref, gb_ref, sf_ref, sb_ref, *, n):
    k = k_ref[...].astype(F32)
    b = _running_sums(_tri(n, False), gf_ref[...])
    k_fwd = (k * jnp.exp2(b[n - 1:n] - b)).astype(BF16)
    e = _running_sums(_tri(n, True), gb_ref[...])
    k_bwd = (k * jnp.exp2(e[0:1] - e)).astype(BF16)
    for h in range(GLA_HEADS):
        v = v_ref[:, h * GLA_DV:(h + 1) * GLA_DV]
        kc = slice(h * GLA_DK, (h + 1) * GLA_DK)
        sf_ref[h] = _dot_tn(v, k_fwd[:, kc])
        sb_ref[h] = _dot_tn(v, k_bwd[:, kc])


def _glastate(ctx_main, ctx_g, *, batch, n_ctx):
    kw, vw = GLA_HEADS * GLA_DK, GLA_HEADS * GLA_DV
    st = jax.ShapeDtypeStruct((batch, GLA_HEADS, GLA_DV, GLA_DK), F32)
    st_spec = pl.BlockSpec((None, GLA_HEADS, GLA_DV, GLA_DK), lambda b: (b, 0, 0, 0))
    return pl.pallas_call(
        functools.partial(_glastate_kernel, n=n_ctx),
        grid=(batch,),
        in_specs=[pl.BlockSpec((n_ctx, kw), lambda b: (b, COL_GK // kw)),
                  pl.BlockSpec((n_ctx, vw), lambda b: (b, COL_GV // vw)),
                  pl.BlockSpec((n_ctx, kw), lambda b: (b, 0)),
                  pl.BlockSpec((n_ctx, kw), lambda b: (b, 1))],
        out_specs=[st_spec, st_spec],
        out_shape=[st, st],
        compiler_params=pltpu.CompilerParams(vmem_limit_bytes=VMEM_LIMIT),
        name="gla_ctx_state",
    )(ctx_main, ctx_main, ctx_g, ctx_g)


GLA_CHUNK = 64
GLA_TBLOCK = 4096
GLA_WAVE = 4


def _gla_kernel(qf_ref, kf_ref, vf_ref, gf_ref, qb_ref, kb_ref, vb_ref, gb_ref,
                sf0_ref, sb0_ref, of_ref, ob_ref, stf, stb):
    @pl.when(pl.program_id(2) == 0)
    def _():
        stf[...] = sf0_ref[...]
        stb[...] = sb0_ref[...]

    c, n = GLA_CHUNK, qf_ref.shape[0] // GLA_CHUNK
    streams = ((gf_ref, qf_ref, kf_ref, vf_ref, of_ref, False),
               (gb_ref, qb_ref, kb_ref, vb_ref, ob_ref, True))
    tri, rows, q, k, v, b, b_mid, b_edge = [], [], [], [], [], [], [], []
    for g_ref, q_ref, k_ref, v_ref, _, upper in streams:
        t = _tri(c, upper)
        edge = 0 if upper else c - 1
        bw = _running_sums(t, jnp.concatenate([g_ref[i * c:(i + 1) * c, :] for i in range(n)], axis=1))
        for i in range(n):
            r = slice(i * c, (i + 1) * c)
            x = bw[:, i * GLA_DK:(i + 1) * GLA_DK]
            tri.append(t), rows.append(r), b.append(x)
            b_mid.append(x[c // 2:c // 2 + 1]), b_edge.append(x[edge:edge + 1])
            q.append(q_ref[r, :]), k.append(k_ref[r, :]), v.append(v_ref[r, :])
    st = [stf[...], stb[...]]
    for w0 in range(0, n, GLA_WAVE):
        u = [(d, d * n + (n - 1 - step if streams[d][5] else step))
             for step in range(w0, w0 + GLA_WAVE) for d in range(len(streams))]
        qt = {j: q[j].astype(F32) * jnp.exp2(b[j] - b_mid[j]) for _, j in u}
        kt = {j: k[j].astype(F32) * jnp.exp2(b_mid[j] - b[j]) for _, j in u}
        s = {j: _dot_nt(qt[j].astype(BF16), kt[j].astype(BF16)) for _, j in u}
        a = {j: jnp.where(tri[j], s[j], 0.0).astype(BF16) for _, j in u}
        k_out = {j: (kt[j] * jnp.exp2(b_edge[j] - b_mid[j])).astype(BF16) for _, j in u}
        upd = {j: _dot_tn(v[j], k_out[j]) for _, j in u}
        intra = {j: _dot(a[j], v[j]) for _, j in u}
        q_in = {j: (qt[j] * jnp.exp2(b_mid[j])).astype(BF16) for _, j in u}
        for d, j in u:
            streams[d][4][rows[j], :] = (intra[j] + _dot_nt(q_in[j], st[d].astype(BF16))).astype(BF16)
            st[d] = st[d] * jnp.exp2(b_edge[j]) + upd[j]
    stf[...] = st[0]
    stb[...] = st[1]


def _gla_kernel_one_block(q_ref, k_ref, v_ref, gf_ref, gb_ref, sf0_ref, sb0_ref, of_ref, ob_ref, stf, stb):
    _gla_kernel(q_ref, k_ref, v_ref, gf_ref, q_ref, k_ref, v_ref, gb_ref, sf0_ref, sb0_ref, of_ref, ob_ref, stf, stb)


def _gla(main, g, sf0, sb0, *, batch, seq):
    tb = min(GLA_TBLOCK, seq)
    nt = seq // tb
    qb_, kb_, vb_ = COL_GQ // GLA_DK, COL_GK // GLA_DK, COL_GV // GLA_DV

    def fwd(col):
        return lambda b, h, t: (b * nt + t, col + h)

    def bwd(col):
        return lambda b, h, t: (b * nt + nt - 1 - t, col + h)

    def seq_specs(m, g_col):
        return [pl.BlockSpec((tb, GLA_DK), m(qb_)), pl.BlockSpec((tb, GLA_DK), m(kb_)),
                pl.BlockSpec((tb, GLA_DV), m(vb_)), pl.BlockSpec((tb, GLA_DK), m(g_col))]

    st_spec = pl.BlockSpec((None, None, GLA_DV, GLA_DK), lambda b, h, t: (b, h, 0, 0))
    out = jax.ShapeDtypeStruct((batch * seq, GLA_HEADS * GLA_DV), BF16)
    if nt == 1:
        kern = _gla_kernel_one_block
        in_specs = seq_specs(fwd, 0) + [pl.BlockSpec((tb, GLA_DK), bwd(GLA_HEADS)), st_spec, st_spec]
        operands = (main, main, main, g, g, sf0, sb0)
    else:
        kern = _gla_kernel
        in_specs = seq_specs(fwd, 0) + seq_specs(bwd, GLA_HEADS) + [st_spec, st_spec]
        operands = (main, main, main, g, main, main, main, g, sf0, sb0)
    return pl.pallas_call(
        kern,
        grid=(batch, GLA_HEADS, nt),
        in_specs=in_specs,
        out_specs=[pl.BlockSpec((tb, GLA_DV), fwd(0)), pl.BlockSpec((tb, GLA_DV), bwd(0))],
        out_shape=[out, out],
        scratch_shapes=[pltpu.VMEM((GLA_DV, GLA_DK), F32), pltpu.VMEM((GLA_DV, GLA_DK), F32)],
        compiler_params=pltpu.CompilerParams(
            dimension_semantics=("arbitrary", "arbitrary", "arbitrary"), vmem_limit_bytes=VMEM_LIMIT),
        name="gla",
    )(*operands)


ATT_TQ = 1024
ATT_TK = 512


ATT_UNSHIFTED_MAX_LOG2 = 56.0


def _attn_kernel(bound_ref, q_ref, k_ref, v_ref, kc_ref, vc_ref, o_ref, vt_sc, vct_sc, m_sc, l_sc, acc_sc, *, seq):
    tq, tk, hd = ATT_TQ, ATT_TK, HEAD_DIM
    nk = seq // tk

    def transposed(v):
        return v.astype(F32).T.astype(BF16)

    @pl.when(pl.program_id(2) == 0)
    def _():
        for j in range(nk):
            vt_sc[j] = transposed(v_ref[j * tk:(j + 1) * tk, :])
        vct_sc[...] = transposed(vc_ref[...])

    q4 = jnp.concatenate([q_ref[:, g * hd:(g + 1) * hd] for g in range(Q_PER_KV)], axis=0)

    def key_sums(e):
        return jnp.sum(e.reshape(e.shape[0] // SUBLANES, SUBLANES, e.shape[1]), axis=0)

    def finish(acc, l):
        o = (acc / jnp.sum(l, axis=0, keepdims=True)).T
        for g in range(Q_PER_KV):
            o_ref[:, g * hd:(g + 1) * hd] = o[g * tq:(g + 1) * tq].astype(BF16)

    small = bound_ref[0, 0] <= ATT_UNSHIFTED_MAX_LOG2

    @pl.when(small)
    def _():
        chunks = [(k_ref[j * tk:(j + 1) * tk, :], vt_sc[j]) for j in range(nk)] + [(kc_ref[...], vct_sc[...])]
        acc = l = None
        for kc, vt in chunks:
            e = jnp.exp2(_dot_nt(kc, q4))
            pv, ks = _dot(vt, e.astype(BF16)), key_sums(e)
            acc, l = (pv, ks) if acc is None else (acc + pv, l + ks)
        finish(acc, l)

    @pl.when(jnp.logical_not(small))
    def _():
        m_sc[...] = jnp.full(m_sc.shape, -jnp.inf, F32)
        acc_sc[...] = jnp.zeros(acc_sc.shape, F32)
        l_sc[...] = jnp.zeros(l_sc.shape, F32)

        def online_step(kc, vt):
            s = _dot_nt(kc, q4)
            m_prev = m_sc[...]
            m_new = jnp.maximum(m_prev, jnp.max(s, axis=0, keepdims=True))
            e = jnp.exp2(s - m_new)
            rescale = jnp.exp2(m_prev - m_new)
            acc_sc[...] = rescale * acc_sc[...] + _dot(vt, e.astype(BF16))
            l_sc[...] = rescale * l_sc[...] + key_sums(e)
            m_sc[...] = m_new

        def body(j, carry):
            online_step(k_ref[pl.ds(pl.multiple_of(j * tk, tk), tk), :], vt_sc[j])
            return carry

        lax.fori_loop(0, nk, body, 0)
        online_step(kc_ref[...], vct_sc[...])
        finish(acc_sc[...], l_sc[...])


def _attn(score_bound, main, ctx_main, *, batch, seq, n_ctx):
    nq = seq // ATT_TQ
    gw = Q_PER_KV * HEAD_DIM
    rows = Q_PER_KV * ATT_TQ
    vt_rows = HEAD_DIM
    return pl.pallas_call(
        functools.partial(_attn_kernel, seq=seq),
        grid=(batch, N_KV_HEADS, nq),
        in_specs=[pl.BlockSpec(memory_space=pltpu.SMEM),
                  pl.BlockSpec((ATT_TQ, gw), lambda b, h, i: (b * nq + i, COL_AQ // gw + h)),
                  pl.BlockSpec((seq, HEAD_DIM), lambda b, h, i: (b, COL_AK // HEAD_DIM + h)),
                  pl.BlockSpec((seq, HEAD_DIM), lambda b, h, i: (b, COL_AV // HEAD_DIM + h)),
                  pl.BlockSpec((n_ctx, HEAD_DIM), lambda b, h, i: (b, COL_AK // HEAD_DIM + h)),
                  pl.BlockSpec((n_ctx, HEAD_DIM), lambda b, h, i: (b, COL_AV // HEAD_DIM + h))],
        out_specs=pl.BlockSpec((ATT_TQ, gw), lambda b, h, i: (b * nq + i, h)),
        out_shape=jax.ShapeDtypeStruct((batch * seq, N_Q_HEADS * HEAD_DIM), BF16),
        scratch_shapes=[pltpu.VMEM((seq // ATT_TK, vt_rows, ATT_TK), BF16),
                        pltpu.VMEM((vt_rows, n_ctx), BF16),
                        pltpu.VMEM((1, rows), F32), pltpu.VMEM((SUBLANES, rows), F32),
                        pltpu.VMEM((vt_rows, rows), F32)],
        compiler_params=pltpu.CompilerParams(
            dimension_semantics=("arbitrary", "arbitrary", "arbitrary"), vmem_limit_bytes=VMEM_LIMIT),
        name="attn",
    )(score_bound, main, main, main, ctx_main, ctx_main)


FF_CHUNK = 1024


def _tail_kernel(x_ref, attn_ref, of_ref, ob_ref, go_ref, ga_ref, gg_ref, gn_ref, gt1_ref,
                 wa_ref, wg_ref, wo_ref, sh_ref, sc_ref, gt2_ref, n2_ref, w1_ref, w2_ref, o_ref, *, d_ff):
    gn = gn_ref[...]
    heads = []
    for h in range(GLA_HEADS):
        s = slice(h * GLA_DV, (h + 1) * GLA_DV)
        o = of_ref[:, s].astype(F32) + ob_ref[:, s].astype(F32)
        heads.append((_head_norm(o, gn) * go_ref[:, s].astype(F32)).astype(BF16))
    gla = jnp.concatenate(heads, axis=-1)
    ya = ga_ref[...].astype(F32) * _dot(attn_ref[...], wa_ref[...])
    yg = gg_ref[...].astype(F32) * _dot(gla, wg_ref[...])
    x1 = x_ref[...] + gt1_ref[...] * _dot((ya + yg).astype(BF16), wo_ref[...])
    ms = jnp.mean(x1 * x1, axis=-1, keepdims=True)
    h2 = x1 * lax.rsqrt(ms + EPS) * n2_ref[...]
    hb = (h2 * (1.0 + sc_ref[...]) + sh_ref[...]).astype(BF16)
    acc = jnp.zeros(x1.shape, F32)
    for c in range(d_ff // FF_CHUNK):
        s = slice(c * FF_CHUNK, (c + 1) * FF_CHUNK)
        u = jnp.maximum(_dot(hb, w1_ref[:, s]), 0.0)
        acc = acc + _dot((u * u).astype(BF16), w2_ref[s, :])
    o_ref[...] = x1 + gt2_ref[...] * acc


def _tail(x2d, attn, o_f, o_b, main, gla_norm, mod4, wa, wg, wo, norm2, w1, w2, *, tm, seq_tiles):
    rows, d = x2d.shape
    d_ff = w1.shape[1]
    tile = pl.BlockSpec((tm, d), lambda i: (i, 0))
    mod = lambda g: pl.BlockSpec((None, None, 1, d), lambda i: (i // seq_tiles, g, 0, 0))
    const = lambda shape: _resident(shape, lambda i: (0, 0))
    return pl.pallas_call(
        functools.partial(_tail_kernel, d_ff=d_ff),
        grid=(rows // tm,),
        in_specs=[tile, tile, tile, tile,
                  pl.BlockSpec((tm, d), lambda i: (i, COL_GO // d)),
                  pl.BlockSpec((tm, d), lambda i: (i, COL_GA // d)),
                  pl.BlockSpec((tm, d), lambda i: (i, COL_GG // d)),
                  const((1, GLA_DV)), mod(2), const((d, d)), const((d, d)), const((d, d)),
                  mod(3), mod(4), mod(5), const((1, d)), const((d, d_ff)), const((d_ff, d))],
        out_specs=tile,
        out_shape=jax.ShapeDtypeStruct((rows, d), F32),
        compiler_params=pltpu.CompilerParams(vmem_limit_bytes=VMEM_LIMIT),
        name="merge_out_mlp",
    )(x2d, attn, o_f, o_b, main, main, main, gla_norm, mod4, wa, wg, wo, mod4, mod4, mod4, norm2, w1, w2)


def _rope_tables(seq):
    t = np.arange(seq)
    half = HEAD_DIM // 2
    freqs = ROPE_THETA ** (-np.arange(0, half, 2, dtype=np.float32) / half)
    ang_r = (t // GRID_W).astype(np.float32)[:, None] * freqs
    ang_c = (t % GRID_W).astype(np.float32)[:, None] * freqs
    cos = np.concatenate([np.cos(ang_r)] * 2 + [np.cos(ang_c)] * 2, axis=-1)
    sin = np.concatenate([-np.sin(ang_r), np.sin(ang_r), -np.sin(ang_c), np.sin(ang_c)], axis=-1)
    return jnp.asarray(cos, F32), jnp.asarray(sin, F32)


def _pad_rows(w, row0):
    return jnp.pad(w, ((row0, LR_PAD - row0 - w.shape[0]), (0, 0)))


def kernel(x, c, ctx, c_ctx, w_ada, b_ada, norm1, w_in, q_norm, k_norm, w_gk_fwd, b_gk_fwd,
           w_gk_bwd, b_gk_bwd, gla_norm, w_br_attn, w_br_gla, w_out, norm2, w_mlp1, w_mlp2):
    batch, seq, d = x.shape
    n_ctx = ctx.shape[1]
    depth = w_ada.shape[0]
    assert depth == 1 and batch < ADA_ROWS
    assert seq % min(GLA_TBLOCK, seq) == 0 and seq % ATT_TK == 0 and seq % ATT_TQ == 0 and seq % TOKEN_TILE == 0
    tm = TOKEN_TILE
    seq_tiles = seq // tm
    ctx_tm = n_ctx
    l = 0

    cc = jnp.zeros((ADA_ROWS, d), F32).at[:batch].set(c).at[batch].set(c_ctx)
    mod4 = _ada(cc, w_ada[l], b_ada[l][None, :]).reshape(ADA_ROWS, 6, 1, d)

    w_main, w_lr = _regroup_w_in(jnp.transpose(w_in[l]))
    cos_t, sin_t = _rope_tables(seq)
    q_scale = (HEAD_DIM ** -0.5) * float(np.log2(np.e))
    q_gain = (q_norm[l] * q_scale)[None, :]
    k_gain = k_norm[l][None, :]
    score_bound = (1.01 * HEAD_DIM * q_scale * jnp.max(jnp.abs(q_norm[l])) * jnp.max(jnp.abs(k_norm[l])))
    score_bound = score_bound.reshape(1, 1).astype(F32)
    n1 = norm1[l][None, :]

    w_gk = jnp.concatenate([_pad_rows(w_gk_fwd[l], 0), _pad_rows(w_gk_bwd[l], GLA_GATE_RANK)], axis=1).astype(BF16)
    b_gk = jnp.concatenate([b_gk_fwd[l], b_gk_bwd[l]])[None, :]

    main, g = _inproj(x.reshape(batch * seq, d), mod4, lambda i: i // seq_tiles, n1, w_main, w_lr, w_gk, b_gk,
                      cos_t, sin_t, q_gain, k_gain, n_cols=N_MAIN, rope=True, tm=tm, seq_tiles=seq_tiles)
    ctx_main, ctx_g = _inproj(ctx.reshape(batch * n_ctx, d), mod4, lambda i: batch, n1, w_main, w_lr, w_gk, b_gk,
                              cos_t, sin_t, q_gain, k_gain, n_cols=N_CTX_MAIN, rope=False, tm=ctx_tm,
                              seq_tiles=1)

    sf0, sb0 = _glastate(ctx_main, ctx_g, batch=batch, n_ctx=n_ctx)
    o_f, o_b = _gla(main, g, sf0, sb0, batch=batch, seq=seq)

    attn = _attn(score_bound, main, ctx_main, batch=batch, seq=seq, n_ctx=n_ctx)

    x2 = _tail(x.reshape(batch * seq, d), attn, o_f, o_b, main, gla_norm[l][None, :], mod4,
               w_br_attn[l].astype(BF16), w_br_gla[l].astype(BF16), w_out[l].astype(BF16),
               norm2[l][None, :], w_mlp1[l].astype(BF16), w_mlp2[l].astype(BF16),
               tm=tm, seq_tiles=seq_tiles)
    return x2.reshape(batch, seq, d)
```

```python
import functools

import numpy as np
import jax
import jax.numpy as jnp
from jax import lax
from jax.experimental import pallas as pl
from jax.experimental.pallas import tpu as pltpu

F32 = jnp.float32
BF16 = jnp.bfloat16

GRID_W = 64
HEAD_DIM = 128
N_Q_HEADS = 8
N_KV_HEADS = 2
Q_PER_KV = N_Q_HEADS // N_KV_HEADS
ROPE_THETA = 10000.0
GLA_HEADS = 4
GLA_DK = 128
GLA_DV = 256
GLA_GATE_RANK = 16
GLA_GATE_NORM = 16.0
EPS = 1e-6
LOG2_E = float(np.log2(np.e))

COL_AK, COL_AV, COL_GK, COL_GV = 0, 256, 512, 1024
COL_AQ, COL_GO, COL_GA, COL_GG, COL_GQ = 2048, 3072, 4096, 5120, 6144
N_MAIN = 6656
N_CTX_MAIN = 2048
SUBLANES = 8
LANES = 128
VMEM_BYTES = 64 * 1024 * 1024
VMEM_LIMIT = VMEM_BYTES - 8 * 1024 * 1024

LR_PAD = LANES
TOKEN_TILE = 512
PROJ_CHUNK = 512
PROJ_ROW_SPLIT = 2


def _dot(a, b):
    return jnp.dot(a, b, preferred_element_type=F32)


def _dot_nt(a, b):
    return lax.dot_general(a, b, (((1,), (1,)), ((), ())), preferred_element_type=F32)


def _dot_tn(a, b):
    return lax.dot_general(a, b, (((0,), (0,)), ((), ())), preferred_element_type=F32)


def _sigmoid(x):
    return 0.5 * jnp.tanh(0.5 * x) + 0.5


def _resident(shape, index_map):
    return pl.BlockSpec(shape, index_map, pipeline_mode=pl.Buffered(1))


def _ada_kernel(c_ref, w_ref, b_ref, o_ref):
    c = c_ref[...]
    s = (c * _sigmoid(c)).astype(BF16)
    o_ref[...] = _dot(s, w_ref[...].astype(BF16)) + b_ref[...]


ADA_ROWS = SUBLANES
ADA_COL_TILE = 1536


def _ada(cc, w_ada, b_ada):
    d, n = w_ada.shape
    tn = ADA_COL_TILE
    return pl.pallas_call(
        _ada_kernel,
        grid=(n // tn,),
        in_specs=[pl.BlockSpec((ADA_ROWS, d), lambda j: (0, 0)),
                  pl.BlockSpec((d, tn), lambda j: (0, j)),
                  pl.BlockSpec((1, tn), lambda j: (0, j))],
        out_specs=pl.BlockSpec((ADA_ROWS, tn), lambda j: (0, j)),
        out_shape=jax.ShapeDtypeStruct((ADA_ROWS, n), F32),
        compiler_params=pltpu.CompilerParams(vmem_limit_bytes=VMEM_LIMIT),
        name="ada",
    )(cc, w_ada, b_ada)


W_IN_LR = (2048, 2080)
W_IN_GROUPS = (((0, 2048), 1.0), ((2080, 3104), 1.0), ((3616, 4640), 1.0), ((4640, 5664), 1.0),
               ((5664, 6688), 1.0), ((3104, 3616), GLA_DK ** -0.5))
W_IN_CHUNKS = tuple((lo + o, scale) for (lo, hi), scale in W_IN_GROUPS for o in range(0, hi - lo, PROJ_CHUNK))
assert len(W_IN_CHUNKS) * PROJ_CHUNK == N_MAIN


def _chunk_table(j, column, divisor=1):
    entry = lambda idx: W_IN_CHUNKS[idx][column] // divisor if divisor != 1 else W_IN_CHUNKS[idx][column]
    out = entry(len(W_IN_CHUNKS) - 1)
    for idx in range(len(W_IN_CHUNKS) - 2, -1, -1):
        out = jnp.where(j == idx, entry(idx), out)
    return out


def _regroup_kernel(src_ref, lr_src_ref, main_ref, lr_ref):
    scale = _chunk_table(pl.program_id(0), 1).astype(F32)
    main_ref[...] = (src_ref[...] * scale).T.astype(BF16)
    rank2 = W_IN_LR[1] - W_IN_LR[0]
    lr_rows = jnp.concatenate([lr_src_ref[...], jnp.zeros((LR_PAD - rank2, lr_src_ref.shape[1]), F32)], axis=0)
    lr_ref[...] = lr_rows.T.astype(BF16)


def _regroup_w_in(w_t):
    n_in, d = w_t.shape
    rank2 = W_IN_LR[1] - W_IN_LR[0]
    n_chunks = len(W_IN_CHUNKS)
    return pl.pallas_call(
        _regroup_kernel,
        grid=(len(W_IN_CHUNKS),),
        in_specs=[pl.BlockSpec((pl.Element(PROJ_CHUNK), pl.Element(d)),
                               lambda j: (_chunk_table(j, 0, rank2) * rank2, 0)),
                  pl.BlockSpec((rank2, d), lambda j: (W_IN_LR[0] // rank2, 0))],
        out_specs=[pl.BlockSpec((None, d, PROJ_CHUNK), lambda j: (j, 0, 0)),
                   pl.BlockSpec((d, LR_PAD), lambda j: (0, 0))],
        out_shape=[jax.ShapeDtypeStruct((n_chunks, d, PROJ_CHUNK), BF16),
                   jax.ShapeDtypeStruct((d, LR_PAD), BF16)],
        compiler_params=pltpu.CompilerParams(vmem_limit_bytes=VMEM_LIMIT),
        name="regroup_w_in",
    )(w_t, w_t)


def _head_norm(a, gain):
    ms = jnp.mean(a * a, axis=-1, keepdims=True)
    return a * lax.rsqrt(ms + EPS) * gain


def _rope(n, cos, sin_signed):
    lane = lax.broadcasted_iota(jnp.int32, n.shape, 1)
    partner = jnp.where((lane % 64) < 32, pltpu.roll(n, 96, 1), pltpu.roll(n, 32, 1))
    return n * cos + partner * sin_signed


def _log2_decay(lowrank, w, b):
    z = _dot(lowrank.astype(BF16), w) + b
    return (jnp.minimum(z, 0.0) - jnp.log(1.0 + jnp.exp(-jnp.abs(z)))) * (LOG2_E / GLA_GATE_NORM)


def _inproj_kernel(x_ref, sh_ref, sc_ref, n1_ref, w_ref, wlr_ref, wgk_ref, bgk_ref, cos_ref, sin_ref,
                   qg_ref, kg_ref, main_ref, g_ref, *, n_cols, rope):
    half = x_ref.shape[0] // PROJ_ROW_SPLIT
    g_slices = min(8, n_cols // PROJ_CHUNK)
    g_width = g_ref.shape[1] // g_slices
    for part in range(PROJ_ROW_SPLIT):
        r = slice(part * half, (part + 1) * half)
        x = x_ref[r, :]
        ms = jnp.mean(x * x, axis=-1, keepdims=True)
        h = x * lax.rsqrt(ms + EPS) * n1_ref[...]
        hb = (h * (1.0 + sc_ref[...]) + sh_ref[...]).astype(BF16)

        def qk_head(a, gain):
            n = _head_norm(a, gain)
            if rope:
                n = _rope(n, cos_ref[r, :], sin_ref[r, :])
            return n.astype(BF16)

        for c in range(n_cols // PROJ_CHUNK):
            lo = c * PROJ_CHUNK
            acc = _dot(hb, w_ref[c])
            for s in range(PROJ_CHUNK // HEAD_DIM):
                col = lo + s * HEAD_DIM
                a = acc[:, s * HEAD_DIM:(s + 1) * HEAD_DIM]
                if COL_AK <= col < COL_AV:
                    main_ref[r, col:col + HEAD_DIM] = qk_head(a, kg_ref[...])
                elif COL_AQ <= col < COL_GO:
                    main_ref[r, col:col + HEAD_DIM] = qk_head(a, qg_ref[...])
                elif COL_GO <= col < COL_GA:
                    main_ref[r, col:col + HEAD_DIM] = (a * _sigmoid(a)).astype(BF16)
                elif COL_GA <= col < COL_GQ:
                    main_ref[r, col:col + HEAD_DIM] = _sigmoid(a).astype(BF16)
                else:
                    main_ref[r, col:col + HEAD_DIM] = a.astype(BF16)
            if c < g_slices:
                if c == 0:
                    lowrank = _dot(hb, wlr_ref[...])
                gs = slice(c * g_width, (c + 1) * g_width)
                g_ref[r, gs] = _log2_decay(lowrank, wgk_ref[:, gs], bgk_ref[:, gs]).astype(BF16)


def _inproj(x2d, mod4, mod_row_of_tile, norm1, w_main, w_lr, w_gk, b_gk, cos_t, sin_t, q_gain, k_gain,
            *, n_cols, rope, tm, seq_tiles):
    rows, d = x2d.shape
    n_g = w_gk.shape[1]
    kern = functools.partial(_inproj_kernel, n_cols=n_cols, rope=rope)
    return pl.pallas_call(
        kern,
        grid=(rows // tm,),
        in_specs=[
            pl.BlockSpec((tm, d), lambda i: (i, 0)),
            pl.BlockSpec((None, None, 1, d), lambda i: (mod_row_of_tile(i), 0, 0, 0)),
            pl.BlockSpec((None, None, 1, d), lambda i: (mod_row_of_tile(i), 1, 0, 0)),
            _resident((1, d), lambda i: (0, 0)),
            _resident((n_cols // PROJ_CHUNK, d, PROJ_CHUNK), lambda i: (0, 0, 0)),
            _resident((d, LR_PAD), lambda i: (0, 0)),
            _resident((LR_PAD, n_g), lambda i: (0, 0)),
            _resident((1, n_g), lambda i: (0, 0)),
            pl.BlockSpec((tm, HEAD_DIM), lambda i: (i % seq_tiles, 0)),
            pl.BlockSpec((tm, HEAD_DIM), lambda i: (i % seq_tiles, 0)),
            _resident((1, HEAD_DIM), lambda i: (0, 0)),
            _resident((1, HEAD_DIM), lambda i: (0, 0)),
        ],
        out_specs=[pl.BlockSpec((tm, n_cols), lambda i: (i, 0)),
                   pl.BlockSpec((tm, n_g), lambda i: (i, 0))],
        out_shape=[jax.ShapeDtypeStruct((rows, n_cols), BF16),
                   jax.ShapeDtypeStruct((rows, n_g), BF16)],
        compiler_params=pltpu.CompilerParams(vmem_limit_bytes=VMEM_LIMIT),
        name="inproj_rope" if rope else "inproj_ctx",
    )(x2d, mod4, mod4, norm1, w_main, w_lr, w_gk, b_gk, cos_t, sin_t, q_gain, k_gain)


def _tri(n, upper):
    r = lax.broadcasted_iota(jnp.int32, (n, n), 0)
    c = lax.broadcasted_iota(jnp.int32, (n, n), 1)
    return (c >= r) if upper else (r >= c)


def _running_sums(tri, g):
    return _dot(tri.astype(BF16), g)


def _glastate_kernel(k_ref, v_ref, gf_ref, gb_ref, sf_ref, sb_ref, *, n):
    k = k_ref[...].astype(F32)
    b = _running_sums(_tri(n, False), gf_ref[...])
    k_fwd = (k * jnp.exp2(b[n - 1:n] - b)).astype(BF16)
    e = _running_sums(_tri(n, True), gb_ref[...])
    k_bwd = (k * jnp.exp2(e[0:1] - e)).astype(BF16)
    for h in range(GLA_HEADS):
        v = v_ref[:, h * GLA_DV:(h + 1) * GLA_DV]
        kc = slice(h * GLA_DK, (h + 1) * GLA_DK)
        sf_ref[h] = _dot_tn(v, k_fwd[:, kc])
        sb_ref[h] = _dot_tn(v, k_bwd[:, kc])


def _glastate(ctx_main, ctx_g, *, batch, n_ctx):
    kw, vw = GLA_HEADS * GLA_DK, GLA_HEADS * GLA_DV
    st = jax.ShapeDtypeStruct((batch, GLA_HEADS, GLA_DV, GLA_DK), F32)
    st_spec = pl.BlockSpec((None, GLA_HEADS, GLA_DV, GLA_DK), lambda b: (b, 0, 0, 0))
    return pl.pallas_call(
        functools.partial(_glastate_kernel, n=n_ctx),
        grid=(batch,),
        in_specs=[pl.BlockSpec((n_ctx, kw), lambda b: (b, COL_GK // kw)),
                  pl.BlockSpec((n_ctx, vw), lambda b: (b, COL_GV // vw)),
                  pl.BlockSpec((n_ctx, kw), lambda b: (b, 0)),
                  pl.BlockSpec((n_ctx, kw), lambda b: (b, 1))],
        out_specs=[st_spec, st_spec],
        out_shape=[st, st],
        compiler_params=pltpu.CompilerParams(vmem_limit_bytes=VMEM_LIMIT),
        name="gla_ctx_state",
    )(ctx_main, ctx_main, ctx_g, ctx_g)


GLA_CHUNK = 64
GLA_TBLOCK = 4096
GLA_WAVE = 4


GLA_FACTORED_MAX_LOG2 = 120.0
GLA_MASKED_EXPONENT = -1e30


def _gla_kernel(qf_ref, kf_ref, vf_ref, gf_ref, qb_ref, kb_ref, vb_ref, gb_ref,
                sf0_ref, sb0_ref, of_ref, ob_ref, stf, stb, row_sc):
    @pl.when(pl.program_id(2) == 0)
    def _():
        stf[...] = sf0_ref[...]
        stb[...] = sb0_ref[...]

    c, n = GLA_CHUNK, qf_ref.shape[0] // GLA_CHUNK
    streams = ((gf_ref, qf_ref, kf_ref, vf_ref, of_ref, False),
               (gb_ref, qb_ref, kb_ref, vb_ref, ob_ref, True))

    states, worst = _gla_factored(streams, stf, stb)
    factorable = worst <= GLA_FACTORED_MAX_LOG2

    @pl.when(factorable)
    def _():
        stf[...] = states[0]
        stb[...] = states[1]

    @pl.when(jnp.logical_not(factorable))
    def _():
        _gla_unfactored(streams, stf, stb, row_sc)


def _gla_unfactored(streams, stf, stb, row_sc):
    c = GLA_CHUNK
    ones = jnp.ones((SUBLANES, GLA_DK), BF16)
    key_idx = lax.broadcasted_iota(jnp.int32, (c, GLA_DK), 0)
    for (g_ref, q_ref, k_ref, v_ref, o_ref, upper), st_ref in zip(streams, (stf, stb)):
        n = q_ref.shape[0] // c
        tri = _tri(c, upper)
        edge = 0 if upper else c - 1

        def chunk(step, carry, g_ref=g_ref, q_ref=q_ref, k_ref=k_ref, v_ref=v_ref, o_ref=o_ref,
                  upper=upper, st_ref=st_ref, n=n, tri=tri, edge=edge):
            i = (n - 1 - step) if upper else step
            r = pl.ds(pl.multiple_of(i * c, c), c)
            b = _running_sums(tri, g_ref[r, :])
            q, k, v = q_ref[r, :].astype(F32), k_ref[r, :].astype(F32), v_ref[r, :]
            st = st_ref[...]
            inter = _dot_nt((q * jnp.exp2(b)).astype(BF16), st.astype(BF16))
            for row in range(c):
                valid = (key_idx >= row) if upper else (key_idx <= row)
                e = jnp.exp2(jnp.where(valid, b[row:row + 1, :] - b, GLA_MASKED_EXPONENT))
                w = (k * e * q[row:row + 1, :]).astype(BF16)
                a_row = _dot_nt(ones, w)
                row_sc[row:row + 1, :] = _dot(a_row.astype(BF16), v)[0:1, :]
            o_ref[r, :] = (row_sc[...] + inter).astype(BF16)
            k_out = (k * jnp.exp2(b[edge:edge + 1, :] - b)).astype(BF16)
            st_ref[...] = st * jnp.exp2(b[edge:edge + 1, :]) + _dot_tn(v, k_out)
            return carry

        lax.fori_loop(0, n, chunk, 0)


def _gla_factored(streams, stf, stb):
    c, n = GLA_CHUNK, streams[0][1].shape[0] // GLA_CHUNK
    tri, rows, q, k, v, b, b_mid, b_edge = [], [], [], [], [], [], [], []
    for g_ref, q_ref, k_ref, v_ref, _, upper in streams:
        t = _tri(c, upper)
        edge = 0 if upper else c - 1
        bw = _running_sums(t, jnp.concatenate([g_ref[i * c:(i + 1) * c, :] for i in range(n)], axis=1))
        for i in range(n):
            r = slice(i * c, (i + 1) * c)
            x = bw[:, i * GLA_DK:(i + 1) * GLA_DK]
            tri.append(t), rows.append(r), b.append(x)
            b_mid.append(x[c // 2:c // 2 + 1]), b_edge.append(x[edge:edge + 1])
            q.append(q_ref[r, :]), k.append(k_ref[r, :]), v.append(v_ref[r, :])
    st = [stf[...], stb[...]]
    for w0 in range(0, n, GLA_WAVE):
        u = [(d, d * n + (n - 1 - step if streams[d][5] else step))
             for step in range(w0, w0 + GLA_WAVE) for d in range(len(streams))]
        qt = {j: q[j].astype(F32) * jnp.exp2(b[j] - b_mid[j]) for _, j in u}
        kt = {j: k[j].astype(F32) * jnp.exp2(b_mid[j] - b[j]) for _, j in u}
        s = {j: _dot_nt(qt[j].astype(BF16), kt[j].astype(BF16)) for _, j in u}
        a = {j: jnp.where(tri[j], s[j], 0.0).astype(BF16) for _, j in u}
        k_out = {j: (kt[j] * jnp.exp2(b_edge[j] - b_mid[j])).astype(BF16) for _, j in u}
        upd = {j: _dot_tn(v[j], k_out[j]) for _, j in u}
        intra = {j: _dot(a[j], v[j]) for _, j in u}
        q_in = {j: (qt[j] * jnp.exp2(b_mid[j])).astype(BF16) for _, j in u}
        for d, j in u:
            streams[d][4][rows[j], :] = (intra[j] + _dot_nt(q_in[j], st[d].astype(BF16))).astype(BF16)
            st[d] = st[d] * jnp.exp2(b_edge[j]) + upd[j]
    worst = b_edge
    while len(worst) > 1:
        worst = [jnp.minimum(x, y) for x, y in zip(worst[0::2], worst[1::2])]
    return st, -jnp.min(worst[0])


def _gla_kernel_one_block(q_ref, k_ref, v_ref, gf_ref, gb_ref, sf0_ref, sb0_ref, of_ref, ob_ref, stf, stb, row_sc):
    _gla_kernel(q_ref, k_ref, v_ref, gf_ref, q_ref, k_ref, v_ref, gb_ref, sf0_ref, sb0_ref, of_ref, ob_ref,
                stf, stb, row_sc)


def _gla(main, g, sf0, sb0, *, batch, seq):
    tb = min(GLA_TBLOCK, seq)
    nt = seq // tb
    qb_, kb_, vb_ = COL_GQ // GLA_DK, COL_GK // GLA_DK, COL_GV // GLA_DV

    def fwd(col):
        return lambda b, h, t: (b * nt + t, col + h)

    def bwd(col):
        return lambda b, h, t: (b * nt + nt - 1 - t, col + h)

    def seq_specs(m, g_col):
        return [pl.BlockSpec((tb, GLA_DK), m(qb_)), pl.BlockSpec((tb, GLA_DK), m(kb_)),
                pl.BlockSpec((tb, GLA_DV), m(vb_)), pl.BlockSpec((tb, GLA_DK), m(g_col))]

    st_spec = pl.BlockSpec((None, None, GLA_DV, GLA_DK), lambda b, h, t: (b, h, 0, 0))
    out = jax.ShapeDtypeStruct((batch * seq, GLA_HEADS * GLA_DV), BF16)
    if nt == 1:
        kern = _gla_kernel_one_block
        in_specs = seq_specs(fwd, 0) + [pl.BlockSpec((tb, GLA_DK), bwd(GLA_HEADS)), st_spec, st_spec]
        operands = (main, main, main, g, g, sf0, sb0)
    else:
        kern = _gla_kernel
        in_specs = seq_specs(fwd, 0) + seq_specs(bwd, GLA_HEADS) + [st_spec, st_spec]
        operands = (main, main, main, g, main, main, main, g, sf0, sb0)
    return pl.pallas_call(
        kern,
        grid=(batch, GLA_HEADS, nt),
        in_specs=in_specs,
        out_specs=[pl.BlockSpec((tb, GLA_DV), fwd(0)), pl.BlockSpec((tb, GLA_DV), bwd(0))],
        out_shape=[out, out],
        scratch_shapes=[pltpu.VMEM((GLA_DV, GLA_DK), F32), pltpu.VMEM((GLA_DV, GLA_DK), F32),
                        pltpu.VMEM((GLA_CHUNK, GLA_DV), F32)],
        compiler_params=pltpu.CompilerParams(
            dimension_semantics=("arbitrary", "arbitrary", "arbitrary"), vmem_limit_bytes=VMEM_LIMIT),
        name="gla",
    )(*operands)


ATT_TQ = 1024
ATT_TK = 512


ATT_UNSHIFTED_MAX_LOG2 = 56.0


def _attn_kernel(bound_ref, q_ref, k_ref, v_ref, kc_ref, vc_ref, o_ref, vt_sc, vct_sc, m_sc, l_sc, acc_sc, *, seq):
    tq, tk, hd = ATT_TQ, ATT_TK, HEAD_DIM
    nk = seq // tk

    def transposed(v):
        return v.astype(F32).T.astype(BF16)

    @pl.when(pl.program_id(2) == 0)
    def _():
        for j in range(nk):
            vt_sc[j] = transposed(v_ref[j * tk:(j + 1) * tk, :])
        vct_sc[...] = transposed(vc_ref[...])

    q4 = jnp.concatenate([q_ref[:, g * hd:(g + 1) * hd] for g in range(Q_PER_KV)], axis=0)

    def key_sums(e):
        return jnp.sum(e.reshape(e.shape[0] // SUBLANES, SUBLANES, e.shape[1]), axis=0)

    def finish(acc, l):
        o = (acc / jnp.sum(l, axis=0, keepdims=True)).T
        for g in range(Q_PER_KV):
            o_ref[:, g * hd:(g + 1) * hd] = o[g * tq:(g + 1) * tq].astype(BF16)

    small = bound_ref[0, 0] <= ATT_UNSHIFTED_MAX_LOG2

    @pl.when(small)
    def _():
        chunks = [(k_ref[j * tk:(j + 1) * tk, :], vt_sc[j]) for j in range(nk)] + [(kc_ref[...], vct_sc[...])]
        acc = l = None
        for kc, vt in chunks:
            e = jnp.exp2(_dot_nt(kc, q4))
            pv, ks = _dot(vt, e.astype(BF16)), key_sums(e)
            acc, l = (pv, ks) if acc is None else (acc + pv, l + ks)
        finish(acc, l)

    @pl.when(jnp.logical_not(small))
    def _():
        m_sc[...] = jnp.full(m_sc.shape, -jnp.inf, F32)
        acc_sc[...] = jnp.zeros(acc_sc.shape, F32)
        l_sc[...] = jnp.zeros(l_sc.shape, F32)

        def online_step(kc, vt):
            s = _dot_nt(kc, q4)
            m_prev = m_sc[...]
            m_new = jnp.maximum(m_prev, jnp.max(s, axis=0, keepdims=True))
            e = jnp.exp2(s - m_new)
            rescale = jnp.exp2(m_prev - m_new)
            acc_sc[...] = rescale * acc_sc[...] + _dot(vt, e.astype(BF16))
            l_sc[...] = rescale * l_sc[...] + key_sums(e)
            m_sc[...] = m_new

        def body(j, carry):
            online_step(k_ref[pl.ds(pl.multiple_of(j * tk, tk), tk), :], vt_sc[j])
            return carry

        lax.fori_loop(0, nk, body, 0)
        online_step(kc_ref[...], vct_sc[...])
        finish(acc_sc[...], l_sc[...])


def _attn(score_bound, main, ctx_main, *, batch, seq, n_ctx):
    nq = seq // ATT_TQ
    gw = Q_PER_KV * HEAD_DIM
    rows = Q_PER_KV * ATT_TQ
    vt_rows = HEAD_DIM
    return pl.pallas_call(
        functools.partial(_attn_kernel, seq=seq),
        grid=(batch, N_KV_HEADS, nq),
        in_specs=[pl.BlockSpec(memory_space=pltpu.SMEM),
                  pl.BlockSpec((ATT_TQ, gw), lambda b, h, i: (b * nq + i, COL_AQ // gw + h)),
                  pl.BlockSpec((seq, HEAD_DIM), lambda b, h, i: (b, COL_AK // HEAD_DIM + h)),
                  pl.BlockSpec((seq, HEAD_DIM), lambda b, h, i: (b, COL_AV // HEAD_DIM + h)),
                  pl.BlockSpec((n_ctx, HEAD_DIM), lambda b, h, i: (b, COL_AK // HEAD_DIM + h)),
                  pl.BlockSpec((n_ctx, HEAD_DIM), lambda b, h, i: (b, COL_AV // HEAD_DIM + h))],
        out_specs=pl.BlockSpec((ATT_TQ, gw), lambda b, h, i: (b * nq + i, h)),
        out_shape=jax.ShapeDtypeStruct((batch * seq, N_Q_HEADS * HEAD_DIM), BF16),
        scratch_shapes=[pltpu.VMEM((seq // ATT_TK, vt_rows, ATT_TK), BF16),
                        pltpu.VMEM((vt_rows, n_ctx), BF16),
                        pltpu.VMEM((1, rows), F32), pltpu.VMEM((SUBLANES, rows), F32),
                        pltpu.VMEM((vt_rows, rows), F32)],
        compiler_params=pltpu.CompilerParams(
            dimension_semantics=("arbitrary", "arbitrary", "arbitrary"), vmem_limit_bytes=VMEM_LIMIT),
        name="attn",
    )(score_bound, main, main, main, ctx_main, ctx_main)


FF_CHUNK = 1024


def _tail_kernel(x_ref, attn_ref, of_ref, ob_ref, go_ref, ga_ref, gg_ref, gn_ref, gt1_ref,
                 wa_ref, wg_ref, wo_ref, sh_ref, sc_ref, gt2_ref, n2_ref, w1_ref, w2_ref, o_ref, *, d_ff):
    gn = gn_ref[...]
    heads = []
    for h in range(GLA_HEADS):
        s = slice(h * GLA_DV, (h + 1) * GLA_DV)
        o = of_ref[:, s].astype(F32) + ob_ref[:, s].astype(F32)
        heads.append((_head_norm(o, gn) * go_ref[:, s].astype(F32)).astype(BF16))
    gla = jnp.concatenate(heads, axis=-1)
    ya = ga_ref[...].astype(F32) * _dot(attn_ref[...], wa_ref[...])
    yg = gg_ref[...].astype(F32) * _dot(gla, wg_ref[...])
    x1 = x_ref[...] + gt1_ref[...] * _dot((ya + yg).astype(BF16), wo_ref[...])
    ms = jnp.mean(x1 * x1, axis=-1, keepdims=True)
    h2 = x1 * lax.rsqrt(ms + EPS) * n2_ref[...]
    hb = (h2 * (1.0 + sc_ref[...]) + sh_ref[...]).astype(BF16)
    acc = jnp.zeros(x1.shape, F32)
    for c in range(d_ff // FF_CHUNK):
        s = slice(c * FF_CHUNK, (c + 1) * FF_CHUNK)
        u = jnp.maximum(_dot(hb, w1_ref[:, s]), 0.0)
        acc = acc + _dot((u * u).astype(BF16), w2_ref[s, :])
    o_ref[...] = x1 + gt2_ref[...] * acc


def _tail(x2d, attn, o_f, o_b, main, gla_norm, mod4, wa, wg, wo, norm2, w1, w2, *, tm, seq_tiles):
    rows, d = x2d.shape
    d_ff = w1.shape[1]
    tile = pl.BlockSpec((tm, d), lambda i: (i, 0))
    mod = lambda g: pl.BlockSpec((None, None, 1, d), lambda i: (i // seq_tiles, g, 0, 0))
    const = lambda shape: _resident(shape, lambda i: (0, 0))
    return pl.pallas_call(
        functools.partial(_tail_kernel, d_ff=d_ff),
        grid=(rows // tm,),
        in_specs=[tile, tile, tile, tile,
                  pl.BlockSpec((tm, d), lambda i: (i, COL_GO // d)),
                  pl.BlockSpec((tm, d), lambda i: (i, COL_GA // d)),
                  pl.BlockSpec((tm, d), lambda i: (i, COL_GG // d)),
                  const((1, GLA_DV)), mod(2), const((d, d)), const((d, d)), const((d, d)),
                  mod(3), mod(4), mod(5), const((1, d)), const((d, d_ff)), const((d_ff, d))],
        out_specs=tile,
        out_shape=jax.ShapeDtypeStruct((rows, d), F32),
        compiler_params=pltpu.CompilerParams(vmem_limit_bytes=VMEM_LIMIT),
        name="merge_out_mlp",
    )(x2d, attn, o_f, o_b, main, main, main, gla_norm, mod4, wa, wg, wo, mod4, mod4, mod4, norm2, w1, w2)


def _rope_tables(seq):
    t = np.arange(seq)
    half = HEAD_DIM // 2
    freqs = ROPE_THETA ** (-np.arange(0, half, 2, dtype=np.float32) / half)
    ang_r = (t // GRID_W).astype(np.float32)[:, None] * freqs
    ang_c = (t % GRID_W).astype(np.float32)[:, None] * freqs
    cos = np.concatenate([np.cos(ang_r)] * 2 + [np.cos(ang_c)] * 2, axis=-1)
    sin = np.concatenate([-np.sin(ang_r), np.sin(ang_r), -np.sin(ang_c), np.sin(ang_c)], axis=-1)
    return jnp.asarray(cos, F32), jnp.asarray(sin, F32)


def _pad_rows(w, row0):
    return jnp.pad(w, ((row0, LR_PAD - row0 - w.shape[0]), (0, 0)))


def kernel(x, c, ctx, c_ctx, w_ada, b_ada, norm1, w_in, q_norm, k_norm, w_gk_fwd, b_gk_fwd,
           w_gk_bwd, b_gk_bwd, gla_norm, w_br_attn, w_br_gla, w_out, norm2, w_mlp1, w_mlp2):
    batch, seq, d = x.shape
    n_ctx = ctx.shape[1]
    depth = w_ada.shape[0]
    assert depth == 1 and batch < ADA_ROWS
    assert seq % min(GLA_TBLOCK, seq) == 0 and seq % ATT_TK == 0 and seq % ATT_TQ == 0 and seq % TOKEN_TILE == 0
    tm = TOKEN_TILE
    seq_tiles = seq // tm
    ctx_tm = n_ctx
    l = 0

    cc = jnp.zeros((ADA_ROWS, d), F32).at[:batch].set(c).at[batch].set(c_ctx)
    mod4 = _ada(cc, w_ada[l], b_ada[l][None, :]).reshape(ADA_ROWS, 6, 1, d)

    w_main, w_lr = _regroup_w_in(jnp.transpose(w_in[l]))
    cos_t, sin_t = _rope_tables(seq)
    q_scale = (HEAD_DIM ** -0.5) * float(np.log2(np.e))
    q_gain = (q_norm[l] * q_scale)[None, :]
    k_gain = k_norm[l][None, :]
    score_bound = (1.01 * HEAD_DIM * q_scale * jnp.max(jnp.abs(q_norm[l])) * jnp.max(jnp.abs(k_norm[l])))
    score_bound = score_bound.reshape(1, 1).astype(F32)
    n1 = norm1[l][None, :]

    w_gk = jnp.concatenate([_pad_rows(w_gk_fwd[l], 0), _pad_rows(w_gk_bwd[l], GLA_GATE_RANK)], axis=1).astype(BF16)
    b_gk = jnp.concatenate([b_gk_fwd[l], b_gk_bwd[l]])[None, :]

    main, g = _inproj(x.reshape(batch * seq, d), mod4, lambda i: i // seq_tiles, n1, w_main, w_lr, w_gk, b_gk,
                      cos_t, sin_t, q_gain, k_gain, n_cols=N_MAIN, rope=True, tm=tm, seq_tiles=seq_tiles)
    ctx_main, ctx_g = _inproj(ctx.reshape(batch * n_ctx, d), mod4, lambda i: batch, n1, w_main, w_lr, w_gk, b_gk,
                              cos_t, sin_t, q_gain, k_gain, n_cols=N_CTX_MAIN, rope=False, tm=ctx_tm,
                              seq_tiles=1)

    sf0, sb0 = _glastate(ctx_main, ctx_g, batch=batch, n_ctx=n_ctx)
    o_f, o_b = _gla(main, g, sf0, sb0, batch=batch, seq=seq)

    attn = _attn(score_bound, main, ctx_main, batch=batch, seq=seq, n_ctx=n_ctx)

    x2 = _tail(x.reshape(batch * seq, d), attn, o_f, o_b, main, gla_norm[l][None, :], mod4,
               w_br_attn[l].astype(BF16), w_br_gla[l].astype(BF16), w_out[l].astype(BF16),
               norm2[l][None, :], w_mlp1[l].astype(BF16), w_mlp2[l].astype(BF16),
               tm=tm, seq_tiles=seq_tiles)
    return x2.reshape(batch, seq, d)
```

```python
import functools

import numpy as np
import jax
import jax.numpy as jnp
from jax import lax
from jax.experimental import pallas as pl
from jax.experimental.pallas import tpu as pltpu

F32 = jnp.float32
BF16 = jnp.bfloat16

GRID_W = 64
HEAD_DIM = 128
N_Q_HEADS = 8
N_KV_HEADS = 2
Q_PER_KV = N_Q_HEADS // N_KV_HEADS
ROPE_THETA = 10000.0
GLA_HEADS = 4
GLA_DK = 128
GLA_DV = 256
GLA_GATE_RANK = 16
GLA_GATE_NORM = 16.0
EPS = 1e-6
LOG2_E = float(np.log2(np.e))

COL_AK, COL_AV, COL_GK, COL_GV = 0, 256, 512, 1024
COL_AQ, COL_GO, COL_GA, COL_GG, COL_GQ = 2048, 3072, 4096, 5120, 6144
N_MAIN = 6656
N_CTX_MAIN = 2048
SUBLANES = 8
LANES = 128
VMEM_BYTES = 64 * 1024 * 1024
VMEM_LIMIT = VMEM_BYTES - 8 * 1024 * 1024

LR_PAD = LANES
TOKEN_TILE = 512
PROJ_CHUNK = 512
PROJ_ROW_SPLIT = 2
DECAY_SLICES = 8


def _dot(a, b):
    return jnp.dot(a, b, preferred_element_type=F32)


def _dot_nt(a, b):
    return lax.dot_general(a, b, (((1,), (1,)), ((), ())), preferred_element_type=F32)


def _dot_tn(a, b):
    return lax.dot_general(a, b, (((0,), (0,)), ((), ())), preferred_element_type=F32)


def _sigmoid(x):
    return 0.5 * jnp.tanh(0.5 * x) + 0.5


def _resident(shape, index_map):
    return pl.BlockSpec(shape, index_map, pipeline_mode=pl.Buffered(1))


def _ada_kernel(c_ref, w_ref, b_ref, o_ref):
    c = c_ref[...]
    s = (c * _sigmoid(c)).astype(BF16)
    o_ref[...] = _dot(s, w_ref[...].astype(BF16)) + b_ref[...]


ADA_ROWS = SUBLANES
ADA_COL_TILE = 1536


def _ada(cc, w_ada, b_ada):
    d, n = w_ada.shape
    tn = ADA_COL_TILE
    return pl.pallas_call(
        _ada_kernel,
        grid=(n // tn,),
        in_specs=[pl.BlockSpec((ADA_ROWS, d), lambda j: (0, 0)),
                  pl.BlockSpec((d, tn), lambda j: (0, j)),
                  pl.BlockSpec((1, tn), lambda j: (0, j))],
        out_specs=pl.BlockSpec((ADA_ROWS, tn), lambda j: (0, j)),
        out_shape=jax.ShapeDtypeStruct((ADA_ROWS, n), F32),
        compiler_params=pltpu.CompilerParams(vmem_limit_bytes=VMEM_LIMIT),
        name="ada",
    )(cc, w_ada, b_ada)


W_IN_LR = (2048, 2080)
W_IN_GROUPS = (((0, 2048), 1.0), ((2080, 3104), 1.0), ((3616, 4640), 1.0), ((4640, 5664), 1.0),
               ((5664, 6688), 1.0), ((3104, 3616), GLA_DK ** -0.5))
W_IN_CHUNKS = tuple((lo + o, scale) for (lo, hi), scale in W_IN_GROUPS for o in range(0, hi - lo, PROJ_CHUNK))
assert len(W_IN_CHUNKS) * PROJ_CHUNK == N_MAIN


def _chunk_table(j, column, divisor=1):
    entry = lambda idx: W_IN_CHUNKS[idx][column] // divisor if divisor != 1 else W_IN_CHUNKS[idx][column]
    out = entry(len(W_IN_CHUNKS) - 1)
    for idx in range(len(W_IN_CHUNKS) - 2, -1, -1):
        out = jnp.where(j == idx, entry(idx), out)
    return out


def _regroup_kernel(src_ref, lr_src_ref, main_ref, lr_ref):
    scale = _chunk_table(pl.program_id(0), 1).astype(F32)
    main_ref[...] = (src_ref[...] * scale).T.astype(BF16)
    rank2 = W_IN_LR[1] - W_IN_LR[0]
    lr_rows = jnp.concatenate([lr_src_ref[...], jnp.zeros((LR_PAD - rank2, lr_src_ref.shape[1]), F32)], axis=0)
    lr_ref[...] = lr_rows.T.astype(BF16)


def _regroup_w_in(w_t):
    n_in, d = w_t.shape
    rank2 = W_IN_LR[1] - W_IN_LR[0]
    n_chunks = len(W_IN_CHUNKS)
    return pl.pallas_call(
        _regroup_kernel,
        grid=(len(W_IN_CHUNKS),),
        in_specs=[pl.BlockSpec((pl.Element(PROJ_CHUNK), pl.Element(d)),
                               lambda j: (_chunk_table(j, 0, rank2) * rank2, 0)),
                  pl.BlockSpec((rank2, d), lambda j: (W_IN_LR[0] // rank2, 0))],
        out_specs=[pl.BlockSpec((None, d, PROJ_CHUNK), lambda j: (j, 0, 0)),
                   pl.BlockSpec((d, LR_PAD), lambda j: (0, 0))],
        out_shape=[jax.ShapeDtypeStruct((n_chunks, d, PROJ_CHUNK), BF16),
                   jax.ShapeDtypeStruct((d, LR_PAD), BF16)],
        compiler_params=pltpu.CompilerParams(vmem_limit_bytes=VMEM_LIMIT),
        name="regroup_w_in",
    )(w_t, w_t)


def _head_norm(a, gain):
    ms = jnp.mean(a * a, axis=-1, keepdims=True)
    return a * lax.rsqrt(ms + EPS) * gain


def _rope(n, cos, sin_signed):
    half, quarter = HEAD_DIM // 2, HEAD_DIM // 4
    lane = lax.broadcasted_iota(jnp.int32, n.shape, 1)
    partner = jnp.where((lane % half) < quarter, pltpu.roll(n, HEAD_DIM - quarter, 1), pltpu.roll(n, quarter, 1))
    return n * cos + partner * sin_signed


def _log2_decay(lowrank, w, b):
    z = _dot(lowrank.astype(BF16), w) + b
    return (jnp.minimum(z, 0.0) - jnp.log(1.0 + jnp.exp(-jnp.abs(z)))) * (LOG2_E / GLA_GATE_NORM)


def _inproj_kernel(x_ref, sh_ref, sc_ref, n1_ref, w_ref, wlr_ref, wgk_ref, bgk_ref, cos_ref, sin_ref,
                   qg_ref, kg_ref, main_ref, g_ref, *, n_cols, rope):
    half = x_ref.shape[0] // PROJ_ROW_SPLIT
    g_slices = min(DECAY_SLICES, n_cols // PROJ_CHUNK)
    g_width = g_ref.shape[1] // g_slices
    for part in range(PROJ_ROW_SPLIT):
        r = slice(part * half, (part + 1) * half)
        x = x_ref[r, :]
        ms = jnp.mean(x * x, axis=-1, keepdims=True)
        h = x * lax.rsqrt(ms + EPS) * n1_ref[...]
        hb = (h * (1.0 + sc_ref[...]) + sh_ref[...]).astype(BF16)

        def qk_head(a, gain):
            n = _head_norm(a, gain)
            if rope:
                n = _rope(n, cos_ref[r, :], sin_ref[r, :])
            return n.astype(BF16)

        for c in range(n_cols // PROJ_CHUNK):
            lo = c * PROJ_CHUNK
            acc = _dot(hb, w_ref[c])
            for s in range(PROJ_CHUNK // HEAD_DIM):
                col = lo + s * HEAD_DIM
                a = acc[:, s * HEAD_DIM:(s + 1) * HEAD_DIM]
                if COL_AK <= col < COL_AV:
                    main_ref[r, col:col + HEAD_DIM] = qk_head(a, kg_ref[...])
                elif COL_AQ <= col < COL_GO:
                    main_ref[r, col:col + HEAD_DIM] = qk_head(a, qg_ref[...])
                elif COL_GO <= col < COL_GA:
                    main_ref[r, col:col + HEAD_DIM] = (a * _sigmoid(a)).astype(BF16)
                elif COL_GA <= col < COL_GQ:
                    main_ref[r, col:col + HEAD_DIM] = _sigmoid(a).astype(BF16)
                else:
                    main_ref[r, col:col + HEAD_DIM] = a.astype(BF16)
            if c < g_slices:
                if c == 0:
                    lowrank = _dot(hb, wlr_ref[...])
                gs = slice(c * g_width, (c + 1) * g_width)
                g_ref[r, gs] = _log2_decay(lowrank, wgk_ref[:, gs], bgk_ref[:, gs]).astype(BF16)


def _inproj(x2d, mod4, mod_row_of_tile, norm1, w_main, w_lr, w_gk, b_gk, cos_t, sin_t, q_gain, k_gain,
            *, n_cols, rope, tm, seq_tiles):
    rows, d = x2d.shape
    n_g = w_gk.shape[1]
    kern = functools.partial(_inproj_kernel, n_cols=n_cols, rope=rope)
    return pl.pallas_call(
        kern,
        grid=(rows // tm,),
        in_specs=[
            pl.BlockSpec((tm, d), lambda i: (i, 0)),
            pl.BlockSpec((None, None, 1, d), lambda i: (mod_row_of_tile(i), 0, 0, 0)),
            pl.BlockSpec((None, None, 1, d), lambda i: (mod_row_of_tile(i), 1, 0, 0)),
            _resident((1, d), lambda i: (0, 0)),
            _resident((n_cols // PROJ_CHUNK, d, PROJ_CHUNK), lambda i: (0, 0, 0)),
            _resident((d, LR_PAD), lambda i: (0, 0)),
            _resident((LR_PAD, n_g), lambda i: (0, 0)),
            _resident((1, n_g), lambda i: (0, 0)),
            pl.BlockSpec((tm, HEAD_DIM), lambda i: (i % seq_tiles, 0)),
            pl.BlockSpec((tm, HEAD_DIM), lambda i: (i % seq_tiles, 0)),
            _resident((1, HEAD_DIM), lambda i: (0, 0)),
            _resident((1, HEAD_DIM), lambda i: (0, 0)),
        ],
        out_specs=[pl.BlockSpec((tm, n_cols), lambda i: (i, 0)),
                   pl.BlockSpec((tm, n_g), lambda i: (i, 0))],
        out_shape=[jax.ShapeDtypeStruct((rows, n_cols), BF16),
                   jax.ShapeDtypeStruct((rows, n_g), BF16)],
        compiler_params=pltpu.CompilerParams(vmem_limit_bytes=VMEM_LIMIT),
        name="inproj_rope" if rope else "inproj_ctx",
    )(x2d, mod4, mod4, norm1, w_main, w_lr, w_gk, b_gk, cos_t, sin_t, q_gain, k_gain)


def _tri(n, upper):
    r = lax.broadcasted_iota(jnp.int32, (n, n), 0)
    c = lax.broadcasted_iota(jnp.int32, (n, n), 1)
    return (c >= r) if upper else (r >= c)


def _running_sums(tri, g):
    return _dot(tri.astype(BF16), g)


def _glastate_kernel(k_ref, v_ref, gf_ref, gb_ref, sf_ref, sb_ref, *, n):
    k = k_ref[...].astype(F32)
    b = _running_sums(_tri(n, False), gf_ref[...])
    k_fwd = (k * jnp.exp2(b[n - 1:n] - b)).astype(BF16)
    e = _running_sums(_tri(n, True), gb_ref[...])
    k_bwd = (k * jnp.exp2(e[0:1] - e)).astype(BF16)
    for h in range(GLA_HEADS):
        v = v_ref[:, h * GLA_DV:(h + 1) * GLA_DV]
        kc = slice(h * GLA_DK, (h + 1) * GLA_DK)
        sf_ref[h] = _dot_tn(v, k_fwd[:, kc])
        sb_ref[h] = _dot_tn(v, k_bwd[:, kc])


def _glastate(ctx_main, ctx_g, *, batch, n_ctx):
    kw, vw = GLA_HEADS * GLA_DK, GLA_HEADS * GLA_DV
    st = jax.ShapeDtypeStruct((batch, GLA_HEADS, GLA_DV, GLA_DK), F32)
    st_spec = pl.BlockSpec((None, GLA_HEADS, GLA_DV, GLA_DK), lambda b: (b, 0, 0, 0))
    return pl.pallas_call(
        functools.partial(_glastate_kernel, n=n_ctx),
        grid=(batch,),
        in_specs=[pl.BlockSpec((n_ctx, kw), lambda b: (b, COL_GK // kw)),
                  pl.BlockSpec((n_ctx, vw), lambda b: (b, COL_GV // vw)),
                  pl.BlockSpec((n_ctx, kw), lambda b: (b, 0)),
                  pl.BlockSpec((n_ctx, kw), lambda b: (b, 1))],
        out_specs=[st_spec, st_spec],
        out_shape=[st, st],
        compiler_params=pltpu.CompilerParams(vmem_limit_bytes=VMEM_LIMIT),
        name="gla_ctx_state",
    )(ctx_main, ctx_main, ctx_g, ctx_g)


GLA_CHUNK = 64
GLA_TBLOCK = 4096
GLA_WAVE = 1


GLA_FACTORED_MAX_LOG2 = 120.0
GLA_MASKED_EXPONENT = -1e30


def _gla_kernel(qf_ref, kf_ref, vf_ref, gf_ref, qb_ref, kb_ref, vb_ref, gb_ref,
                sf0_ref, sb0_ref, of_ref, ob_ref, stf, stb, row_sc):
    @pl.when(pl.program_id(2) == 0)
    def _():
        stf[...] = sf0_ref[...]
        stb[...] = sb0_ref[...]

    c, n = GLA_CHUNK, qf_ref.shape[0] // GLA_CHUNK
    streams = ((gf_ref, qf_ref, kf_ref, vf_ref, of_ref, False),
               (gb_ref, qb_ref, kb_ref, vb_ref, ob_ref, True))

    states, worst = _gla_factored(streams, stf, stb)
    factorable = worst <= GLA_FACTORED_MAX_LOG2

    @pl.when(factorable)
    def _():
        stf[...] = states[0]
        stb[...] = states[1]

    @pl.when(jnp.logical_not(factorable))
    def _():
        _gla_unfactored(streams, stf, stb, row_sc)


def _gla_unfactored(streams, stf, stb, row_sc):
    c = GLA_CHUNK
    ones = jnp.ones((SUBLANES, GLA_DK), BF16)
    key_idx = lax.broadcasted_iota(jnp.int32, (c, GLA_DK), 0)
    for (g_ref, q_ref, k_ref, v_ref, o_ref, upper), st_ref in zip(streams, (stf, stb)):
        n = q_ref.shape[0] // c
        tri = _tri(c, upper)
        edge = 0 if upper else c - 1

        def chunk(step, carry, g_ref=g_ref, q_ref=q_ref, k_ref=k_ref, v_ref=v_ref, o_ref=o_ref,
                  upper=upper, st_ref=st_ref, n=n, tri=tri, edge=edge):
            i = (n - 1 - step) if upper else step
            r = pl.ds(pl.multiple_of(i * c, c), c)
            b = _running_sums(tri, g_ref[r, :])
            q, k, v = q_ref[r, :].astype(F32), k_ref[r, :].astype(F32), v_ref[r, :]
            st = st_ref[...]
            inter = _dot_nt((q * jnp.exp2(b)).astype(BF16), st.astype(BF16))
            for row in range(c):
                valid = (key_idx >= row) if upper else (key_idx <= row)
                e = jnp.exp2(jnp.where(valid, b[row:row + 1, :] - b, GLA_MASKED_EXPONENT))
                w = (k * e * q[row:row + 1, :]).astype(BF16)
                a_row = _dot_nt(ones, w)
                row_sc[row:row + 1, :] = _dot(a_row.astype(BF16), v)[0:1, :]
            o_ref[r, :] = (row_sc[...] + inter).astype(BF16)
            k_out = (k * jnp.exp2(b[edge:edge + 1, :] - b)).astype(BF16)
            st_ref[...] = st * jnp.exp2(b[edge:edge + 1, :]) + _dot_tn(v, k_out)
            return carry

        lax.fori_loop(0, n, chunk, 0)


def _gla_factored(streams, stf, stb):
    c, n = GLA_CHUNK, streams[0][1].shape[0] // GLA_CHUNK
    tri, rows, q, k, v, b, b_mid, b_edge = [], [], [], [], [], [], [], []
    for g_ref, q_ref, k_ref, v_ref, _, upper in streams:
        t = _tri(c, upper)
        edge = 0 if upper else c - 1
        bw = _running_sums(t, jnp.concatenate([g_ref[i * c:(i + 1) * c, :] for i in range(n)], axis=1))
        for i in range(n):
            r = slice(i * c, (i + 1) * c)
            x = bw[:, i * GLA_DK:(i + 1) * GLA_DK]
            tri.append(t), rows.append(r), b.append(x)
            b_mid.append(x[c // 2:c // 2 + 1]), b_edge.append(x[edge:edge + 1])
            q.append(q_ref[r, :]), k.append(k_ref[r, :]), v.append(v_ref[r, :])
    def stage(w0):
        u = [(d, d * n + (n - 1 - step if streams[d][5] else step))
             for step in range(w0, w0 + GLA_WAVE) for d in range(len(streams))]
        qt = {j: q[j].astype(F32) * jnp.exp2(b[j] - b_mid[j]) for _, j in u}
        kt = {j: k[j].astype(F32) * jnp.exp2(b_mid[j] - b[j]) for _, j in u}
        s = {j: _dot_nt(qt[j].astype(BF16), kt[j].astype(BF16)) for _, j in u}
        a = {j: jnp.where(tri[j], s[j], 0.0).astype(BF16) for _, j in u}
        k_out = {j: (kt[j] * jnp.exp2(b_edge[j] - b_mid[j])).astype(BF16) for _, j in u}
        upd = {j: _dot_tn(v[j], k_out[j]) for _, j in u}
        intra = {j: _dot(a[j], v[j]) for _, j in u}
        q_in = {j: (qt[j] * jnp.exp2(b_mid[j])).astype(BF16) for _, j in u}
        return u, upd, intra, q_in

    st = [stf[...], stb[...]]
    staged = stage(0)
    for w0 in range(0, n, GLA_WAVE):
        following = stage(w0 + GLA_WAVE) if w0 + GLA_WAVE < n else None
        u, upd, intra, q_in = staged
        for d, j in u:
            streams[d][4][rows[j], :] = (intra[j] + _dot_nt(q_in[j], st[d].astype(BF16))).astype(BF16)
            st[d] = st[d] * jnp.exp2(b_edge[j]) + upd[j]
        staged = following
    worst = b_edge
    while len(worst) > 1:
        worst = [jnp.minimum(x, y) for x, y in zip(worst[0::2], worst[1::2])]
    return st, -jnp.min(worst[0])


def _gla_kernel_one_block(q_ref, k_ref, v_ref, gf_ref, gb_ref, sf0_ref, sb0_ref, of_ref, ob_ref, stf, stb, row_sc):
    _gla_kernel(q_ref, k_ref, v_ref, gf_ref, q_ref, k_ref, v_ref, gb_ref, sf0_ref, sb0_ref, of_ref, ob_ref,
                stf, stb, row_sc)


def _gla(main, g, sf0, sb0, *, batch, seq):
    tb = min(GLA_TBLOCK, seq)
    nt = seq // tb
    qb_, kb_, vb_ = COL_GQ // GLA_DK, COL_GK // GLA_DK, COL_GV // GLA_DV

    def fwd(col):
        return lambda b, h, t: (b * nt + t, col + h)

    def bwd(col):
        return lambda b, h, t: (b * nt + nt - 1 - t, col + h)

    def seq_specs(m, g_col):
        return [pl.BlockSpec((tb, GLA_DK), m(qb_)), pl.BlockSpec((tb, GLA_DK), m(kb_)),
                pl.BlockSpec((tb, GLA_DV), m(vb_)), pl.BlockSpec((tb, GLA_DK), m(g_col))]

    st_spec = pl.BlockSpec((None, None, GLA_DV, GLA_DK), lambda b, h, t: (b, h, 0, 0))
    out = jax.ShapeDtypeStruct((batch * seq, GLA_HEADS * GLA_DV), BF16)
    if nt == 1:
        kern = _gla_kernel_one_block
        in_specs = seq_specs(fwd, 0) + [pl.BlockSpec((tb, GLA_DK), bwd(GLA_HEADS)), st_spec, st_spec]
        operands = (main, main, main, g, g, sf0, sb0)
    else:
        kern = _gla_kernel
        in_specs = seq_specs(fwd, 0) + seq_specs(bwd, GLA_HEADS) + [st_spec, st_spec]
        operands = (main, main, main, g, main, main, main, g, sf0, sb0)
    return pl.pallas_call(
        kern,
        grid=(batch, GLA_HEADS, nt),
        in_specs=in_specs,
        out_specs=[pl.BlockSpec((tb, GLA_DV), fwd(0)), pl.BlockSpec((tb, GLA_DV), bwd(0))],
        out_shape=[out, out],
        scratch_shapes=[pltpu.VMEM((GLA_DV, GLA_DK), F32), pltpu.VMEM((GLA_DV, GLA_DK), F32),
                        pltpu.VMEM((GLA_CHUNK, GLA_DV), F32)],
        compiler_params=pltpu.CompilerParams(
            dimension_semantics=("arbitrary", "arbitrary", "arbitrary"), vmem_limit_bytes=VMEM_LIMIT),
        name="gla",
    )(*operands)


ATT_TQ = 1024
ATT_TK = 512


ATT_UNSHIFTED_MAX_LOG2 = 56.0


def _attn_kernel(bound_ref, q_ref, k_ref, v_ref, kc_ref, vc_ref, o_ref, vt_sc, vct_sc, m_sc, l_sc, acc_sc, *, seq):
    tq, tk, hd = ATT_TQ, ATT_TK, HEAD_DIM
    nk = seq // tk

    def transposed(v):
        return v.astype(F32).T.astype(BF16)

    @pl.when(pl.program_id(2) == 0)
    def _():
        for j in range(nk):
            vt_sc[j] = transposed(v_ref[j * tk:(j + 1) * tk, :])
        vct_sc[...] = transposed(vc_ref[...])

    q4 = jnp.concatenate([q_ref[:, g * hd:(g + 1) * hd] for g in range(Q_PER_KV)], axis=0)

    def key_sums(e):
        return jnp.sum(e.reshape(e.shape[0] // SUBLANES, SUBLANES, e.shape[1]), axis=0)

    def finish(acc, l):
        o = (acc / jnp.sum(l, axis=0, keepdims=True)).T
        for g in range(Q_PER_KV):
            o_ref[:, g * hd:(g + 1) * hd] = o[g * tq:(g + 1) * tq].astype(BF16)

    small = bound_ref[0, 0] <= ATT_UNSHIFTED_MAX_LOG2

    @pl.when(small)
    def _():
        chunks = [(k_ref[j * tk:(j + 1) * tk, :], vt_sc[j]) for j in range(nk)] + [(kc_ref[...], vct_sc[...])]
        acc = l = None
        for kc, vt in chunks:
            e = jnp.exp2(_dot_nt(kc, q4))
            pv, ks = _dot(vt, e.astype(BF16)), key_sums(e)
            acc, l = (pv, ks) if acc is None else (acc + pv, l + ks)
        finish(acc, l)

    @pl.when(jnp.logical_not(small))
    def _():
        m_sc[...] = jnp.full(m_sc.shape, -jnp.inf, F32)
        acc_sc[...] = jnp.zeros(acc_sc.shape, F32)
        l_sc[...] = jnp.zeros(l_sc.shape, F32)

        def online_step(kc, vt):
            s = _dot_nt(kc, q4)
            m_prev = m_sc[...]
            m_new = jnp.maximum(m_prev, jnp.max(s, axis=0, keepdims=True))
            e = jnp.exp2(s - m_new)
            rescale = jnp.exp2(m_prev - m_new)
            acc_sc[...] = rescale * acc_sc[...] + _dot(vt, e.astype(BF16))
            l_sc[...] = rescale * l_sc[...] + key_sums(e)
            m_sc[...] = m_new

        def body(j, carry):
            online_step(k_ref[pl.ds(pl.multiple_of(j * tk, tk), tk), :], vt_sc[j])
            return carry

        lax.fori_loop(0, nk, body, 0)
        online_step(kc_ref[...], vct_sc[...])
        finish(acc_sc[...], l_sc[...])


def _attn(score_bound, main, ctx_main, *, batch, seq, n_ctx):
    nq = seq // ATT_TQ
    gw = Q_PER_KV * HEAD_DIM
    rows = Q_PER_KV * ATT_TQ
    vt_rows = HEAD_DIM
    return pl.pallas_call(
        functools.partial(_attn_kernel, seq=seq),
        grid=(batch, N_KV_HEADS, nq),
        in_specs=[pl.BlockSpec(memory_space=pltpu.SMEM),
                  pl.BlockSpec((ATT_TQ, gw), lambda b, h, i: (b * nq + i, COL_AQ // gw + h)),
                  pl.BlockSpec((seq, HEAD_DIM), lambda b, h, i: (b, COL_AK // HEAD_DIM + h)),
                  pl.BlockSpec((seq, HEAD_DIM), lambda b, h, i: (b, COL_AV // HEAD_DIM + h)),
                  pl.BlockSpec((n_ctx, HEAD_DIM), lambda b, h, i: (b, COL_AK // HEAD_DIM + h)),
                  pl.BlockSpec((n_ctx, HEAD_DIM), lambda b, h, i: (b, COL_AV // HEAD_DIM + h))],
        out_specs=pl.BlockSpec((ATT_TQ, gw), lambda b, h, i: (b * nq + i, h)),
        out_shape=jax.ShapeDtypeStruct((batch * seq, N_Q_HEADS * HEAD_DIM), BF16),
        scratch_shapes=[pltpu.VMEM((seq // ATT_TK, vt_rows, ATT_TK), BF16),
                        pltpu.VMEM((vt_rows, n_ctx), BF16),
                        pltpu.VMEM((1, rows), F32), pltpu.VMEM((SUBLANES, rows), F32),
                        pltpu.VMEM((vt_rows, rows), F32)],
        compiler_params=pltpu.CompilerParams(
            dimension_semantics=("arbitrary", "arbitrary", "arbitrary"), vmem_limit_bytes=VMEM_LIMIT),
        name="attn",
    )(score_bound, main, main, main, ctx_main, ctx_main)


FF_CHUNK = 1024


def _tail_kernel(x_ref, attn_ref, of_ref, ob_ref, go_ref, ga_ref, gg_ref, gn_ref, gt1_ref,
                 wa_ref, wg_ref, wo_ref, sh_ref, sc_ref, gt2_ref, n2_ref, w1_ref, w2_ref, o_ref, *, d_ff):
    gn = gn_ref[...]
    heads = []
    for h in range(GLA_HEADS):
        s = slice(h * GLA_DV, (h + 1) * GLA_DV)
        o = of_ref[:, s].astype(F32) + ob_ref[:, s].astype(F32)
        heads.append((_head_norm(o, gn) * go_ref[:, s].astype(F32)).astype(BF16))
    gla = jnp.concatenate(heads, axis=-1)
    ya = ga_ref[...].astype(F32) * _dot(attn_ref[...], wa_ref[...])
    yg = gg_ref[...].astype(F32) * _dot(gla, wg_ref[...])
    x1 = x_ref[...] + gt1_ref[...] * _dot((ya + yg).astype(BF16), wo_ref[...])
    ms = jnp.mean(x1 * x1, axis=-1, keepdims=True)
    h2 = x1 * lax.rsqrt(ms + EPS) * n2_ref[...]
    hb = (h2 * (1.0 + sc_ref[...]) + sh_ref[...]).astype(BF16)
    acc = jnp.zeros(x1.shape, F32)
    for c in range(d_ff // FF_CHUNK):
        s = slice(c * FF_CHUNK, (c + 1) * FF_CHUNK)
        u = jnp.maximum(_dot(hb, w1_ref[:, s]), 0.0)
        acc = acc + _dot((u * u).astype(BF16), w2_ref[s, :])
    o_ref[...] = x1 + gt2_ref[...] * acc


def _tail(x2d, attn, o_f, o_b, main, gla_norm, mod4, wa, wg, wo, norm2, w1, w2, *, tm, seq_tiles):
    rows, d = x2d.shape
    d_ff = w1.shape[1]
    tile = pl.BlockSpec((tm, d), lambda i: (i, 0))
    mod = lambda g: pl.BlockSpec((None, None, 1, d), lambda i: (i // seq_tiles, g, 0, 0))
    const = lambda shape: _resident(shape, lambda i: (0, 0))
    return pl.pallas_call(
        functools.partial(_tail_kernel, d_ff=d_ff),
        grid=(rows // tm,),
        in_specs=[tile, tile, tile, tile,
                  pl.BlockSpec((tm, d), lambda i: (i, COL_GO // d)),
                  pl.BlockSpec((tm, d), lambda i: (i, COL_GA // d)),
                  pl.BlockSpec((tm, d), lambda i: (i, COL_GG // d)),
                  const((1, GLA_DV)), mod(2), const((d, d)), const((d, d)), const((d, d)),
                  mod(3), mod(4), mod(5), const((1, d)), const((d, d_ff)), const((d_ff, d))],
        out_specs=tile,
        out_shape=jax.ShapeDtypeStruct((rows, d), F32),
        compiler_params=pltpu.CompilerParams(vmem_limit_bytes=VMEM_LIMIT),
        name="merge_out_mlp",
    )(x2d, attn, o_f, o_b, main, main, main, gla_norm, mod4, wa, wg, wo, mod4, mod4, mod4, norm2, w1, w2)


def _rope_tables(seq):
    t = np.arange(seq)
    half = HEAD_DIM // 2
    freqs = ROPE_THETA ** (-np.arange(0, half, 2, dtype=np.float32) / half)
    ang_r = (t // GRID_W).astype(np.float32)[:, None] * freqs
    ang_c = (t % GRID_W).astype(np.float32)[:, None] * freqs
    cos = np.concatenate([np.cos(ang_r)] * 2 + [np.cos(ang_c)] * 2, axis=-1)
    sin = np.concatenate([-np.sin(ang_r), np.sin(ang_r), -np.sin(ang_c), np.sin(ang_c)], axis=-1)
    return jnp.asarray(cos, F32), jnp.asarray(sin, F32)


def _pad_rows(w, row0):
    return jnp.pad(w, ((row0, LR_PAD - row0 - w.shape[0]), (0, 0)))


def kernel(x, c, ctx, c_ctx, w_ada, b_ada, norm1, w_in, q_norm, k_norm, w_gk_fwd, b_gk_fwd,
           w_gk_bwd, b_gk_bwd, gla_norm, w_br_attn, w_br_gla, w_out, norm2, w_mlp1, w_mlp2):
    batch, seq, d = x.shape
    n_ctx = ctx.shape[1]
    depth = w_ada.shape[0]
    assert depth == 1 and batch < ADA_ROWS
    assert seq % min(GLA_TBLOCK, seq) == 0 and seq % ATT_TK == 0 and seq % ATT_TQ == 0 and seq % TOKEN_TILE == 0
    tm = TOKEN_TILE
    seq_tiles = seq // tm
    ctx_tm = n_ctx
    l = 0

    cc = jnp.zeros((ADA_ROWS, d), F32).at[:batch].set(c).at[batch].set(c_ctx)
    mod4 = _ada(cc, w_ada[l], b_ada[l][None, :]).reshape(ADA_ROWS, 6, 1, d)

    w_main, w_lr = _regroup_w_in(jnp.transpose(w_in[l]))
    cos_t, sin_t = _rope_tables(seq)
    q_scale = (HEAD_DIM ** -0.5) * float(np.log2(np.e))
    q_gain = (q_norm[l] * q_scale)[None, :]
    k_gain = k_norm[l][None, :]
    score_bound = (1.01 * HEAD_DIM * q_scale * jnp.max(jnp.abs(q_norm[l])) * jnp.max(jnp.abs(k_norm[l])))
    score_bound = score_bound.reshape(1, 1).astype(F32)
    n1 = norm1[l][None, :]

    w_gk = jnp.concatenate([_pad_rows(w_gk_fwd[l], 0), _pad_rows(w_gk_bwd[l], GLA_GATE_RANK)], axis=1).astype(BF16)
    b_gk = jnp.concatenate([b_gk_fwd[l], b_gk_bwd[l]])[None, :]

    main, g = _inproj(x.reshape(batch * seq, d), mod4, lambda i: i // seq_tiles, n1, w_main, w_lr, w_gk, b_gk,
                      cos_t, sin_t, q_gain, k_gain, n_cols=N_MAIN, rope=True, tm=tm, seq_tiles=seq_tiles)
    ctx_main, ctx_g = _inproj(ctx.reshape(batch * n_ctx, d), mod4, lambda i: batch, n1, w_main, w_lr, w_gk, b_gk,
                              cos_t, sin_t, q_gain, k_gain, n_cols=N_CTX_MAIN, rope=False, tm=ctx_tm,
                              seq_tiles=1)

    sf0, sb0 = _glastate(ctx_main, ctx_g, batch=batch, n_ctx=n_ctx)
    o_f, o_b = _gla(main, g, sf0, sb0, batch=batch, seq=seq)

    attn = _attn(score_bound, main, ctx_main, batch=batch, seq=seq, n_ctx=n_ctx)

    x2 = _tail(x.reshape(batch * seq, d), attn, o_f, o_b, main, gla_norm[l][None, :], mod4,
               w_br_attn[l].astype(BF16), w_br_gla[l].astype(BF16), w_out[l].astype(BF16),
               norm2[l][None, :], w_mlp1[l].astype(BF16), w_mlp2[l].astype(BF16),
               tm=tm, seq_tiles=seq_tiles)
    return x2.reshape(batch, seq, d)
```

```python
import functools

import numpy as np
import jax
import jax.numpy as jnp
from jax import lax
from jax.experimental import pallas as pl
from jax.experimental.pallas import tpu as pltpu

F32 = jnp.float32
BF16 = jnp.bfloat16

GRID_W = 64
HEAD_DIM = 128
N_Q_HEADS = 8
N_KV_HEADS = 2
Q_PER_KV = N_Q_HEADS // N_KV_HEADS
ROPE_THETA = 10000.0
GLA_HEADS = 4
GLA_DK = 128
GLA_DV = 256
GLA_GATE_RANK = 16
GLA_GATE_NORM = 16.0
EPS = 1e-6
LOG2_E = float(np.log2(np.e))

COL_AK, COL_AV, COL_GK, COL_GV = 0, 256, 512, 1024
COL_AQ, COL_GO, COL_GA, COL_GG, COL_GQ = 2048, 3072, 4096, 5120, 6144
N_MAIN = 6656
N_CTX_MAIN = 2048
SUBLANES = 8
LANES = 128
VMEM_BYTES = 64 * 1024 * 1024
VMEM_LIMIT = VMEM_BYTES - 8 * 1024 * 1024

LR_PAD = LANES
TOKEN_TILE = 512
PROJ_CHUNK = 512
PROJ_ROW_SPLIT = 2
DECAY_SLICES = 8


def _dot(a, b):
    return jnp.dot(a, b, preferred_element_type=F32)


def _dot_nt(a, b):
    return lax.dot_general(a, b, (((1,), (1,)), ((), ())), preferred_element_type=F32)


def _dot_tn(a, b):
    return lax.dot_general(a, b, (((0,), (0,)), ((), ())), preferred_element_type=F32)


def _sigmoid(x):
    return 0.5 * jnp.tanh(0.5 * x) + 0.5


def _resident(shape, index_map):
    return pl.BlockSpec(shape, index_map, pipeline_mode=pl.Buffered(1))


def _ada_kernel(c_ref, w_ref, b_ref, o_ref):
    c = c_ref[...]
    s = (c * _sigmoid(c)).astype(BF16)
    o_ref[...] = _dot(s, w_ref[...].astype(BF16)) + b_ref[...]


ADA_ROWS = SUBLANES
ADA_COL_TILE = 1536


def _ada(cc, w_ada, b_ada):
    d, n = w_ada.shape
    tn = ADA_COL_TILE
    return pl.pallas_call(
        _ada_kernel,
        grid=(n // tn,),
        in_specs=[pl.BlockSpec((ADA_ROWS, d), lambda j: (0, 0)),
                  pl.BlockSpec((d, tn), lambda j: (0, j)),
                  pl.BlockSpec((1, tn), lambda j: (0, j))],
        out_specs=pl.BlockSpec((ADA_ROWS, tn), lambda j: (0, j)),
        out_shape=jax.ShapeDtypeStruct((ADA_ROWS, n), F32),
        compiler_params=pltpu.CompilerParams(vmem_limit_bytes=VMEM_LIMIT),
        name="ada",
    )(cc, w_ada, b_ada)


W_IN_LR = (2048, 2080)
W_IN_GROUPS = (((0, 2048), 1.0), ((2080, 3104), 1.0), ((3616, 4640), 1.0), ((4640, 5664), 1.0),
               ((5664, 6688), 1.0), ((3104, 3616), GLA_DK ** -0.5))
W_IN_CHUNKS = tuple((lo + o, scale) for (lo, hi), scale in W_IN_GROUPS for o in range(0, hi - lo, PROJ_CHUNK))
assert len(W_IN_CHUNKS) * PROJ_CHUNK == N_MAIN


def _chunk_table(j, column, divisor=1):
    entry = lambda idx: W_IN_CHUNKS[idx][column] // divisor if divisor != 1 else W_IN_CHUNKS[idx][column]
    out = entry(len(W_IN_CHUNKS) - 1)
    for idx in range(len(W_IN_CHUNKS) - 2, -1, -1):
        out = jnp.where(j == idx, entry(idx), out)
    return out


def _regroup_kernel(src_ref, lr_src_ref, main_ref, lr_ref):
    scale = _chunk_table(pl.program_id(0), 1).astype(F32)
    main_ref[...] = (src_ref[...] * scale).T.astype(BF16)
    rank2 = W_IN_LR[1] - W_IN_LR[0]
    lr_rows = jnp.concatenate([lr_src_ref[...], jnp.zeros((LR_PAD - rank2, lr_src_ref.shape[1]), F32)], axis=0)
    lr_ref[...] = lr_rows.T.astype(BF16)


def _regroup_w_in(w_t):
    n_in, d = w_t.shape
    rank2 = W_IN_LR[1] - W_IN_LR[0]
    n_chunks = len(W_IN_CHUNKS)
    return pl.pallas_call(
        _regroup_kernel,
        grid=(len(W_IN_CHUNKS),),
        in_specs=[pl.BlockSpec((pl.Element(PROJ_CHUNK), pl.Element(d)),
                               lambda j: (_chunk_table(j, 0, rank2) * rank2, 0)),
                  pl.BlockSpec((rank2, d), lambda j: (W_IN_LR[0] // rank2, 0))],
        out_specs=[pl.BlockSpec((None, d, PROJ_CHUNK), lambda j: (j, 0, 0)),
                   pl.BlockSpec((d, LR_PAD), lambda j: (0, 0))],
        out_shape=[jax.ShapeDtypeStruct((n_chunks, d, PROJ_CHUNK), BF16),
                   jax.ShapeDtypeStruct((d, LR_PAD), BF16)],
        compiler_params=pltpu.CompilerParams(vmem_limit_bytes=VMEM_LIMIT),
        name="regroup_w_in",
    )(w_t, w_t)


def _head_norm(a, gain):
    ms = jnp.mean(a * a, axis=-1, keepdims=True)
    return a * lax.rsqrt(ms + EPS) * gain


def _rope(n, cos, sin_signed):
    half, quarter = HEAD_DIM // 2, HEAD_DIM // 4
    lane = lax.broadcasted_iota(jnp.int32, n.shape, 1)
    partner = jnp.where((lane % half) < quarter, pltpu.roll(n, HEAD_DIM - quarter, 1), pltpu.roll(n, quarter, 1))
    return n * cos + partner * sin_signed


def _log2_decay(lowrank, w, b):
    z = _dot(lowrank.astype(BF16), w) + b
    return (jnp.minimum(z, 0.0) - jnp.log(1.0 + jnp.exp(-jnp.abs(z)))) * (LOG2_E / GLA_GATE_NORM)


def _inproj_kernel(x_ref, sh_ref, sc_ref, n1_ref, w_ref, wlr_ref, wgk_ref, bgk_ref, cos_ref, sin_ref,
                   qg_ref, kg_ref, main_ref, g_ref, *, n_cols, rope):
    half = x_ref.shape[0] // PROJ_ROW_SPLIT
    g_slices = min(DECAY_SLICES, n_cols // PROJ_CHUNK)
    g_width = g_ref.shape[1] // g_slices
    for part in range(PROJ_ROW_SPLIT):
        r = slice(part * half, (part + 1) * half)
        x = x_ref[r, :]
        ms = jnp.mean(x * x, axis=-1, keepdims=True)
        h = x * lax.rsqrt(ms + EPS) * n1_ref[...]
        hb = (h * (1.0 + sc_ref[...]) + sh_ref[...]).astype(BF16)

        def qk_head(a, gain):
            n = _head_norm(a, gain)
            if rope:
                n = _rope(n, cos_ref[r, :], sin_ref[r, :])
            return n.astype(BF16)

        for c in range(n_cols // PROJ_CHUNK):
            lo = c * PROJ_CHUNK
            acc = _dot(hb, w_ref[c])
            for s in range(PROJ_CHUNK // HEAD_DIM):
                col = lo + s * HEAD_DIM
                a = acc[:, s * HEAD_DIM:(s + 1) * HEAD_DIM]
                if COL_AK <= col < COL_AV:
                    main_ref[r, col:col + HEAD_DIM] = qk_head(a, kg_ref[...])
                elif COL_AQ <= col < COL_GO:
                    main_ref[r, col:col + HEAD_DIM] = qk_head(a, qg_ref[...])
                elif COL_GO <= col < COL_GA:
                    main_ref[r, col:col + HEAD_DIM] = (a * _sigmoid(a)).astype(BF16)
                elif COL_GA <= col < COL_GQ:
                    main_ref[r, col:col + HEAD_DIM] = _sigmoid(a).astype(BF16)
                else:
                    main_ref[r, col:col + HEAD_DIM] = a.astype(BF16)
            if c < g_slices:
                if c == 0:
                    lowrank = _dot(hb, wlr_ref[...])
                gs = slice(c * g_width, (c + 1) * g_width)
                g_ref[r, gs] = _log2_decay(lowrank, wgk_ref[:, gs], bgk_ref[:, gs]).astype(BF16)


def _inproj(x2d, mod4, mod_row_of_tile, norm1, w_main, w_lr, w_gk, b_gk, cos_t, sin_t, q_gain, k_gain,
            *, n_cols, rope, tm, seq_tiles):
    rows, d = x2d.shape
    n_g = w_gk.shape[1]
    kern = functools.partial(_inproj_kernel, n_cols=n_cols, rope=rope)
    return pl.pallas_call(
        kern,
        grid=(rows // tm,),
        in_specs=[
            pl.BlockSpec((tm, d), lambda i: (i, 0)),
            pl.BlockSpec((None, None, 1, d), lambda i: (mod_row_of_tile(i), 0, 0, 0)),
            pl.BlockSpec((None, None, 1, d), lambda i: (mod_row_of_tile(i), 1, 0, 0)),
            _resident((1, d), lambda i: (0, 0)),
            _resident((n_cols // PROJ_CHUNK, d, PROJ_CHUNK), lambda i: (0, 0, 0)),
            _resident((d, LR_PAD), lambda i: (0, 0)),
            _resident((LR_PAD, n_g), lambda i: (0, 0)),
            _resident((1, n_g), lambda i: (0, 0)),
            pl.BlockSpec((tm, HEAD_DIM), lambda i: (i % seq_tiles, 0)),
            pl.BlockSpec((tm, HEAD_DIM), lambda i: (i % seq_tiles, 0)),
            _resident((1, HEAD_DIM), lambda i: (0, 0)),
            _resident((1, HEAD_DIM), lambda i: (0, 0)),
        ],
        out_specs=[pl.BlockSpec((tm, n_cols), lambda i: (i, 0)),
                   pl.BlockSpec((tm, n_g), lambda i: (i, 0))],
        out_shape=[jax.ShapeDtypeStruct((rows, n_cols), BF16),
                   jax.ShapeDtypeStruct((rows, n_g), BF16)],
        compiler_params=pltpu.CompilerParams(vmem_limit_bytes=VMEM_LIMIT),
        name="inproj_rope" if rope else "inproj_ctx",
    )(x2d, mod4, mod4, norm1, w_main, w_lr, w_gk, b_gk, cos_t, sin_t, q_gain, k_gain)


def _tri(n, upper):
    r = lax.broadcasted_iota(jnp.int32, (n, n), 0)
    c = lax.broadcasted_iota(jnp.int32, (n, n), 1)
    return (c >= r) if upper else (r >= c)


def _running_sums(tri, g):
    return _dot(tri.astype(BF16), g)


def _glastate_kernel(k_ref, v_ref, gf_ref, gb_ref, sf_ref, sb_ref, *, n):
    k = k_ref[...].astype(F32)
    b = _running_sums(_tri(n, False), gf_ref[...])
    k_fwd = (k * jnp.exp2(b[n - 1:n] - b)).astype(BF16)
    e = _running_sums(_tri(n, True), gb_ref[...])
    k_bwd = (k * jnp.exp2(e[0:1] - e)).astype(BF16)
    for h in range(GLA_HEADS):
        v = v_ref[:, h * GLA_DV:(h + 1) * GLA_DV]
        kc = slice(h * GLA_DK, (h + 1) * GLA_DK)
        sf_ref[h] = _dot_tn(v, k_fwd[:, kc])
        sb_ref[h] = _dot_tn(v, k_bwd[:, kc])


def _glastate(ctx_main, ctx_g, *, batch, n_ctx):
    kw, vw = GLA_HEADS * GLA_DK, GLA_HEADS * GLA_DV
    st = jax.ShapeDtypeStruct((batch, GLA_HEADS, GLA_DV, GLA_DK), F32)
    st_spec = pl.BlockSpec((None, GLA_HEADS, GLA_DV, GLA_DK), lambda b: (b, 0, 0, 0))
    return pl.pallas_call(
        functools.partial(_glastate_kernel, n=n_ctx),
        grid=(batch,),
        in_specs=[pl.BlockSpec((n_ctx, kw), lambda b: (b, COL_GK // kw)),
                  pl.BlockSpec((n_ctx, vw), lambda b: (b, COL_GV // vw)),
                  pl.BlockSpec((n_ctx, kw), lambda b: (b, 0)),
                  pl.BlockSpec((n_ctx, kw), lambda b: (b, 1))],
        out_specs=[st_spec, st_spec],
        out_shape=[st, st],
        compiler_params=pltpu.CompilerParams(vmem_limit_bytes=VMEM_LIMIT),
        name="gla_ctx_state",
    )(ctx_main, ctx_main, ctx_g, ctx_g)


GLA_CHUNK = 64
GLA_TBLOCK = 4096
GLA_WAVE = 1
GLA_AHEAD = 4


GLA_FACTORED_MAX_LOG2 = 120.0
GLA_MASKED_EXPONENT = -1e30


def _gla_kernel(qf_ref, kf_ref, vf_ref, gf_ref, qb_ref, kb_ref, vb_ref, gb_ref,
                sf0_ref, sb0_ref, of_ref, ob_ref, stf, stb, row_sc):
    @pl.when(pl.program_id(2) == 0)
    def _():
        stf[...] = sf0_ref[...]
        stb[...] = sb0_ref[...]

    c, n = GLA_CHUNK, qf_ref.shape[0] // GLA_CHUNK
    streams = ((gf_ref, qf_ref, kf_ref, vf_ref, of_ref, False),
               (gb_ref, qb_ref, kb_ref, vb_ref, ob_ref, True))

    states, worst = _gla_factored(streams, stf, stb)
    factorable = worst <= GLA_FACTORED_MAX_LOG2

    @pl.when(factorable)
    def _():
        stf[...] = states[0]
        stb[...] = states[1]

    @pl.when(jnp.logical_not(factorable))
    def _():
        _gla_unfactored(streams, stf, stb, row_sc)


def _gla_unfactored(streams, stf, stb, row_sc):
    c = GLA_CHUNK
    ones = jnp.ones((SUBLANES, GLA_DK), BF16)
    key_idx = lax.broadcasted_iota(jnp.int32, (c, GLA_DK), 0)
    for (g_ref, q_ref, k_ref, v_ref, o_ref, upper), st_ref in zip(streams, (stf, stb)):
        n = q_ref.shape[0] // c
        tri = _tri(c, upper)
        edge = 0 if upper else c - 1

        def chunk(step, carry, g_ref=g_ref, q_ref=q_ref, k_ref=k_ref, v_ref=v_ref, o_ref=o_ref,
                  upper=upper, st_ref=st_ref, n=n, tri=tri, edge=edge):
            i = (n - 1 - step) if upper else step
            r = pl.ds(pl.multiple_of(i * c, c), c)
            b = _running_sums(tri, g_ref[r, :])
            q, k, v = q_ref[r, :].astype(F32), k_ref[r, :].astype(F32), v_ref[r, :]
            st = st_ref[...]
            inter = _dot_nt((q * jnp.exp2(b)).astype(BF16), st.astype(BF16))
            for row in range(c):
                valid = (key_idx >= row) if upper else (key_idx <= row)
                e = jnp.exp2(jnp.where(valid, b[row:row + 1, :] - b, GLA_MASKED_EXPONENT))
                w = (k * e * q[row:row + 1, :]).astype(BF16)
                a_row = _dot_nt(ones, w)
                row_sc[row:row + 1, :] = _dot(a_row.astype(BF16), v)[0:1, :]
            o_ref[r, :] = (row_sc[...] + inter).astype(BF16)
            k_out = (k * jnp.exp2(b[edge:edge + 1, :] - b)).astype(BF16)
            st_ref[...] = st * jnp.exp2(b[edge:edge + 1, :]) + _dot_tn(v, k_out)
            return carry

        lax.fori_loop(0, n, chunk, 0)


def _gla_factored(streams, stf, stb):
    c, n = GLA_CHUNK, streams[0][1].shape[0] // GLA_CHUNK
    tri, rows, q, k, v, b, b_mid, b_edge = [], [], [], [], [], [], [], []
    for g_ref, q_ref, k_ref, v_ref, _, upper in streams:
        t = _tri(c, upper)
        edge = 0 if upper else c - 1
        bw = _running_sums(t, jnp.concatenate([g_ref[i * c:(i + 1) * c, :] for i in range(n)], axis=1))
        for i in range(n):
            r = slice(i * c, (i + 1) * c)
            x = bw[:, i * GLA_DK:(i + 1) * GLA_DK]
            tri.append(t), rows.append(r), b.append(x)
            b_mid.append(x[c // 2:c // 2 + 1]), b_edge.append(x[edge:edge + 1])
            q.append(q_ref[r, :]), k.append(k_ref[r, :]), v.append(v_ref[r, :])
    def stage(w0):
        u = [(d, d * n + (n - 1 - step if streams[d][5] else step))
             for step in range(w0, w0 + GLA_WAVE) for d in range(len(streams))]
        qt = {j: q[j].astype(F32) * jnp.exp2(b[j] - b_mid[j]) for _, j in u}
        kt = {j: k[j].astype(F32) * jnp.exp2(b_mid[j] - b[j]) for _, j in u}
        s = {j: _dot_nt(qt[j].astype(BF16), kt[j].astype(BF16)) for _, j in u}
        a = {j: jnp.where(tri[j], s[j], 0.0).astype(BF16) for _, j in u}
        k_out = {j: (kt[j] * jnp.exp2(b_edge[j] - b_mid[j])).astype(BF16) for _, j in u}
        upd = {j: _dot_tn(v[j], k_out[j]) for _, j in u}
        intra = {j: _dot(a[j], v[j]) for _, j in u}
        q_in = {j: (qt[j] * jnp.exp2(b_mid[j])).astype(BF16) for _, j in u}
        return u, upd, intra, q_in

    st = [stf[...], stb[...]]
    waves = list(range(0, n, GLA_WAVE))
    staged = [stage(w) for w in waves[:GLA_AHEAD]]
    for idx in range(len(waves)):
        if idx + GLA_AHEAD < len(waves):
            staged.append(stage(waves[idx + GLA_AHEAD]))
        u, upd, intra, q_in = staged.pop(0)
        for d, j in u:
            streams[d][4][rows[j], :] = (intra[j] + _dot_nt(q_in[j], st[d].astype(BF16))).astype(BF16)
            st[d] = st[d] * jnp.exp2(b_edge[j]) + upd[j]
    worst = b_edge
    while len(worst) > 1:
        worst = [jnp.minimum(x, y) for x, y in zip(worst[0::2], worst[1::2])]
    return st, -jnp.min(worst[0])


def _gla_kernel_one_block(q_ref, k_ref, v_ref, gf_ref, gb_ref, sf0_ref, sb0_ref, of_ref, ob_ref, stf, stb, row_sc):
    _gla_kernel(q_ref, k_ref, v_ref, gf_ref, q_ref, k_ref, v_ref, gb_ref, sf0_ref, sb0_ref, of_ref, ob_ref,
                stf, stb, row_sc)


def _gla(main, g, sf0, sb0, *, batch, seq):
    tb = min(GLA_TBLOCK, seq)
    nt = seq // tb
    qb_, kb_, vb_ = COL_GQ // GLA_DK, COL_GK // GLA_DK, COL_GV // GLA_DV

    def fwd(col):
        return lambda b, h, t: (b * nt + t, col + h)

    def bwd(col):
        return lambda b, h, t: (b * nt + nt - 1 - t, col + h)

    def seq_specs(m, g_col):
        return [pl.BlockSpec((tb, GLA_DK), m(qb_)), pl.BlockSpec((tb, GLA_DK), m(kb_)),
                pl.BlockSpec((tb, GLA_DV), m(vb_)), pl.BlockSpec((tb, GLA_DK), m(g_col))]

    st_spec = pl.BlockSpec((None, None, GLA_DV, GLA_DK), lambda b, h, t: (b, h, 0, 0))
    out = jax.ShapeDtypeStruct((batch * seq, GLA_HEADS * GLA_DV), BF16)
    if nt == 1:
        kern = _gla_kernel_one_block
        in_specs = seq_specs(fwd, 0) + [pl.BlockSpec((tb, GLA_DK), bwd(GLA_HEADS)), st_spec, st_spec]
        operands = (main, main, main, g, g, sf0, sb0)
    else:
        kern = _gla_kernel
        in_specs = seq_specs(fwd, 0) + seq_specs(bwd, GLA_HEADS) + [st_spec, st_spec]
        operands = (main, main, main, g, main, main, main, g, sf0, sb0)
    return pl.pallas_call(
        kern,
        grid=(batch, GLA_HEADS, nt),
        in_specs=in_specs,
        out_specs=[pl.BlockSpec((tb, GLA_DV), fwd(0)), pl.BlockSpec((tb, GLA_DV), bwd(0))],
        out_shape=[out, out],
        scratch_shapes=[pltpu.VMEM((GLA_DV, GLA_DK), F32), pltpu.VMEM((GLA_DV, GLA_DK), F32),
                        pltpu.VMEM((GLA_CHUNK, GLA_DV), F32)],
        compiler_params=pltpu.CompilerParams(
            dimension_semantics=("arbitrary", "arbitrary", "arbitrary"), vmem_limit_bytes=VMEM_LIMIT),
        name="gla",
    )(*operands)


ATT_TQ = 1024
ATT_TK = 512


ATT_UNSHIFTED_MAX_LOG2 = 56.0


def _attn_kernel(bound_ref, q_ref, k_ref, v_ref, kc_ref, vc_ref, o_ref, vt_sc, vct_sc, m_sc, l_sc, acc_sc, *, seq):
    tq, tk, hd = ATT_TQ, ATT_TK, HEAD_DIM
    nk = seq // tk

    def transposed(v):
        return v.astype(F32).T.astype(BF16)

    @pl.when(pl.program_id(2) == 0)
    def _():
        for j in range(nk):
            vt_sc[j] = transposed(v_ref[j * tk:(j + 1) * tk, :])
        vct_sc[...] = transposed(vc_ref[...])

    q4 = jnp.concatenate([q_ref[:, g * hd:(g + 1) * hd] for g in range(Q_PER_KV)], axis=0)

    def key_sums(e):
        return jnp.sum(e.reshape(e.shape[0] // SUBLANES, SUBLANES, e.shape[1]), axis=0)

    def finish(acc, l):
        o = (acc / jnp.sum(l, axis=0, keepdims=True)).T
        for g in range(Q_PER_KV):
            o_ref[:, g * hd:(g + 1) * hd] = o[g * tq:(g + 1) * tq].astype(BF16)

    small = bound_ref[0, 0] <= ATT_UNSHIFTED_MAX_LOG2

    @pl.when(small)
    def _():
        chunks = [(k_ref[j * tk:(j + 1) * tk, :], vt_sc[j]) for j in range(nk)] + [(kc_ref[...], vct_sc[...])]
        acc = l = None
        for kc, vt in chunks:
            e = jnp.exp2(_dot_nt(kc, q4))
            pv, ks = _dot(vt, e.astype(BF16)), key_sums(e)
            acc, l = (pv, ks) if acc is None else (acc + pv, l + ks)
        finish(acc, l)

    @pl.when(jnp.logical_not(small))
    def _():
        m_sc[...] = jnp.full(m_sc.shape, -jnp.inf, F32)
        acc_sc[...] = jnp.zeros(acc_sc.shape, F32)
        l_sc[...] = jnp.zeros(l_sc.shape, F32)

        def online_step(kc, vt):
            s = _dot_nt(kc, q4)
            m_prev = m_sc[...]
            m_new = jnp.maximum(m_prev, jnp.max(s, axis=0, keepdims=True))
            e = jnp.exp2(s - m_new)
            rescale = jnp.exp2(m_prev - m_new)
            acc_sc[...] = rescale * acc_sc[...] + _dot(vt, e.astype(BF16))
            l_sc[...] = rescale * l_sc[...] + key_sums(e)
            m_sc[...] = m_new

        def body(j, carry):
            online_step(k_ref[pl.ds(pl.multiple_of(j * tk, tk), tk), :], vt_sc[j])
            return carry

        lax.fori_loop(0, nk, body, 0)
        online_step(kc_ref[...], vct_sc[...])
        finish(acc_sc[...], l_sc[...])


def _attn(score_bound, main, ctx_main, *, batch, seq, n_ctx):
    nq = seq // ATT_TQ
    gw = Q_PER_KV * HEAD_DIM
    rows = Q_PER_KV * ATT_TQ
    vt_rows = HEAD_DIM
    return pl.pallas_call(
        functools.partial(_attn_kernel, seq=seq),
        grid=(batch, N_KV_HEADS, nq),
        in_specs=[pl.BlockSpec(memory_space=pltpu.SMEM),
                  pl.BlockSpec((ATT_TQ, gw), lambda b, h, i: (b * nq + i, COL_AQ // gw + h)),
                  pl.BlockSpec((seq, HEAD_DIM), lambda b, h, i: (b, COL_AK // HEAD_DIM + h)),
                  pl.BlockSpec((seq, HEAD_DIM), lambda b, h, i: (b, COL_AV // HEAD_DIM + h)),
                  pl.BlockSpec((n_ctx, HEAD_DIM), lambda b, h, i: (b, COL_AK // HEAD_DIM + h)),
                  pl.BlockSpec((n_ctx, HEAD_DIM), lambda b, h, i: (b, COL_AV // HEAD_DIM + h))],
        out_specs=pl.BlockSpec((ATT_TQ, gw), lambda b, h, i: (b * nq + i, h)),
        out_shape=jax.ShapeDtypeStruct((batch * seq, N_Q_HEADS * HEAD_DIM), BF16),
        scratch_shapes=[pltpu.VMEM((seq // ATT_TK, vt_rows, ATT_TK), BF16),
                        pltpu.VMEM((vt_rows, n_ctx), BF16),
                        pltpu.VMEM((1, rows), F32), pltpu.VMEM((SUBLANES, rows), F32),
                        pltpu.VMEM((vt_rows, rows), F32)],
        compiler_params=pltpu.CompilerParams(
            dimension_semantics=("arbitrary", "arbitrary", "arbitrary"), vmem_limit_bytes=VMEM_LIMIT),
        name="attn",
    )(score_bound, main, main, main, ctx_main, ctx_main)


FF_CHUNK = 1024


def _tail_kernel(x_ref, attn_ref, of_ref, ob_ref, go_ref, ga_ref, gg_ref, gn_ref, gt1_ref,
                 wa_ref, wg_ref, wo_ref, sh_ref, sc_ref, gt2_ref, n2_ref, w1_ref, w2_ref, o_ref, *, d_ff):
    gn = gn_ref[...]
    heads = []
    for h in range(GLA_HEADS):
        s = slice(h * GLA_DV, (h + 1) * GLA_DV)
        o = of_ref[:, s].astype(F32) + ob_ref[:, s].astype(F32)
        heads.append((_head_norm(o, gn) * go_ref[:, s].astype(F32)).astype(BF16))
    gla = jnp.concatenate(heads, axis=-1)
    ya = ga_ref[...].astype(F32) * _dot(attn_ref[...], wa_ref[...])
    yg = gg_ref[...].astype(F32) * _dot(gla, wg_ref[...])
    x1 = x_ref[...] + gt1_ref[...] * _dot((ya + yg).astype(BF16), wo_ref[...])
    ms = jnp.mean(x1 * x1, axis=-1, keepdims=True)
    h2 = x1 * lax.rsqrt(ms + EPS) * n2_ref[...]
    hb = (h2 * (1.0 + sc_ref[...]) + sh_ref[...]).astype(BF16)
    acc = jnp.zeros(x1.shape, F32)
    for c in range(d_ff // FF_CHUNK):
        s = slice(c * FF_CHUNK, (c + 1) * FF_CHUNK)
        u = jnp.maximum(_dot(hb, w1_ref[:, s]), 0.0)
        acc = acc + _dot((u * u).astype(BF16), w2_ref[s, :])
    o_ref[...] = x1 + gt2_ref[...] * acc


def _tail(x2d, attn, o_f, o_b, main, gla_norm, mod4, wa, wg, wo, norm2, w1, w2, *, tm, seq_tiles):
    rows, d = x2d.shape
    d_ff = w1.shape[1]
    tile = pl.BlockSpec((tm, d), lambda i: (i, 0))
    mod = lambda g: pl.BlockSpec((None, None, 1, d), lambda i: (i // seq_tiles, g, 0, 0))
    const = lambda shape: _resident(shape, lambda i: (0, 0))
    return pl.pallas_call(
        functools.partial(_tail_kernel, d_ff=d_ff),
        grid=(rows // tm,),
        in_specs=[tile, tile, tile, tile,
                  pl.BlockSpec((tm, d), lambda i: (i, COL_GO // d)),
                  pl.BlockSpec((tm, d), lambda i: (i, COL_GA // d)),
                  pl.BlockSpec((tm, d), lambda i: (i, COL_GG // d)),
                  const((1, GLA_DV)), mod(2), const((d, d)), const((d, d)), const((d, d)),
                  mod(3), mod(4), mod(5), const((1, d)), const((d, d_ff)), const((d_ff, d))],
        out_specs=tile,
        out_shape=jax.ShapeDtypeStruct((rows, d), F32),
        compiler_params=pltpu.CompilerParams(vmem_limit_bytes=VMEM_LIMIT),
        name="merge_out_mlp",
    )(x2d, attn, o_f, o_b, main, main, main, gla_norm, mod4, wa, wg, wo, mod4, mod4, mod4, norm2, w1, w2)


def _rope_tables(seq):
    t = np.arange(seq)
    half = HEAD_DIM // 2
    freqs = ROPE_THETA ** (-np.arange(0, half, 2, dtype=np.float32) / half)
    ang_r = (t // GRID_W).astype(np.float32)[:, None] * freqs
    ang_c = (t % GRID_W).astype(np.float32)[:, None] * freqs
    cos = np.concatenate([np.cos(ang_r)] * 2 + [np.cos(ang_c)] * 2, axis=-1)
    sin = np.concatenate([-np.sin(ang_r), np.sin(ang_r), -np.sin(ang_c), np.sin(ang_c)], axis=-1)
    return jnp.asarray(cos, F32), jnp.asarray(sin, F32)


def _pad_rows(w, row0):
    return jnp.pad(w, ((row0, LR_PAD - row0 - w.shape[0]), (0, 0)))


def kernel(x, c, ctx, c_ctx, w_ada, b_ada, norm1, w_in, q_norm, k_norm, w_gk_fwd, b_gk_fwd,
           w_gk_bwd, b_gk_bwd, gla_norm, w_br_attn, w_br_gla, w_out, norm2, w_mlp1, w_mlp2):
    batch, seq, d = x.shape
    n_ctx = ctx.shape[1]
    depth = w_ada.shape[0]
    assert depth == 1 and batch < ADA_ROWS
    assert seq % min(GLA_TBLOCK, seq) == 0 and seq % ATT_TK == 0 and seq % ATT_TQ == 0 and seq % TOKEN_TILE == 0
    tm = TOKEN_TILE
    seq_tiles = seq // tm
    ctx_tm = n_ctx
    l = 0

    cc = jnp.zeros((ADA_ROWS, d), F32).at[:batch].set(c).at[batch].set(c_ctx)
    mod4 = _ada(cc, w_ada[l], b_ada[l][None, :]).reshape(ADA_ROWS, 6, 1, d)

    w_main, w_lr = _regroup_w_in(jnp.transpose(w_in[l]))
    cos_t, sin_t = _rope_tables(seq)
    q_scale = (HEAD_DIM ** -0.5) * float(np.log2(np.e))
    q_gain = (q_norm[l] * q_scale)[None, :]
    k_gain = k_norm[l][None, :]
    score_bound = (1.01 * HEAD_DIM * q_scale * jnp.max(jnp.abs(q_norm[l])) * jnp.max(jnp.abs(k_norm[l])))
    score_bound = score_bound.reshape(1, 1).astype(F32)
    n1 = norm1[l][None, :]

    w_gk = jnp.concatenate([_pad_rows(w_gk_fwd[l], 0), _pad_rows(w_gk_bwd[l], GLA_GATE_RANK)], axis=1).astype(BF16)
    b_gk = jnp.concatenate([b_gk_fwd[l], b_gk_bwd[l]])[None, :]

    main, g = _inproj(x.reshape(batch * seq, d), mod4, lambda i: i // seq_tiles, n1, w_main, w_lr, w_gk, b_gk,
                      cos_t, sin_t, q_gain, k_gain, n_cols=N_MAIN, rope=True, tm=tm, seq_tiles=seq_tiles)
    ctx_main, ctx_g = _inproj(ctx.reshape(batch * n_ctx, d), mod4, lambda i: batch, n1, w_main, w_lr, w_gk, b_gk,
                              cos_t, sin_t, q_gain, k_gain, n_cols=N_CTX_MAIN, rope=False, tm=ctx_tm,
                              seq_tiles=1)

    sf0, sb0 = _glastate(ctx_main, ctx_g, batch=batch, n_ctx=n_ctx)
    o_f, o_b = _gla(main, g, sf0, sb0, batch=batch, seq=seq)

    attn = _attn(score_bound, main, ctx_main, batch=batch, seq=seq, n_ctx=n_ctx)

    x2 = _tail(x.reshape(batch * seq, d), attn, o_f, o_b, main, gla_norm[l][None, :], mod4,
               w_br_attn[l].astype(BF16), w_br_gla[l].astype(BF16), w_out[l].astype(BF16),
               norm2[l][None, :], w_mlp1[l].astype(BF16), w_mlp2[l].astype(BF16),
               tm=tm, seq_tiles=seq_tiles)
    return x2.reshape(batch, seq, d)
```

```python
import functools

import numpy as np
import jax
import jax.numpy as jnp
from jax import lax
from jax.experimental import pallas as pl
from jax.experimental.pallas import tpu as pltpu

F32 = jnp.float32
BF16 = jnp.bfloat16

GRID_W = 64
HEAD_DIM = 128
N_Q_HEADS = 8
N_KV_HEADS = 2
Q_PER_KV = N_Q_HEADS // N_KV_HEADS
ROPE_THETA = 10000.0
GLA_HEADS = 4
GLA_DK = 128
GLA_DV = 256
GLA_GATE_RANK = 16
GLA_GATE_NORM = 16.0
EPS = 1e-6
LOG2_E = float(np.log2(np.e))

COL_AK, COL_AV, COL_GK, COL_GV = 0, 256, 512, 1024
COL_AQ, COL_GO, COL_GA, COL_GG, COL_GQ = 2048, 3072, 4096, 5120, 6144
N_MAIN = 6656
N_CTX_MAIN = 2048
SUBLANES = 8
LANES = 128
VMEM_BYTES = 64 * 1024 * 1024
VMEM_LIMIT = VMEM_BYTES - 8 * 1024 * 1024

LR_PAD = LANES
TOKEN_TILE = 512
PROJ_CHUNK = 512
PROJ_ROW_SPLIT = 2
DECAY_SLICES = 8


def _dot(a, b):
    return jnp.dot(a, b, preferred_element_type=F32)


def _dot_nt(a, b):
    return lax.dot_general(a, b, (((1,), (1,)), ((), ())), preferred_element_type=F32)


def _dot_tn(a, b):
    return lax.dot_general(a, b, (((0,), (0,)), ((), ())), preferred_element_type=F32)


def _sigmoid(x):
    return 0.5 * jnp.tanh(0.5 * x) + 0.5


def _resident(shape, index_map):
    return pl.BlockSpec(shape, index_map, pipeline_mode=pl.Buffered(1))


def _ada_kernel(c_ref, w_ref, b_ref, o_ref):
    c = c_ref[...]
    s = (c * _sigmoid(c)).astype(BF16)
    o_ref[...] = _dot(s, w_ref[...].astype(BF16)) + b_ref[...]


ADA_ROWS = SUBLANES
ADA_COL_TILE = 1536


def _ada(cc, w_ada, b_ada):
    d, n = w_ada.shape
    tn = ADA_COL_TILE
    return pl.pallas_call(
        _ada_kernel,
        grid=(n // tn,),
        in_specs=[pl.BlockSpec((ADA_ROWS, d), lambda j: (0, 0)),
                  pl.BlockSpec((d, tn), lambda j: (0, j)),
                  pl.BlockSpec((1, tn), lambda j: (0, j))],
        out_specs=pl.BlockSpec((ADA_ROWS, tn), lambda j: (0, j)),
        out_shape=jax.ShapeDtypeStruct((ADA_ROWS, n), F32),
        compiler_params=pltpu.CompilerParams(vmem_limit_bytes=VMEM_LIMIT),
        name="ada",
    )(cc, w_ada, b_ada)


W_IN_LR = (2048, 2080)
W_IN_GROUPS = (((0, 2048), 1.0), ((2080, 3104), 1.0), ((3616, 4640), 1.0), ((4640, 5664), 1.0),
               ((5664, 6688), 1.0), ((3104, 3616), GLA_DK ** -0.5))
W_IN_CHUNKS = tuple((lo + o, scale) for (lo, hi), scale in W_IN_GROUPS for o in range(0, hi - lo, PROJ_CHUNK))
assert len(W_IN_CHUNKS) * PROJ_CHUNK == N_MAIN


def _chunk_table(j, column, divisor=1):
    entry = lambda idx: W_IN_CHUNKS[idx][column] // divisor if divisor != 1 else W_IN_CHUNKS[idx][column]
    out = entry(len(W_IN_CHUNKS) - 1)
    for idx in range(len(W_IN_CHUNKS) - 2, -1, -1):
        out = jnp.where(j == idx, entry(idx), out)
    return out


def _regroup_kernel(src_ref, lr_src_ref, main_ref, lr_ref):
    scale = _chunk_table(pl.program_id(0), 1).astype(F32)
    main_ref[...] = (src_ref[...] * scale).T.astype(BF16)
    rank2 = W_IN_LR[1] - W_IN_LR[0]
    lr_rows = jnp.concatenate([lr_src_ref[...], jnp.zeros((LR_PAD - rank2, lr_src_ref.shape[1]), F32)], axis=0)
    lr_ref[...] = lr_rows.T.astype(BF16)


def _regroup_w_in(w_t):
    n_in, d = w_t.shape
    rank2 = W_IN_LR[1] - W_IN_LR[0]
    n_chunks = len(W_IN_CHUNKS)
    return pl.pallas_call(
        _regroup_kernel,
        grid=(len(W_IN_CHUNKS),),
        in_specs=[pl.BlockSpec((pl.Element(PROJ_CHUNK), pl.Element(d)),
                               lambda j: (_chunk_table(j, 0, rank2) * rank2, 0)),
                  pl.BlockSpec((rank2, d), lambda j: (W_IN_LR[0] // rank2, 0))],
        out_specs=[pl.BlockSpec((None, d, PROJ_CHUNK), lambda j: (j, 0, 0)),
                   pl.BlockSpec((d, LR_PAD), lambda j: (0, 0))],
        out_shape=[jax.ShapeDtypeStruct((n_chunks, d, PROJ_CHUNK), BF16),
                   jax.ShapeDtypeStruct((d, LR_PAD), BF16)],
        compiler_params=pltpu.CompilerParams(vmem_limit_bytes=VMEM_LIMIT),
        name="regroup_w_in",
    )(w_t, w_t)


def _head_norm(a, gain):
    ms = jnp.mean(a * a, axis=-1, keepdims=True)
    return a * lax.rsqrt(ms + EPS) * gain


def _rope(n, cos, sin_signed):
    half, quarter = HEAD_DIM // 2, HEAD_DIM // 4
    lane = lax.broadcasted_iota(jnp.int32, n.shape, 1)
    partner = jnp.where((lane % half) < quarter, pltpu.roll(n, HEAD_DIM - quarter, 1), pltpu.roll(n, quarter, 1))
    return n * cos + partner * sin_signed


def _log2_decay(lowrank, w, b):
    z = _dot(lowrank.astype(BF16), w) + b
    return (jnp.minimum(z, 0.0) - jnp.log(1.0 + jnp.exp(-jnp.abs(z)))) * (LOG2_E / GLA_GATE_NORM)


def _inproj_kernel(x_ref, sh_ref, sc_ref, n1_ref, w_ref, wlr_ref, wgk_ref, bgk_ref, cos_ref, sin_ref,
                   qg_ref, kg_ref, main_ref, g_ref, *, n_cols, rope):
    half = x_ref.shape[0] // PROJ_ROW_SPLIT
    g_slices = min(DECAY_SLICES, n_cols // PROJ_CHUNK)
    g_width = g_ref.shape[1] // g_slices
    for part in range(PROJ_ROW_SPLIT):
        r = slice(part * half, (part + 1) * half)
        x = x_ref[r, :]
        ms = jnp.mean(x * x, axis=-1, keepdims=True)
        h = x * lax.rsqrt(ms + EPS) * n1_ref[...]
        hb = (h * (1.0 + sc_ref[...]) + sh_ref[...]).astype(BF16)

        def qk_head(a, gain):
            n = _head_norm(a, gain)
            if rope:
                n = _rope(n, cos_ref[r, :], sin_ref[r, :])
            return n.astype(BF16)

        n_chunks = n_cols // PROJ_CHUNK
        ahead = _dot(hb, w_ref[0])
        for c in range(n_chunks):
            lo = c * PROJ_CHUNK
            acc, ahead = ahead, (_dot(hb, w_ref[c + 1]) if c + 1 < n_chunks else None)
            for s in range(PROJ_CHUNK // HEAD_DIM):
                col = lo + s * HEAD_DIM
                a = acc[:, s * HEAD_DIM:(s + 1) * HEAD_DIM]
                if COL_AK <= col < COL_AV:
                    main_ref[r, col:col + HEAD_DIM] = qk_head(a, kg_ref[...])
                elif COL_AQ <= col < COL_GO:
                    main_ref[r, col:col + HEAD_DIM] = qk_head(a, qg_ref[...])
                elif COL_GO <= col < COL_GA:
                    main_ref[r, col:col + HEAD_DIM] = (a * _sigmoid(a)).astype(BF16)
                elif COL_GA <= col < COL_GQ:
                    main_ref[r, col:col + HEAD_DIM] = _sigmoid(a).astype(BF16)
                else:
                    main_ref[r, col:col + HEAD_DIM] = a.astype(BF16)
            if c < g_slices:
                if c == 0:
                    lowrank = _dot(hb, wlr_ref[...])
                gs = slice(c * g_width, (c + 1) * g_width)
                g_ref[r, gs] = _log2_decay(lowrank, wgk_ref[:, gs], bgk_ref[:, gs]).astype(BF16)


def _inproj(x2d, mod4, mod_row_of_tile, norm1, w_main, w_lr, w_gk, b_gk, cos_t, sin_t, q_gain, k_gain,
            *, n_cols, rope, tm, seq_tiles):
    rows, d = x2d.shape
    n_g = w_gk.shape[1]
    kern = functools.partial(_inproj_kernel, n_cols=n_cols, rope=rope)
    return pl.pallas_call(
        kern,
        grid=(rows // tm,),
        in_specs=[
            pl.BlockSpec((tm, d), lambda i: (i, 0)),
            pl.BlockSpec((None, None, 1, d), lambda i: (mod_row_of_tile(i), 0, 0, 0)),
            pl.BlockSpec((None, None, 1, d), lambda i: (mod_row_of_tile(i), 1, 0, 0)),
            _resident((1, d), lambda i: (0, 0)),
            _resident((n_cols // PROJ_CHUNK, d, PROJ_CHUNK), lambda i: (0, 0, 0)),
            _resident((d, LR_PAD), lambda i: (0, 0)),
            _resident((LR_PAD, n_g), lambda i: (0, 0)),
            _resident((1, n_g), lambda i: (0, 0)),
            pl.BlockSpec((tm, HEAD_DIM), lambda i: (i % seq_tiles, 0)),
            pl.BlockSpec((tm, HEAD_DIM), lambda i: (i % seq_tiles, 0)),
            _resident((1, HEAD_DIM), lambda i: (0, 0)),
            _resident((1, HEAD_DIM), lambda i: (0, 0)),
        ],
        out_specs=[pl.BlockSpec((tm, n_cols), lambda i: (i, 0)),
                   pl.BlockSpec((tm, n_g), lambda i: (i, 0))],
        out_shape=[jax.ShapeDtypeStruct((rows, n_cols), BF16),
                   jax.ShapeDtypeStruct((rows, n_g), BF16)],
        compiler_params=pltpu.CompilerParams(vmem_limit_bytes=VMEM_LIMIT),
        name="inproj_rope" if rope else "inproj_ctx",
    )(x2d, mod4, mod4, norm1, w_main, w_lr, w_gk, b_gk, cos_t, sin_t, q_gain, k_gain)


def _tri(n, upper):
    r = lax.broadcasted_iota(jnp.int32, (n, n), 0)
    c = lax.broadcasted_iota(jnp.int32, (n, n), 1)
    return (c >= r) if upper else (r >= c)


def _running_sums(tri, g):
    return _dot(tri.astype(BF16), g)


def _glastate_kernel(k_ref, v_ref, gf_ref, gb_ref, sf_ref, sb_ref, *, n):
    k = k_ref[...].astype(F32)
    b = _running_sums(_tri(n, False), gf_ref[...])
    k_fwd = (k * jnp.exp2(b[n - 1:n] - b)).astype(BF16)
    e = _running_sums(_tri(n, True), gb_ref[...])
    k_bwd = (k * jnp.exp2(e[0:1] - e)).astype(BF16)
    for h in range(GLA_HEADS):
        v = v_ref[:, h * GLA_DV:(h + 1) * GLA_DV]
        kc = slice(h * GLA_DK, (h + 1) * GLA_DK)
        sf_ref[h] = _dot_tn(v, k_fwd[:, kc])
        sb_ref[h] = _dot_tn(v, k_bwd[:, kc])


def _glastate(ctx_main, ctx_g, *, batch, n_ctx):
    kw, vw = GLA_HEADS * GLA_DK, GLA_HEADS * GLA_DV
    st = jax.ShapeDtypeStruct((batch, GLA_HEADS, GLA_DV, GLA_DK), F32)
    st_spec = pl.BlockSpec((None, GLA_HEADS, GLA_DV, GLA_DK), lambda b: (b, 0, 0, 0))
    return pl.pallas_call(
        functools.partial(_glastate_kernel, n=n_ctx),
        grid=(batch,),
        in_specs=[pl.BlockSpec((n_ctx, kw), lambda b: (b, COL_GK // kw)),
                  pl.BlockSpec((n_ctx, vw), lambda b: (b, COL_GV // vw)),
                  pl.BlockSpec((n_ctx, kw), lambda b: (b, 0)),
                  pl.BlockSpec((n_ctx, kw), lambda b: (b, 1))],
        out_specs=[st_spec, st_spec],
        out_shape=[st, st],
        compiler_params=pltpu.CompilerParams(vmem_limit_bytes=VMEM_LIMIT),
        name="gla_ctx_state",
    )(ctx_main, ctx_main, ctx_g, ctx_g)


GLA_CHUNK = 64
GLA_TBLOCK = 4096
GLA_WAVE = 1
GLA_AHEAD = 4


GLA_FACTORED_MAX_LOG2 = 120.0
GLA_MASKED_EXPONENT = -1e30


def _gla_kernel(qf_ref, kf_ref, vf_ref, gf_ref, qb_ref, kb_ref, vb_ref, gb_ref,
                sf0_ref, sb0_ref, of_ref, ob_ref, stf, stb, row_sc):
    @pl.when(pl.program_id(2) == 0)
    def _():
        stf[...] = sf0_ref[...]
        stb[...] = sb0_ref[...]

    c, n = GLA_CHUNK, qf_ref.shape[0] // GLA_CHUNK
    streams = ((gf_ref, qf_ref, kf_ref, vf_ref, of_ref, False),
               (gb_ref, qb_ref, kb_ref, vb_ref, ob_ref, True))

    states, worst = _gla_factored(streams, stf, stb)
    factorable = worst <= GLA_FACTORED_MAX_LOG2

    @pl.when(factorable)
    def _():
        stf[...] = states[0]
        stb[...] = states[1]

    @pl.when(jnp.logical_not(factorable))
    def _():
        _gla_unfactored(streams, stf, stb, row_sc)


def _gla_unfactored(streams, stf, stb, row_sc):
    c = GLA_CHUNK
    ones = jnp.ones((SUBLANES, GLA_DK), BF16)
    key_idx = lax.broadcasted_iota(jnp.int32, (c, GLA_DK), 0)
    for (g_ref, q_ref, k_ref, v_ref, o_ref, upper), st_ref in zip(streams, (stf, stb)):
        n = q_ref.shape[0] // c
        tri = _tri(c, upper)
        edge = 0 if upper else c - 1

        def chunk(step, carry, g_ref=g_ref, q_ref=q_ref, k_ref=k_ref, v_ref=v_ref, o_ref=o_ref,
                  upper=upper, st_ref=st_ref, n=n, tri=tri, edge=edge):
            i = (n - 1 - step) if upper else step
            r = pl.ds(pl.multiple_of(i * c, c), c)
            b = _running_sums(tri, g_ref[r, :])
            q, k, v = q_ref[r, :].astype(F32), k_ref[r, :].astype(F32), v_ref[r, :]
            st = st_ref[...]
            inter = _dot_nt((q * jnp.exp2(b)).astype(BF16), st.astype(BF16))
            for row in range(c):
                valid = (key_idx >= row) if upper else (key_idx <= row)
                e = jnp.exp2(jnp.where(valid, b[row:row + 1, :] - b, GLA_MASKED_EXPONENT))
                w = (k * e * q[row:row + 1, :]).astype(BF16)
                a_row = _dot_nt(ones, w)
                row_sc[row:row + 1, :] = _dot(a_row.astype(BF16), v)[0:1, :]
            o_ref[r, :] = (row_sc[...] + inter).astype(BF16)
            k_out = (k * jnp.exp2(b[edge:edge + 1, :] - b)).astype(BF16)
            st_ref[...] = st * jnp.exp2(b[edge:edge + 1, :]) + _dot_tn(v, k_out)
            return carry

        lax.fori_loop(0, n, chunk, 0)


def _gla_factored(streams, stf, stb):
    c, n = GLA_CHUNK, streams[0][1].shape[0] // GLA_CHUNK
    tri, rows, q, k, v, b, b_mid, b_edge = [], [], [], [], [], [], [], []
    for g_ref, q_ref, k_ref, v_ref, _, upper in streams:
        t = _tri(c, upper)
        edge = 0 if upper else c - 1
        bw = _running_sums(t, jnp.concatenate([g_ref[i * c:(i + 1) * c, :] for i in range(n)], axis=1))
        for i in range(n):
            r = slice(i * c, (i + 1) * c)
            x = bw[:, i * GLA_DK:(i + 1) * GLA_DK]
            tri.append(t), rows.append(r), b.append(x)
            b_mid.append(x[c // 2:c // 2 + 1]), b_edge.append(x[edge:edge + 1])
            q.append(q_ref[r, :]), k.append(k_ref[r, :]), v.append(v_ref[r, :])
    def stage(w0):
        u = [(d, d * n + (n - 1 - step if streams[d][5] else step))
             for step in range(w0, w0 + GLA_WAVE) for d in range(len(streams))]
        qt = {j: q[j].astype(F32) * jnp.exp2(b[j] - b_mid[j]) for _, j in u}
        kt = {j: k[j].astype(F32) * jnp.exp2(b_mid[j] - b[j]) for _, j in u}
        s = {j: _dot_nt(qt[j].astype(BF16), kt[j].astype(BF16)) for _, j in u}
        a = {j: jnp.where(tri[j], s[j], 0.0).astype(BF16) for _, j in u}
        k_out = {j: (kt[j] * jnp.exp2(b_edge[j] - b_mid[j])).astype(BF16) for _, j in u}
        upd = {j: _dot_tn(v[j], k_out[j]) for _, j in u}
        intra = {j: _dot(a[j], v[j]) for _, j in u}
        q_in = {j: (qt[j] * jnp.exp2(b_mid[j])).astype(BF16) for _, j in u}
        return u, upd, intra, q_in

    st = [stf[...], stb[...]]
    waves = list(range(0, n, GLA_WAVE))
    staged = [stage(w) for w in waves[:GLA_AHEAD]]
    for idx in range(len(waves)):
        if idx + GLA_AHEAD < len(waves):
            staged.append(stage(waves[idx + GLA_AHEAD]))
        u, upd, intra, q_in = staged.pop(0)
        for d, j in u:
            streams[d][4][rows[j], :] = (intra[j] + _dot_nt(q_in[j], st[d].astype(BF16))).astype(BF16)
            st[d] = st[d] * jnp.exp2(b_edge[j]) + upd[j]
    worst = b_edge
    while len(worst) > 1:
        worst = [jnp.minimum(x, y) for x, y in zip(worst[0::2], worst[1::2])]
    return st, -jnp.min(worst[0])


def _gla_kernel_one_block(q_ref, k_ref, v_ref, gf_ref, gb_ref, sf0_ref, sb0_ref, of_ref, ob_ref, stf, stb, row_sc):
    _gla_kernel(q_ref, k_ref, v_ref, gf_ref, q_ref, k_ref, v_ref, gb_ref, sf0_ref, sb0_ref, of_ref, ob_ref,
                stf, stb, row_sc)


def _gla(main, g, sf0, sb0, *, batch, seq):
    tb = min(GLA_TBLOCK, seq)
    nt = seq // tb
    qb_, kb_, vb_ = COL_GQ // GLA_DK, COL_GK // GLA_DK, COL_GV // GLA_DV

    def fwd(col):
        return lambda b, h, t: (b * nt + t, col + h)

    def bwd(col):
        return lambda b, h, t: (b * nt + nt - 1 - t, col + h)

    def seq_specs(m, g_col):
        return [pl.BlockSpec((tb, GLA_DK), m(qb_)), pl.BlockSpec((tb, GLA_DK), m(kb_)),
                pl.BlockSpec((tb, GLA_DV), m(vb_)), pl.BlockSpec((tb, GLA_DK), m(g_col))]

    st_spec = pl.BlockSpec((None, None, GLA_DV, GLA_DK), lambda b, h, t: (b, h, 0, 0))
    out = jax.ShapeDtypeStruct((batch * seq, GLA_HEADS * GLA_DV), BF16)
    if nt == 1:
        kern = _gla_kernel_one_block
        in_specs = seq_specs(fwd, 0) + [pl.BlockSpec((tb, GLA_DK), bwd(GLA_HEADS)), st_spec, st_spec]
        operands = (main, main, main, g, g, sf0, sb0)
    else:
        kern = _gla_kernel
        in_specs = seq_specs(fwd, 0) + seq_specs(bwd, GLA_HEADS) + [st_spec, st_spec]
        operands = (main, main, main, g, main, main, main, g, sf0, sb0)
    return pl.pallas_call(
        kern,
        grid=(batch, GLA_HEADS, nt),
        in_specs=in_specs,
        out_specs=[pl.BlockSpec((tb, GLA_DV), fwd(0)), pl.BlockSpec((tb, GLA_DV), bwd(0))],
        out_shape=[out, out],
        scratch_shapes=[pltpu.VMEM((GLA_DV, GLA_DK), F32), pltpu.VMEM((GLA_DV, GLA_DK), F32),
                        pltpu.VMEM((GLA_CHUNK, GLA_DV), F32)],
        compiler_params=pltpu.CompilerParams(
            dimension_semantics=("arbitrary", "arbitrary", "arbitrary"), vmem_limit_bytes=VMEM_LIMIT),
        name="gla",
    )(*operands)


ATT_TQ = 1024
ATT_TK = 512


ATT_UNSHIFTED_MAX_LOG2 = 56.0


def _attn_kernel(bound_ref, q_ref, k_ref, v_ref, kc_ref, vc_ref, o_ref, vt_sc, vct_sc, m_sc, l_sc, acc_sc, *, seq):
    tq, tk, hd = ATT_TQ, ATT_TK, HEAD_DIM
    nk = seq // tk

    def transposed(v):
        return v.astype(F32).T.astype(BF16)

    @pl.when(pl.program_id(2) == 0)
    def _():
        for j in range(nk):
            vt_sc[j] = transposed(v_ref[j * tk:(j + 1) * tk, :])
        vct_sc[...] = transposed(vc_ref[...])

    q4 = jnp.concatenate([q_ref[:, g * hd:(g + 1) * hd] for g in range(Q_PER_KV)], axis=0)

    def key_sums(e):
        return jnp.sum(e.reshape(e.shape[0] // SUBLANES, SUBLANES, e.shape[1]), axis=0)

    def finish(acc, l):
        o = (acc / jnp.sum(l, axis=0, keepdims=True)).T
        for g in range(Q_PER_KV):
            o_ref[:, g * hd:(g + 1) * hd] = o[g * tq:(g + 1) * tq].astype(BF16)

    small = bound_ref[0, 0] <= ATT_UNSHIFTED_MAX_LOG2

    @pl.when(small)
    def _():
        chunks = [(k_ref[j * tk:(j + 1) * tk, :], vt_sc[j]) for j in range(nk)] + [(kc_ref[...], vct_sc[...])]
        acc = l = None
        for kc, vt in chunks:
            e = jnp.exp2(_dot_nt(kc, q4))
            pv, ks = _dot(vt, e.astype(BF16)), key_sums(e)
            acc, l = (pv, ks) if acc is None else (acc + pv, l + ks)
        finish(acc, l)

    @pl.when(jnp.logical_not(small))
    def _():
        m_sc[...] = jnp.full(m_sc.shape, -jnp.inf, F32)
        acc_sc[...] = jnp.zeros(acc_sc.shape, F32)
        l_sc[...] = jnp.zeros(l_sc.shape, F32)

        def online_step(kc, vt):
            s = _dot_nt(kc, q4)
            m_prev = m_sc[...]
            m_new = jnp.maximum(m_prev, jnp.max(s, axis=0, keepdims=True))
            e = jnp.exp2(s - m_new)
            rescale = jnp.exp2(m_prev - m_new)
            acc_sc[...] = rescale * acc_sc[...] + _dot(vt, e.astype(BF16))
            l_sc[...] = rescale * l_sc[...] + key_sums(e)
            m_sc[...] = m_new

        def body(j, carry):
            online_step(k_ref[pl.ds(pl.multiple_of(j * tk, tk), tk), :], vt_sc[j])
            return carry

        lax.fori_loop(0, nk, body, 0)
        online_step(kc_ref[...], vct_sc[...])
        finish(acc_sc[...], l_sc[...])


def _attn(score_bound, main, ctx_main, *, batch, seq, n_ctx):
    nq = seq // ATT_TQ
    gw = Q_PER_KV * HEAD_DIM
    rows = Q_PER_KV * ATT_TQ
    vt_rows = HEAD_DIM
    return pl.pallas_call(
        functools.partial(_attn_kernel, seq=seq),
        grid=(batch, N_KV_HEADS, nq),
        in_specs=[pl.BlockSpec(memory_space=pltpu.SMEM),
                  pl.BlockSpec((ATT_TQ, gw), lambda b, h, i: (b * nq + i, COL_AQ // gw + h)),
                  pl.BlockSpec((seq, HEAD_DIM), lambda b, h, i: (b, COL_AK // HEAD_DIM + h)),
                  pl.BlockSpec((seq, HEAD_DIM), lambda b, h, i: (b, COL_AV // HEAD_DIM + h)),
                  pl.BlockSpec((n_ctx, HEAD_DIM), lambda b, h, i: (b, COL_AK // HEAD_DIM + h)),
                  pl.BlockSpec((n_ctx, HEAD_DIM), lambda b, h, i: (b, COL_AV // HEAD_DIM + h))],
        out_specs=pl.BlockSpec((ATT_TQ, gw), lambda b, h, i: (b * nq + i, h)),
        out_shape=jax.ShapeDtypeStruct((batch * seq, N_Q_HEADS * HEAD_DIM), BF16),
        scratch_shapes=[pltpu.VMEM((seq // ATT_TK, vt_rows, ATT_TK), BF16),
                        pltpu.VMEM((vt_rows, n_ctx), BF16),
                        pltpu.VMEM((1, rows), F32), pltpu.VMEM((SUBLANES, rows), F32),
                        pltpu.VMEM((vt_rows, rows), F32)],
        compiler_params=pltpu.CompilerParams(
            dimension_semantics=("arbitrary", "arbitrary", "arbitrary"), vmem_limit_bytes=VMEM_LIMIT),
        name="attn",
    )(score_bound, main, main, main, ctx_main, ctx_main)


FF_CHUNK = 1024


def _tail_kernel(x_ref, attn_ref, of_ref, ob_ref, go_ref, ga_ref, gg_ref, gn_ref, gt1_ref,
                 wa_ref, wg_ref, wo_ref, sh_ref, sc_ref, gt2_ref, n2_ref, w1_ref, w2_ref, o_ref, *, d_ff):
    gn = gn_ref[...]
    heads = []
    for h in range(GLA_HEADS):
        s = slice(h * GLA_DV, (h + 1) * GLA_DV)
        o = of_ref[:, s].astype(F32) + ob_ref[:, s].astype(F32)
        heads.append((_head_norm(o, gn) * go_ref[:, s].astype(F32)).astype(BF16))
    gla = jnp.concatenate(heads, axis=-1)
    ya = ga_ref[...].astype(F32) * _dot(attn_ref[...], wa_ref[...])
    yg = gg_ref[...].astype(F32) * _dot(gla, wg_ref[...])
    x1 = x_ref[...] + gt1_ref[...] * _dot((ya + yg).astype(BF16), wo_ref[...])
    ms = jnp.mean(x1 * x1, axis=-1, keepdims=True)
    h2 = x1 * lax.rsqrt(ms + EPS) * n2_ref[...]
    hb = (h2 * (1.0 + sc_ref[...]) + sh_ref[...]).astype(BF16)
    acc = jnp.zeros(x1.shape, F32)
    for c in range(d_ff // FF_CHUNK):
        s = slice(c * FF_CHUNK, (c + 1) * FF_CHUNK)
        u = jnp.maximum(_dot(hb, w1_ref[:, s]), 0.0)
        acc = acc + _dot((u * u).astype(BF16), w2_ref[s, :])
    o_ref[...] = x1 + gt2_ref[...] * acc


def _tail(x2d, attn, o_f, o_b, main, gla_norm, mod4, wa, wg, wo, norm2, w1, w2, *, tm, seq_tiles):
    rows, d = x2d.shape
    d_ff = w1.shape[1]
    tile = pl.BlockSpec((tm, d), lambda i: (i, 0))
    mod = lambda g: pl.BlockSpec((None, None, 1, d), lambda i: (i // seq_tiles, g, 0, 0))
    const = lambda shape: _resident(shape, lambda i: (0, 0))
    return pl.pallas_call(
        functools.partial(_tail_kernel, d_ff=d_ff),
        grid=(rows // tm,),
        in_specs=[tile, tile, tile, tile,
                  pl.BlockSpec((tm, d), lambda i: (i, COL_GO // d)),
                  pl.BlockSpec((tm, d), lambda i: (i, COL_GA // d)),
                  pl.BlockSpec((tm, d), lambda i: (i, COL_GG // d)),
                  const((1, GLA_DV)), mod(2), const((d, d)), const((d, d)), const((d, d)),
                  mod(3), mod(4), mod(5), const((1, d)), const((d, d_ff)), const((d_ff, d))],
        out_specs=tile,
        out_shape=jax.ShapeDtypeStruct((rows, d), F32),
        compiler_params=pltpu.CompilerParams(vmem_limit_bytes=VMEM_LIMIT),
        name="merge_out_mlp",
    )(x2d, attn, o_f, o_b, main, main, main, gla_norm, mod4, wa, wg, wo, mod4, mod4, mod4, norm2, w1, w2)


def _rope_tables(seq):
    t = np.arange(seq)
    half = HEAD_DIM // 2
    freqs = ROPE_THETA ** (-np.arange(0, half, 2, dtype=np.float32) / half)
    ang_r = (t // GRID_W).astype(np.float32)[:, None] * freqs
    ang_c = (t % GRID_W).astype(np.float32)[:, None] * freqs
    cos = np.concatenate([np.cos(ang_r)] * 2 + [np.cos(ang_c)] * 2, axis=-1)
    sin = np.concatenate([-np.sin(ang_r), np.sin(ang_r), -np.sin(ang_c), np.sin(ang_c)], axis=-1)
    return jnp.asarray(cos, F32), jnp.asarray(sin, F32)


def _pad_rows(w, row0):
    return jnp.pad(w, ((row0, LR_PAD - row0 - w.shape[0]), (0, 0)))


def kernel(x, c, ctx, c_ctx, w_ada, b_ada, norm1, w_in, q_norm, k_norm, w_gk_fwd, b_gk_fwd,
           w_gk_bwd, b_gk_bwd, gla_norm, w_br_attn, w_br_gla, w_out, norm2, w_mlp1, w_mlp2):
    batch, seq, d = x.shape
    n_ctx = ctx.shape[1]
    depth = w_ada.shape[0]
    assert depth == 1 and batch < ADA_ROWS
    assert seq % min(GLA_TBLOCK, seq) == 0 and seq % ATT_TK == 0 and seq % ATT_TQ == 0 and seq % TOKEN_TILE == 0
    tm = TOKEN_TILE
    seq_tiles = seq // tm
    ctx_tm = n_ctx
    l = 0

    cc = jnp.zeros((ADA_ROWS, d), F32).at[:batch].set(c).at[batch].set(c_ctx)
    mod4 = _ada(cc, w_ada[l], b_ada[l][None, :]).reshape(ADA_ROWS, 6, 1, d)

    w_main, w_lr = _regroup_w_in(jnp.transpose(w_in[l]))
    cos_t, sin_t = _rope_tables(seq)
    q_scale = (HEAD_DIM ** -0.5) * float(np.log2(np.e))
    q_gain = (q_norm[l] * q_scale)[None, :]
    k_gain = k_norm[l][None, :]
    score_bound = (1.01 * HEAD_DIM * q_scale * jnp.max(jnp.abs(q_norm[l])) * jnp.max(jnp.abs(k_norm[l])))
    score_bound = score_bound.reshape(1, 1).astype(F32)
    n1 = norm1[l][None, :]

    w_gk = jnp.concatenate([_pad_rows(w_gk_fwd[l], 0), _pad_rows(w_gk_bwd[l], GLA_GATE_RANK)], axis=1).astype(BF16)
    b_gk = jnp.concatenate([b_gk_fwd[l], b_gk_bwd[l]])[None, :]

    main, g = _inproj(x.reshape(batch * seq, d), mod4, lambda i: i // seq_tiles, n1, w_main, w_lr, w_gk, b_gk,
                      cos_t, sin_t, q_gain, k_gain, n_cols=N_MAIN, rope=True, tm=tm, seq_tiles=seq_tiles)
    ctx_main, ctx_g = _inproj(ctx.reshape(batch * n_ctx, d), mod4, lambda i: batch, n1, w_main, w_lr, w_gk, b_gk,
                              cos_t, sin_t, q_gain, k_gain, n_cols=N_CTX_MAIN, rope=False, tm=ctx_tm,
                              seq_tiles=1)

    sf0, sb0 = _glastate(ctx_main, ctx_g, batch=batch, n_ctx=n_ctx)
    o_f, o_b = _gla(main, g, sf0, sb0, batch=batch, seq=seq)

    attn = _attn(score_bound, main, ctx_main, batch=batch, seq=seq, n_ctx=n_ctx)

    x2 = _tail(x.reshape(batch * seq, d), attn, o_f, o_b, main, gla_norm[l][None, :], mod4,
               w_br_attn[l].astype(BF16), w_br_gla[l].astype(BF16), w_out[l].astype(BF16),
               norm2[l][None, :], w_mlp1[l].astype(BF16), w_mlp2[l].astype(BF16),
               tm=tm, seq_tiles=seq_tiles)
    return x2.reshape(batch, seq, d)
```
